```python
import math
import jax, jax.numpy as jnp
from jax import lax
import numpy as np

D_MODEL = 1024
BATCH = 8
SEQ = 4096
DEPTH = 2

GRID_W = 64
CTX_LEN = 256
HEAD_DIM = 64
NA_HEADS = 4
NA_WIN_R = 8
NA_WIN_C = 16
GQA_Q_HEADS = 4
GQA_KV_HEADS = 2
DN_HEADS = 4
DN_CONV = 5
DN_CHUNK = 64
MLA_HEADS = 4
MLA_Q_RANK = 256
MLA_KV_RANK = 128
MLA_NOPE = 64
MLA_ROPE = 32
MLA_V = 64
MLA_QK = MLA_NOPE + MLA_ROPE
N_EXPERTS = 32
TOP_K = 4
D_EXPERT = 1024
SWIGLU_LIMIT = 7.0
SWIGLU_ALPHA = 1.702
ROPE_THETA = 10000.0
EPS = 1e-6
NEG_INF = -1e30
Q_BLOCK = 128
N_ADA = 6

NA_W = NA_HEADS * HEAD_DIM
GQA_QW = GQA_Q_HEADS * HEAD_DIM
GQA_KVW = GQA_KV_HEADS * HEAD_DIM
DN_W = DN_HEADS * HEAD_DIM
MLA_W = MLA_HEADS * MLA_V
MIX_W = NA_W + GQA_QW + DN_W + MLA_W
IN_SIZES = (NA_W, NA_W, NA_W, GQA_QW, GQA_KVW, GQA_KVW, DN_W, DN_W, DN_W, DN_W, 2 * DN_HEADS, 2 * DN_HEADS, MLA_Q_RANK, MLA_KV_RANK, MLA_ROPE)
IN_OFFSETS = tuple(int(o) for o in np.cumsum(IN_SIZES)[:-1])
IN_W = int(sum(IN_SIZES))

kernel_name = 'hybrid_natten_gqa_gdn_mla_moe_dit'


def rms_norm(x, g):
    xf = x.astype(jnp.float32)
    y = xf * lax.rsqrt(jnp.mean(jnp.square(xf), axis=-1, keepdims=True) + EPS)
    return (y * g.astype(jnp.float32)).astype(x.dtype)


def l2_norm(x):
    xf = x.astype(jnp.float32)
    return (xf * lax.rsqrt(jnp.sum(jnp.square(xf), axis=-1, keepdims=True) + EPS)).astype(x.dtype)


def modulate(h, shift, scale):
    return h * (1 + scale) + shift


def split_heads(t, n):
    return t.reshape(t.shape[0], t.shape[1], n, t.shape[2] // n)


def axial_rope_tables(n_tok, rot_dim):
    t = jnp.arange(n_tok, dtype=jnp.int32)
    row = (t // GRID_W).astype(jnp.float32)
    col = (t % GRID_W).astype(jnp.float32)
    n_freq = rot_dim // 4
    freqs = ROPE_THETA ** (-jnp.arange(n_freq, dtype=jnp.float32) / n_freq)
    ang = jnp.concatenate([row[:, None] * freqs, col[:, None] * freqs], axis=-1)
    return jnp.cos(ang), jnp.sin(ang)


def apply_rope(x, cos, sin):
    xf = x.astype(jnp.float32).reshape(*x.shape[:-1], -1, 2)
    x1, x2 = xf[..., 0], xf[..., 1]
    cs, sn = cos[:, None, :], sin[:, None, :]
    out = jnp.stack([x1 * cs - x2 * sn, x1 * sn + x2 * cs], axis=-1)
    return out.reshape(x.shape).astype(x.dtype)


def rope_tail(t, cos, sin):
    return jnp.concatenate([t[..., :MLA_NOPE], apply_rope(t[..., MLA_NOPE:], cos, sin)], axis=-1)


def blocked_attention(q, k, v):
    B, N, G, R, dk = q.shape
    nb = N // Q_BLOCK
    scale = dk ** -0.5
    qb = jnp.moveaxis(q.reshape(B, nb, Q_BLOCK, G, R, dk), 1, 0)

    def one_block(qi):
        s = jnp.einsum('bqgrd,bkgd->bgrqk', qi, k).astype(jnp.float32) * scale
        p = jax.nn.softmax(s, axis=-1).astype(v.dtype)
        return jnp.einsum('bgrqk,bkge->bqgre', p, v)

    o = lax.map(one_block, qb)
    return jnp.moveaxis(o, 0, 1).reshape(B, N, G, R, v.shape[-1])


def neighbourhood_attention(q, k, v, kc, vc, rel_bias):
    B, N, H, d = q.shape
    rows = N // GRID_W
    kr = min(NA_WIN_R, rows)
    scale = d ** -0.5
    qg = q.reshape(B, rows, GRID_W, H, d)
    kg = k.reshape(B, rows, GRID_W, H, d)
    vg = v.reshape(B, rows, GRID_W, H, d)
    r = jnp.arange(rows)
    r0 = jnp.clip(r - kr // 2, 0, rows - kr)
    ridx = r0[:, None] + jnp.arange(kr)[None, :]
    kw = kg[:, ridx]
    vw = vg[:, ridx]
    cpos = jnp.arange(GRID_W)
    c0 = jnp.clip(cpos - NA_WIN_C // 2, 0, GRID_W - NA_WIN_C)
    col_ok = (cpos[None, :] >= c0[:, None]) & (cpos[None, :] < c0[:, None] + NA_WIN_C)
    dr = ridx - r[:, None] + (NA_WIN_R - 1)
    dc = jnp.clip(cpos[None, :] - cpos[:, None], -(NA_WIN_C - 1), NA_WIN_C - 1) + (NA_WIN_C - 1)
    bias = rel_bias[:, dr[:, None, :, None], dc[None, :, None, :]]
    s_loc = jnp.einsum('brchd,brkwhd->bhrckw', qg, kw).astype(jnp.float32) * scale + bias.astype(jnp.float32)
    s_loc = jnp.where(col_ok[:, None, :], s_loc, NEG_INF)
    s_ctx = jnp.einsum('brchd,blhd->bhrcl', qg, kc).astype(jnp.float32) * scale
    s = jnp.concatenate([s_loc.reshape(B, H, rows, GRID_W, kr * GRID_W), s_ctx], axis=-1)
    p = jax.nn.softmax(s, axis=-1).astype(v.dtype)
    p_loc = p[..., :kr * GRID_W].reshape(B, H, rows, GRID_W, kr, GRID_W)
    p_ctx = p[..., kr * GRID_W:]
    o = jnp.einsum('bhrckw,brkwhd->brchd', p_loc, vw) + jnp.einsum('bhrcl,blhd->brchd', p_ctx, vc)
    return o.reshape(B, N, H, d)


def short_conv(x, w):
    C = x.shape[-1]
    return lax.conv_general_dilated(x, w[:, None, :].astype(x.dtype), window_strides=(1,),
                                    padding=[(DN_CONV // 2, DN_CONV // 2)],
                                    dimension_numbers=('NWC', 'WIO', 'NWC'), feature_group_count=C)


def dn_inputs(q, k, v, beta_raw, a_raw, conv_w, a_log, dt_bias):
    B, N, _ = q.shape
    qkv = jax.nn.silu(short_conv(jnp.concatenate([q, k, v], axis=-1), conv_w))
    q, k, v = [split_heads(t, DN_HEADS) for t in jnp.split(qkv, 3, axis=-1)]
    beta = jax.nn.sigmoid(beta_raw.astype(jnp.float32)).reshape(B, N, 2, DN_HEADS)
    g = -jnp.exp(a_log.astype(jnp.float32)) * jax.nn.softplus(
        a_raw.astype(jnp.float32).reshape(B, N, 2, DN_HEADS) + dt_bias.astype(jnp.float32))
    return l2_norm(q), l2_norm(k), v, g, beta


def gated_delta_chunked(q, k, v, g, beta, s0):
    B, N, H, dk = q.shape
    dv = v.shape[-1]
    nc = N // DN_CHUNK

    def chunks(t):
        t = t.astype(jnp.float32).reshape(B, nc, DN_CHUNK, H, *t.shape[3:])
        return jnp.moveaxis(jnp.moveaxis(t, 1, 0), 3, 2)

    qch = chunks(q) * dk ** -0.5
    kch = chunks(k)
    vch = chunks(v)
    gc = jnp.cumsum(chunks(g), axis=-1)
    bch = chunks(beta)
    i = jnp.arange(DN_CHUNK)
    lower = i[:, None] >= i[None, :]
    strict = i[:, None] > i[None, :]
    decay = jnp.exp(jnp.where(lower, gc[..., :, None] - gc[..., None, :], NEG_INF))
    kb = kch * bch[..., None]
    a = jnp.where(strict, jnp.einsum('nbhid,nbhjd->nbhij', kb, kch) * decay, 0.0)
    eye = jnp.eye(DN_CHUNK, dtype=jnp.float32)
    t_inv = lax.linalg.triangular_solve(eye + a, jnp.broadcast_to(eye, a.shape), left_side=True,
                                        lower=True, unit_diagonal=True)
    u = jnp.einsum('nbhij,nbhje->nbhie', t_inv, vch * bch[..., None])
    w = jnp.einsum('nbhij,nbhjd->nbhid', t_inv, kb * jnp.exp(gc)[..., None])
    attn = jnp.where(lower, jnp.einsum('nbhid,nbhjd->nbhij', qch, kch) * decay, 0.0)

    def step(S, inp):
        qi, ki, ui, wi, gi, ai = inp
        v_new = ui - jnp.einsum('bhcd,bhde->bhce', wi, S)
        o = jnp.einsum('bhcd,bhde->bhce', qi * jnp.exp(gi)[..., None], S) + jnp.einsum('bhij,bhje->bhie', ai, v_new)
        g_last = gi[..., -1]
        k_dec = ki * jnp.exp(g_last[..., None] - gi)[..., None]
        S = S * jnp.exp(g_last)[..., None, None] + jnp.einsum('bhcd,bhce->bhde', k_dec, v_new)
        return S, o

    s_final, o = lax.scan(step, s0.astype(jnp.float32), (qch, kch, u, w, gc, attn))
    o = jnp.moveaxis(jnp.moveaxis(o, 2, 3), 0, 1).reshape(B, N, H, dv)
    return o.astype(v.dtype), s_final


def bidir_scan(q, k, v, g, beta, s0_f, s0_b):
    o_f, s_f = gated_delta_chunked(q, k, v, g[:, :, 0], beta[:, :, 0], s0_f)
    o_b, s_b = gated_delta_chunked(jnp.flip(q, 1), jnp.flip(k, 1), jnp.flip(v, 1),
                                   jnp.flip(g[:, :, 1], 1), jnp.flip(beta[:, :, 1], 1), s0_b)
    return o_f + jnp.flip(o_b, 1), s_f, s_b


def mla_queries(cq, cq_g, w_uq, qn_g):
    B, N, _ = cq.shape
    q = (rms_norm(cq, cq_g) @ w_uq).reshape(B, N, MLA_HEADS, MLA_QK)
    return rms_norm(q, qn_g)


def mla_keys_values(ckv, kpe, ckv_g, w_ukv, kn_g):
    B, N, _ = ckv.shape
    kv = (rms_norm(ckv, ckv_g) @ w_ukv).reshape(B, N, MLA_HEADS, MLA_NOPE + MLA_V)
    k_pe = jnp.broadcast_to(kpe[:, :, None, :], (B, N, MLA_HEADS, MLA_ROPE))
    k = jnp.concatenate([kv[..., :MLA_NOPE], k_pe], axis=-1)
    return rms_norm(k, kn_g), kv[..., MLA_NOPE:]


def moe_ffn(t, router_w, router_b, w1, b1, w2, b2):
    logits = (t @ router_w + router_b).astype(jnp.float32)
    top_val, top_idx = lax.top_k(logits, TOP_K)
    gates = jax.nn.softmax(top_val, axis=-1)
    combine = jnp.sum(jax.nn.one_hot(top_idx, N_EXPERTS, dtype=jnp.float32) * gates[..., None], axis=1).astype(t.dtype)
    out = jnp.zeros_like(t)
    for e in range(N_EXPERTS):
        hh = t @ w1[e] + b1[e]
        glu = jnp.minimum(hh[:, :D_EXPERT], SWIGLU_LIMIT)
        lin = jnp.clip(hh[:, D_EXPERT:], -SWIGLU_LIMIT, SWIGLU_LIMIT)
        act = glu * jax.nn.sigmoid(SWIGLU_ALPHA * glu) * (lin + 1)
        out = out + combine[:, e:e + 1] * (act @ w2[e] + b2[e])
    return out


def setup_inputs(seed: int = 0) -> dict:
    key = jax.random.key(seed)
    ks = jax.random.split(key, 40)
    f32 = jnp.float32
    D = D_MODEL

    def nrm(k, shape, s):
        return jax.random.normal(k, shape, f32) * s

    dt = jnp.exp(jax.random.uniform(ks[17], (DEPTH, 2, DN_HEADS), f32, math.log(1e-3), math.log(1e-1)))
    return {
        'x': nrm(ks[0], (BATCH, SEQ, D), 1.0),
        'c': nrm(ks[1], (BATCH, D), 1.0),
        'ctx': nrm(ks[2], (BATCH, CTX_LEN, D), 1.0),
        'c_ctx': nrm(ks[3], (D,), 1.0),
        'ada_w': nrm(ks[4], (DEPTH, D, N_ADA * D), 0.5 * D ** -0.5),
        'ada_b': nrm(ks[5], (DEPTH, N_ADA * D), 0.02),
        'norm1_g': 1.0 + nrm(ks[6], (DEPTH, D), 0.02),
        'norm2_g': 1.0 + nrm(ks[7], (DEPTH, D), 0.02),
        'w_in': nrm(ks[8], (DEPTH, D, IN_W), D ** -0.5),
        'w_out': nrm(ks[9], (DEPTH, MIX_W, D), MIX_W ** -0.5),
        'na_qn_g': 1.0 + nrm(ks[10], (DEPTH, HEAD_DIM), 0.02),
        'na_kn_g': 1.0 + nrm(ks[11], (DEPTH, HEAD_DIM), 0.02),
        'na_rel_bias': nrm(ks[12], (DEPTH, NA_HEADS, 2 * NA_WIN_R - 1, 2 * NA_WIN_C - 1), 0.1),
        'gqa_qn_g': 1.0 + nrm(ks[13], (DEPTH, HEAD_DIM), 0.02),
        'gqa_kn_g': 1.0 + nrm(ks[14], (DEPTH, HEAD_DIM), 0.02),
        'dn_conv_w': nrm(ks[15], (DEPTH, DN_CONV, 3 * DN_W), DN_CONV ** -0.5),
        'dn_a_log': jnp.log(jax.random.uniform(ks[16], (DEPTH, 2, DN_HEADS), f32, 1.0, 16.0)),
        'dn_dt_bias': dt + jnp.log(-jnp.expm1(-dt)),
        'dn_out_g': 1.0 + nrm(ks[18], (DEPTH, HEAD_DIM), 0.02),
        'mla_cq_g': 1.0 + nrm(ks[19], (DEPTH, MLA_Q_RANK), 0.02),
        'mla_ckv_g': 1.0 + nrm(ks[20], (DEPTH, MLA_KV_RANK), 0.02),
        'mla_w_uq': nrm(ks[21], (DEPTH, MLA_Q_RANK, MLA_HEADS * MLA_QK), MLA_Q_RANK ** -0.5),
        'mla_w_ukv': nrm(ks[22], (DEPTH, MLA_KV_RANK, MLA_HEADS * (MLA_NOPE + MLA_V)), MLA_KV_RANK ** -0.5),
        'mla_qn_g': 1.0 + nrm(ks[23], (DEPTH, MLA_QK), 0.02),
        'mla_kn_g': 1.0 + nrm(ks[24], (DEPTH, MLA_QK), 0.02),
        'router_w': nrm(ks[25], (DEPTH, D, N_EXPERTS), D ** -0.5),
        'router_b': nrm(ks[26], (DEPTH, N_EXPERTS), 0.01),
        'exp_w1': nrm(ks[27], (DEPTH, N_EXPERTS, D, 2 * D_EXPERT), D ** -0.5),
        'exp_b1': nrm(ks[28], (DEPTH, N_EXPERTS, 2 * D_EXPERT), 0.01),
        'exp_w2': nrm(ks[29], (DEPTH, N_EXPERTS, D_EXPERT, D), D_EXPERT ** -0.5),
        'exp_b2': nrm(ks[30], (DEPTH, N_EXPERTS, D), 0.01),
    }


def reference(x, c, ctx, c_ctx, ada_w, ada_b, norm1_g, norm2_g, w_in, w_out, na_qn_g, na_kn_g, na_rel_bias,
              gqa_qn_g, gqa_kn_g, dn_conv_w, dn_a_log, dn_dt_bias, dn_out_g, mla_cq_g, mla_ckv_g, mla_w_uq,
              mla_w_ukv, mla_qn_g, mla_kn_g, router_w, router_b, exp_w1, exp_b1, exp_w2, exp_b2):
    B, N, D = x.shape
    L = ctx.shape[1]
    grp = GQA_Q_HEADS // GQA_KV_HEADS
    cos_g, sin_g = axial_rope_tables(N, HEAD_DIM)
    cos_m, sin_m = axial_rope_tables(N, MLA_ROPE)
    sc = jax.nn.silu(c)
    scc = jax.nn.silu(c_ctx)
    s0 = jnp.zeros((B, DN_HEADS, HEAD_DIM, HEAD_DIM), jnp.float32)
    for l in range(DEPTH):
        with_ctx = l < DEPTH - 1
        mod = jnp.split(sc @ ada_w[l] + ada_b[l], N_ADA, axis=-1)
        mod_c = jnp.split(scc @ ada_w[l] + ada_b[l], N_ADA, axis=-1)
        h = modulate(rms_norm(x, norm1_g[l]), mod[0][:, None], mod[1][:, None])
        hc = modulate(rms_norm(ctx, norm1_g[l]), mod_c[0], mod_c[1])
        p = jnp.split(h @ w_in[l], IN_OFFSETS, axis=-1)
        pc = jnp.split(hc @ w_in[l], IN_OFFSETS, axis=-1)

        qa = rms_norm(split_heads(p[0], NA_HEADS), na_qn_g[l])
        ka = rms_norm(split_heads(p[1], NA_HEADS), na_kn_g[l])
        va = split_heads(p[2], NA_HEADS)
        kac = rms_norm(split_heads(pc[1], NA_HEADS), na_kn_g[l])
        vac = split_heads(pc[2], NA_HEADS)
        ya = neighbourhood_attention(qa, ka, va, kac, vac, na_rel_bias[l]).reshape(B, N, NA_W)

        qb = apply_rope(rms_norm(split_heads(p[3], GQA_Q_HEADS), gqa_qn_g[l]), cos_g, sin_g)
        kb = apply_rope(rms_norm(split_heads(p[4], GQA_KV_HEADS), gqa_kn_g[l]), cos_g, sin_g)
        vb = split_heads(p[5], GQA_KV_HEADS)
        kbc = rms_norm(split_heads(pc[4], GQA_KV_HEADS), gqa_kn_g[l])
        vbc = split_heads(pc[5], GQA_KV_HEADS)
        yb = blocked_attention(qb.reshape(B, N, GQA_KV_HEADS, grp, HEAD_DIM),
                               jnp.concatenate([kb, kbc], axis=1), jnp.concatenate([vb, vbc], axis=1)).reshape(B, N, GQA_QW)

        qdc, kdc, vdc, gdc, bdc = dn_inputs(pc[6], pc[7], pc[8], pc[10], pc[11], dn_conv_w[l], dn_a_log[l], dn_dt_bias[l])
        odc, s_f, s_b = bidir_scan(qdc, kdc, vdc, gdc, bdc, s0, s0)
        qd, kd, vd, gd, bd = dn_inputs(p[6], p[7], p[8], p[10], p[11], dn_conv_w[l], dn_a_log[l], dn_dt_bias[l])
        od, _, _ = bidir_scan(qd, kd, vd, gd, bd, s_f, s_b)
        yc = (rms_norm(od, dn_out_g[l]) * jax.nn.silu(split_heads(p[9], DN_HEADS))).reshape(B, N, DN_W)

        qm = rope_tail(mla_queries(p[12], mla_cq_g[l], mla_w_uq[l], mla_qn_g[l]), cos_m, sin_m)
        km, vm = mla_keys_values(p[13], p[14], mla_ckv_g[l], mla_w_ukv[l], mla_kn_g[l])
        km = rope_tail(km, cos_m, sin_m)
        kmc, vmc = mla_keys_values(pc[13], pc[14], mla_ckv_g[l], mla_w_ukv[l], mla_kn_g[l])
        yd = blocked_attention(qm[:, :, :, None], jnp.concatenate([km, kmc], axis=1),
                               jnp.concatenate([vm, vmc], axis=1)).reshape(B, N, MLA_W)

        x = x + mod[2][:, None] * (jnp.concatenate([ya, yb, yc, yd], axis=-1) @ w_out[l])
        h2 = modulate(rms_norm(x, norm2_g[l]), mod[3][:, None], mod[4][:, None])

        if with_ctx:
            qac = rms_norm(split_heads(pc[0], NA_HEADS), na_qn_g[l])
            yac = blocked_attention(qac[:, :, :, None], kac, vac).reshape(B, L, NA_W)
            qbc = rms_norm(split_heads(pc[3], GQA_Q_HEADS), gqa_qn_g[l])
            ybc = blocked_attention(qbc.reshape(B, L, GQA_KV_HEADS, grp, HEAD_DIM), kbc, vbc).reshape(B, L, GQA_QW)
            ycc = (rms_norm(odc, dn_out_g[l]) * jax.nn.silu(split_heads(pc[9], DN_HEADS))).reshape(B, L, DN_W)
            qmc = mla_queries(pc[12], mla_cq_g[l], mla_w_uq[l], mla_qn_g[l])
            ydc = blocked_attention(qmc[:, :, :, None], kmc, vmc).reshape(B, L, MLA_W)
            ctx = ctx + mod_c[2] * (jnp.concatenate([yac, ybc, ycc, ydc], axis=-1) @ w_out[l])
            h2c = modulate(rms_norm(ctx, norm2_g[l]), mod_c[3], mod_c[4])
            tok = jnp.concatenate([h2.reshape(B * N, D), h2c.reshape(B * L, D)], axis=0)
            f = moe_ffn(tok, router_w[l], router_b[l], exp_w1[l], exp_b1[l], exp_w2[l], exp_b2[l])
            x = x + mod[5][:, None] * f[:B * N].reshape(B, N, D)
            ctx = ctx + mod_c[5] * f[B * N:].reshape(B, L, D)
        else:
            f = moe_ffn(h2.reshape(B * N, D), router_w[l], router_b[l], exp_w1[l], exp_b1[l], exp_w2[l], exp_b2[l])
            x = x + mod[5][:, None] * f.reshape(B, N, D)
    return x
```

```python
import functools
import math

import jax
import jax.numpy as jnp
import numpy as np
from jax import lax
from jax.experimental import pallas as pl
from jax.experimental.pallas import tpu as pltpu

F32 = jnp.float32
BF16 = jnp.bfloat16
HIGHEST = lax.Precision.HIGHEST

GRID_W = 64
HEAD_DIM = 64
NA_HEADS = 4
NA_WIN_R = 8
NA_WIN_C = 16
GQA_Q_HEADS = 4
GQA_KV_HEADS = 2
DN_HEADS = 4
DN_CONV = 5
DN_CHUNK = 64
MLA_HEADS = 4
MLA_Q_RANK = 256
MLA_KV_RANK = 128
MLA_NOPE = 64
MLA_ROPE = 32
MLA_V = 64
MLA_QK = MLA_NOPE + MLA_ROPE
N_EXPERTS = 32
TOP_K = 4
SWIGLU_LIMIT = 7.0
SWIGLU_ALPHA = 1.702
ROPE_THETA = 10000.0
EPS = 1e-6
NEG_INF = -1e30
N_ADA = 6

NA_W = NA_HEADS * HEAD_DIM
GQA_QW = GQA_Q_HEADS * HEAD_DIM
GQA_KVW = GQA_KV_HEADS * HEAD_DIM
DN_W = DN_HEADS * HEAD_DIM
MLA_W = MLA_HEADS * MLA_V
IN_SIZES = (NA_W, NA_W, NA_W, GQA_QW, GQA_KVW, GQA_KVW, DN_W, DN_W, DN_W, DN_W, 2 * DN_HEADS, 2 * DN_HEADS,
            MLA_Q_RANK, MLA_KV_RANK, MLA_ROPE)
IN_OFFSETS = tuple(int(o) for o in np.cumsum(IN_SIZES)[:-1])

V7X_VMEM_BYTES = 64 * 1024 * 1024
V7X_LANES = 128
VMEM_LIMIT = V7X_VMEM_BYTES - 8 * 1024 * 1024

PROJ_TM = 512
ATTN_ROWS = 256
ATTN_TK = 256
DN_BLOCK_CHUNKS = 8
MOE_TM = 512
MOE_FC = 512
GATHER_ROWS = 512
COMBINE_ROWS = 256

PROJ_GROUPS = (NA_W, NA_W, NA_W, GQA_QW, 2 * GQA_KVW, 3 * DN_W, DN_W, MLA_Q_RANK, MLA_KV_RANK, V7X_LANES)
SMALL_USED = MLA_ROPE + 4 * DN_HEADS


def _cparams(semantics):
    return pltpu.CompilerParams(dimension_semantics=semantics, vmem_limit_bytes=VMEM_LIMIT)


def _rms(x, g):
    return x * lax.rsqrt(jnp.mean(x * x, axis=-1, keepdims=True) + EPS) * g


def _mm_bias_kernel(x_ref, w_ref, b_ref, o_ref):
    o_ref[...] = jnp.dot(x_ref[...].astype(BF16), w_ref[...].astype(BF16), preferred_element_type=F32) + b_ref[...]


def _mm_bias(x, w, b, tn):
    m, k = x.shape
    n = w.shape[1]
    return pl.pallas_call(
        _mm_bias_kernel,
        grid=(n // tn,),
        in_specs=[pl.BlockSpec((m, k), lambda j: (0, 0)),
                  pl.BlockSpec((k, tn), lambda j: (0, j)),
                  pl.BlockSpec((1, tn), lambda j: (0, j))],
        out_specs=pl.BlockSpec((m, tn), lambda j: (0, j)),
        out_shape=jax.ShapeDtypeStruct((m, n), F32),
        compiler_params=_cparams(("parallel",)),
        name="ada_mm",
    )(x, w, b.reshape(1, n))


def _norm_mm_kernel(x_ref, g_ref, w_ref, o_ref):
    y = _rms(x_ref[...], g_ref[...])
    o_ref[...] = jnp.dot(y.astype(BF16), w_ref[...].astype(BF16), preferred_element_type=F32)


def _norm_mm(x, g, w, tm):
    m, k = x.shape
    n = w.shape[1]
    return pl.pallas_call(
        _norm_mm_kernel,
        grid=(m // tm,),
        in_specs=[pl.BlockSpec((tm, k), lambda i: (i, 0)),
                  pl.BlockSpec((1, k), lambda i: (0, 0)),
                  pl.BlockSpec((k, n), lambda i: (0, 0))],
        out_specs=pl.BlockSpec((tm, n), lambda i: (i, 0)),
        out_shape=jax.ShapeDtypeStruct((m, n), F32),
        compiler_params=_cparams(("parallel",)),
        name="norm_mm",
    )(x, g.reshape(1, k), w)


def _in_proj_kernel(x_ref, g_ref, sh_ref, sc_ref, w_ref, *out_refs):
    h = _rms(x_ref[0], g_ref[...]) * (1.0 + sc_ref[0]) + sh_ref[0]
    hb = h.astype(BF16)
    off = 0
    for o_ref in out_refs:
        wd = o_ref.shape[-1]
        o_ref[0] = jnp.dot(hb, w_ref[:, off:off + wd], preferred_element_type=F32)
        off += wd


def _in_proj(x, g, shift, scale, w_perm, tm):
    b, n, d = x.shape
    wtot = w_perm.shape[1]
    vec = pl.BlockSpec((1, 1, d), lambda i, j: (i, 0, 0))
    return pl.pallas_call(
        _in_proj_kernel,
        grid=(b, n // tm),
        in_specs=[pl.BlockSpec((1, tm, d), lambda i, j: (i, j, 0)),
                  pl.BlockSpec((1, d), lambda i, j: (0, 0)),
                  vec, vec,
                  pl.BlockSpec((d, wtot), lambda i, j: (0, 0))],
        out_specs=[pl.BlockSpec((1, tm, wd), lambda i, j: (i, j, 0)) for wd in PROJ_GROUPS],
        out_shape=[jax.ShapeDtypeStruct((b, n, wd), F32) for wd in PROJ_GROUPS],
        compiler_params=_cparams(("parallel", "parallel")),
        name="in_proj",
    )(x, g.reshape(1, d), shift.reshape(b, 1, d), scale.reshape(b, 1, d), w_perm)


def _attn_kernel(q_ref, k_ref, v_ref, o_ref, m_scr, l_scr, acc_scr, *, tk):
    grp, tq, dk = q_ref.shape[2:]
    rows = grp * tq
    n_keys = k_ref.shape[2]
    dv = v_ref.shape[3]
    q = q_ref[0, 0].reshape(rows, dk)
    m_scr[...] = jnp.full(m_scr.shape, NEG_INF, F32)
    l_scr[...] = jnp.zeros(l_scr.shape, F32)
    acc_scr[...] = jnp.zeros(acc_scr.shape, F32)

    def body(j, carry):
        off = pl.multiple_of(j * tk, tk)
        kj = k_ref[0, 0, pl.ds(off, tk), :]
        vj = v_ref[0, 0, pl.ds(off, tk), :]
        s = lax.dot_general(q, kj, (((1,), (1,)), ((), ())), preferred_element_type=F32)
        m_old = m_scr[...]
        m_new = jnp.maximum(m_old, jnp.max(s, axis=-1, keepdims=True))
        alpha = jnp.exp(m_old - m_new)
        p = jnp.exp(s - m_new)
        l_scr[...] = alpha * l_scr[...] + jnp.sum(p, axis=-1, keepdims=True)
        acc_scr[...] = alpha * acc_scr[...] + jnp.dot(p.astype(BF16), vj, preferred_element_type=F32)
        m_scr[...] = m_new
        return carry

    lax.fori_loop(0, n_keys // tk, body, 0)
    o = acc_scr[...] / l_scr[...]
    o_ref[0, 0] = o.reshape(grp, tq, dv).astype(o_ref.dtype)


def _attention(q, k, v, tk=ATTN_TK):
    b, hkv, grp, nq, dk = q.shape
    m = k.shape[2]
    dv = v.shape[3]
    tq = min(ATTN_ROWS // grp, nq)
    tk = min(tk, m)
    rows = grp * tq
    return pl.pallas_call(
        functools.partial(_attn_kernel, tk=tk),
        grid=(b, hkv, nq // tq),
        in_specs=[pl.BlockSpec((1, 1, grp, tq, dk), lambda i, h, j: (i, h, 0, j, 0)),
                  pl.BlockSpec((1, 1, m, dk), lambda i, h, j: (i, h, 0, 0)),
                  pl.BlockSpec((1, 1, m, dv), lambda i, h, j: (i, h, 0, 0))],
        out_specs=pl.BlockSpec((1, 1, grp, tq, dv), lambda i, h, j: (i, h, 0, j, 0)),
        out_shape=jax.ShapeDtypeStruct((b, hkv, grp, nq, dv), BF16),
        scratch_shapes=[pltpu.VMEM((rows, 1), F32), pltpu.VMEM((rows, 1), F32), pltpu.VMEM((rows, dv), F32)],
        compiler_params=_cparams(("parallel", "parallel", "parallel")),
        name="attention",
    )(q, k, v)


def _na_kernel(q_ref, k_ref, v_ref, kc_ref, vc_ref, bias_ref, o_ref, *, rows):
    r = pl.program_id(1)
    r0 = jnp.clip(r - NA_WIN_R // 2, 0, rows - NA_WIN_R)
    off = pl.multiple_of(r0 * GRID_W, GRID_W)
    kwin = k_ref[0, pl.ds(off, NA_WIN_R * GRID_W), :]
    vwin = v_ref[0, pl.ds(off, NA_WIN_R * GRID_W), :]
    q = q_ref[0]
    kc = kc_ref[0]
    vc = vc_ref[0]
    outs = []
    for h in range(NA_HEADS):
        sl = slice(h * HEAD_DIM, (h + 1) * HEAD_DIM)
        qh = q[:, sl]
        s_loc = lax.dot_general(qh, kwin[:, sl], (((1,), (1,)), ((), ())), preferred_element_type=F32) + bias_ref[0, h]
        s_ctx = lax.dot_general(qh, kc[:, sl], (((1,), (1,)), ((), ())), preferred_element_type=F32)
        m = jnp.maximum(jnp.max(s_loc, axis=-1, keepdims=True), jnp.max(s_ctx, axis=-1, keepdims=True))
        p_loc = jnp.exp(s_loc - m)
        p_ctx = jnp.exp(s_ctx - m)
        den = jnp.sum(p_loc, axis=-1, keepdims=True) + jnp.sum(p_ctx, axis=-1, keepdims=True)
        o = (jnp.dot(p_loc.astype(BF16), vwin[:, sl], preferred_element_type=F32)
             + jnp.dot(p_ctx.astype(BF16), vc[:, sl], preferred_element_type=F32))
        outs.append(o / den)
    o_ref[0] = jnp.concatenate(outs, axis=-1).astype(o_ref.dtype)


def _na_bias_table(rel_bias, rows):
    kr = NA_WIN_R
    cpos = np.arange(GRID_W)
    c0 = np.clip(cpos - NA_WIN_C // 2, 0, GRID_W - NA_WIN_C)
    col_ok = (cpos[None, :] >= c0[:, None]) & (cpos[None, :] < c0[:, None] + NA_WIN_C)
    dc = np.clip(cpos[None, :] - cpos[:, None], -(NA_WIN_C - 1), NA_WIN_C - 1) + (NA_WIN_C - 1)
    tabs = []
    for o in range(kr):
        dr = o + np.arange(kr)
        t = rel_bias[:, dr[:, None, None], dc[None, :, :]]
        t = jnp.where(col_ok[None, None], t, NEG_INF)
        tabs.append(jnp.transpose(t, (0, 2, 1, 3)).reshape(NA_HEADS, GRID_W, kr * GRID_W))
    return jnp.stack(tabs).astype(F32)


def _na_attention(q, k, v, kc, vc, bias_tab):
    b, n, w = q.shape
    l = kc.shape[1]
    rows = n // GRID_W
    assert rows >= NA_WIN_R

    def bias_map(i, r):
        r0 = jnp.clip(r - NA_WIN_R // 2, 0, rows - NA_WIN_R)
        return (r0 - r + NA_WIN_R - 1, 0, 0, 0)

    return pl.pallas_call(
        functools.partial(_na_kernel, rows=rows),
        grid=(b, rows),
        in_specs=[pl.BlockSpec((1, GRID_W, w), lambda i, r: (i, r, 0)),
                  pl.BlockSpec((1, n, w), lambda i, r: (i, 0, 0)),
                  pl.BlockSpec((1, n, w), lambda i, r: (i, 0, 0)),
                  pl.BlockSpec((1, l, w), lambda i, r: (i, 0, 0)),
                  pl.BlockSpec((1, l, w), lambda i, r: (i, 0, 0)),
                  pl.BlockSpec((1, NA_HEADS, GRID_W, NA_WIN_R * GRID_W), bias_map)],
        out_specs=pl.BlockSpec((1, GRID_W, w), lambda i, r: (i, r, 0)),
        out_shape=jax.ShapeDtypeStruct((b, n, w), BF16),
        compiler_params=_cparams(("parallel", "arbitrary")),
        name="na_attention",
    )(q, k, v, kc, vc, bias_tab)


def _dn_kernel(q_ref, k_ref, v_ref, g_ref, b_ref, s0_ref, o_ref, sf_ref, s_scr, *, reverse, chunks, col0):
    c = DN_CHUNK
    d = HEAD_DIM
    blk = pl.program_id(1)

    @pl.when(blk == 0)
    def _():
        s_scr[...] = s0_ref[0]

    ii = lax.broadcasted_iota(jnp.int32, (c, c), 0)
    jj = lax.broadcasted_iota(jnp.int32, (c, c), 1)
    if reverse:
        incl = ii <= jj
        strict = ii < jj
    else:
        incl = ii >= jj
        strict = ii > jj
    tri = incl.astype(F32)
    eye = (ii == jj).astype(F32)
    lvl_masks = []
    for lv in range(int(math.log2(c))):
        same = (ii >> (lv + 1)) == (jj >> (lv + 1))
        hi_i = ((ii >> lv) & 1) == 1
        hi_j = ((jj >> lv) & 1) == 1
        if reverse:
            msk = same & jnp.logical_not(hi_i) & hi_j
        else:
            msk = same & hi_i & jnp.logical_not(hi_j)
        lvl_masks.append(msk)
    last = 0 if reverse else c - 1

    def mm(a, b_):
        return jnp.dot(a, b_, preferred_element_type=F32, precision=HIGHEST)

    def mm_nt(a, b_):
        return lax.dot_general(a, b_, (((1,), (1,)), ((), ())), preferred_element_type=F32, precision=HIGHEST)

    def chunk(ci, carry):
        idx = (chunks - 1 - ci) if reverse else ci
        off = pl.multiple_of(idx * c, c)
        qc = q_ref[0, pl.ds(off, c), :]
        kc = k_ref[0, pl.ds(off, c), :]
        vc = v_ref[0, pl.ds(off, c), :]
        gcol = g_ref[0, pl.ds(off, c), :]
        bcol = b_ref[0, pl.ds(off, c), :]
        gc_all = mm(tri, gcol)
        gc_all_t = gc_all.T
        outs = []
        for h in range(DN_HEADS):
            col = col0 + h
            sl = slice(h * d, (h + 1) * d)
            gc_c = gc_all[:, col:col + 1]
            gc_r = gc_all_t[col:col + 1, :]
            be = bcol[:, col:col + 1]
            qh = qc[:, sl] * (d ** -0.5)
            kh = kc[:, sl]
            vh = vc[:, sl]
            decay = jnp.exp(jnp.where(incl, gc_c - gc_r, NEG_INF))
            kb = kh * be
            a = jnp.where(strict, mm_nt(kb, kh) * decay, 0.0)
            t_inv = eye
            for msk in lvl_masks:
                t_inv = t_inv - mm(mm(t_inv, jnp.where(msk, a, 0.0)), t_inv)
            e_gc = jnp.exp(gc_c)
            u = mm(t_inv, vh * be)
            w = mm(t_inv, kb * e_gc)
            attn = jnp.where(incl, mm_nt(qh, kh) * decay, 0.0)
            s = s_scr[h]
            v_new = u - mm(w, s)
            outs.append(mm(qh * e_gc, s) + mm(attn, v_new))
            g_last = gc_c[last:last + 1, :]
            k_dec = kh * jnp.exp(g_last - gc_c)
            s_scr[h] = s * jnp.exp(g_last) + mm(k_dec.T, v_new)
        o_ref[0, pl.ds(off, c), :] = jnp.concatenate(outs, axis=-1)
        return carry

    lax.fori_loop(0, chunks, chunk, 0)

    @pl.when(blk == pl.num_programs(1) - 1)
    def _():
        sf_ref[0] = s_scr[...]


def _delta_scan(q, k, v, g, beta, s0, direction):
    b, n, w = q.shape
    reverse = direction == 1
    nchunks = n // DN_CHUNK
    chunks = min(DN_BLOCK_CHUNKS, nchunks)
    nblk = nchunks // chunks
    bt = chunks * DN_CHUNK

    def tok_map(i, j):
        return (i, (nblk - 1 - j) if reverse else j, 0)

    tok = pl.BlockSpec((1, bt, w), tok_map)
    small = pl.BlockSpec((1, bt, V7X_LANES), tok_map)
    state = pl.BlockSpec((1, DN_HEADS, HEAD_DIM, HEAD_DIM), lambda i, j: (i, 0, 0, 0))
    return pl.pallas_call(
        functools.partial(_dn_kernel, reverse=reverse, chunks=chunks, col0=direction * DN_HEADS),
        grid=(b, nblk),
        in_specs=[tok, tok, tok, small, small, state],
        out_specs=[tok, state],
        out_shape=[jax.ShapeDtypeStruct((b, n, w), F32),
                   jax.ShapeDtypeStruct((b, DN_HEADS, HEAD_DIM, HEAD_DIM), F32)],
        scratch_shapes=[pltpu.VMEM((DN_HEADS, HEAD_DIM, HEAD_DIM), F32)],
        compiler_params=_cparams(("parallel", "arbitrary")),
        name="delta_scan",
    )(q, k, v, g, beta, s0)


def _out_proj_kernel(x_ref, ya_ref, yb_ref, yc_ref, yd_ref, w_ref, gate_ref, g2_ref, sh_ref, sc_ref, rw_ref, rb_ref,
                     xo_ref, h2_ref, idx_ref, gt_ref):
    acc = None
    for i, y_ref in enumerate((ya_ref, yb_ref, yc_ref, yd_ref)):
        wd = y_ref.shape[-1]
        part = jnp.dot(y_ref[0], w_ref[i * wd:(i + 1) * wd, :], preferred_element_type=F32)
        acc = part if acc is None else acc + part
    xn = x_ref[0] + gate_ref[0] * acc
    xo_ref[0] = xn
    h2 = _rms(xn, g2_ref[...]) * (1.0 + sc_ref[0]) + sh_ref[0]
    h2_ref[0] = h2
    logits = jnp.dot(h2, rw_ref[...], preferred_element_type=F32, precision=HIGHEST) + rb_ref[...]
    lane = lax.broadcasted_iota(jnp.int32, logits.shape, 1).astype(F32)
    vals, idxs = [], []
    cur = logits
    for _ in range(TOP_K):
        mx = jnp.max(cur, axis=-1, keepdims=True)
        ik = jnp.min(jnp.where(cur == mx, lane, float(V7X_LANES)), axis=-1, keepdims=True)
        vals.append(mx)
        idxs.append(ik)
        cur = jnp.where(lane == ik, -jnp.inf, cur)
    es = [jnp.exp(vv - vals[0]) for vv in vals]
    den = es[0] + es[1] + es[2] + es[3]
    idx_out = jnp.zeros(logits.shape, F32)
    gate_out = jnp.zeros(logits.shape, F32)
    for kk in range(TOP_K):
        idx_out = jnp.where(lane == float(kk), idxs[kk], idx_out)
        gate_out = jnp.where(lane == float(kk), es[kk] / den, gate_out)
    idx_ref[0] = idx_out.astype(jnp.int32)
    gt_ref[0] = gate_out


def _out_proj(x, ys, w_out, gate, g2, shift, scale, rw_pad, rb_pad, tm):
    b, n, d = x.shape
    yw = ys[0].shape[-1]
    vec = pl.BlockSpec((1, 1, d), lambda i, j: (i, 0, 0))
    tok = pl.BlockSpec((1, tm, d), lambda i, j: (i, j, 0))
    ytok = pl.BlockSpec((1, tm, yw), lambda i, j: (i, j, 0))
    ltok = pl.BlockSpec((1, tm, V7X_LANES), lambda i, j: (i, j, 0))
    return pl.pallas_call(
        _out_proj_kernel,
        grid=(b, n // tm),
        in_specs=[tok, ytok, ytok, ytok, ytok,
                  pl.BlockSpec((4 * yw, d), lambda i, j: (0, 0)),
                  vec,
                  pl.BlockSpec((1, d), lambda i, j: (0, 0)),
                  vec, vec,
                  pl.BlockSpec((d, V7X_LANES), lambda i, j: (0, 0)),
                  pl.BlockSpec((1, V7X_LANES), lambda i, j: (0, 0))],
        out_specs=[tok, tok, ltok, ltok],
        out_shape=[jax.ShapeDtypeStruct((b, n, d), F32), jax.ShapeDtypeStruct((b, n, d), F32),
                   jax.ShapeDtypeStruct((b, n, V7X_LANES), jnp.int32), jax.ShapeDtypeStruct((b, n, V7X_LANES), F32)],
        compiler_params=_cparams(("parallel", "parallel")),
        name="out_proj",
    )(x, ys[0], ys[1], ys[2], ys[3], w_out, gate.reshape(b, 1, d), g2.reshape(1, d),
      shift.reshape(b, 1, d), scale.reshape(b, 1, d), rw_pad, rb_pad)


def _gather_kernel(src_ref, h_hbm, o_ref, sem):
    rows = o_ref.shape[0]

    def issue(r, carry):
        tok = src_ref[0, 0, r]
        pltpu.make_async_copy(h_hbm.at[pl.ds(tok, 1)], o_ref.at[pl.ds(r, 1)], sem).start()
        return carry

    lax.fori_loop(0, rows, issue, 0)
    pltpu.make_async_copy(h_hbm.at[pl.ds(0, rows)], o_ref, sem).wait()


def _gather_rows(h, src, rows_per_step):
    t, d = h.shape
    r = src.shape[0]
    nsteps = r // rows_per_step
    return pl.pallas_call(
        _gather_kernel,
        grid=(nsteps,),
        in_specs=[pl.BlockSpec((1, 1, rows_per_step), lambda i: (i, 0, 0), memory_space=pltpu.SMEM),
                  pl.BlockSpec(memory_space=pl.ANY)],
        out_specs=pl.BlockSpec((rows_per_step, d), lambda i: (i, 0)),
        out_shape=jax.ShapeDtypeStruct((r, d), h.dtype),
        scratch_shapes=[pltpu.SemaphoreType.DMA],
        compiler_params=_cparams(("arbitrary",)),
        name="moe_gather",
    )(src.reshape(nsteps, 1, rows_per_step), h)


def _experts_kernel(te_ref, tf_ref, tv_ref, x_ref, w1_ref, b1_ref, w2_ref, b2_ref, gr_ref, o_ref, w1b, w2b, *, fc):
    i = pl.program_id(0)
    f = w2_ref.shape[1]

    @pl.when(tv_ref[i] == 0)
    def _():
        o_ref[...] = jnp.zeros(o_ref.shape, o_ref.dtype)

    @pl.when(tv_ref[i] != 0)
    def _():
        @pl.when(tf_ref[i] != 0)
        def _():
            w1b[...] = w1_ref[0].astype(BF16)
            w2b[...] = w2_ref[0].astype(BF16)

        xb = x_ref[...].astype(BF16)
        acc = None
        for j in range(f // fc):
            glu = jnp.dot(xb, w1b[:, j * fc:(j + 1) * fc], preferred_element_type=F32) + b1_ref[0, :, j * fc:(j + 1) * fc]
            lin = (jnp.dot(xb, w1b[:, f + j * fc:f + (j + 1) * fc], preferred_element_type=F32)
                   + b1_ref[0, :, f + j * fc:f + (j + 1) * fc])
            glu = jnp.minimum(glu, SWIGLU_LIMIT)
            lin = jnp.clip(lin, -SWIGLU_LIMIT, SWIGLU_LIMIT)
            act = glu * jax.nn.sigmoid(SWIGLU_ALPHA * glu) * (lin + 1.0)
            part = jnp.dot(act.astype(BF16), w2b[j * fc:(j + 1) * fc, :], preferred_element_type=F32)
            acc = part if acc is None else acc + part
        o_ref[...] = (acc + b2_ref[0]) * gr_ref[...]


def _expert_tiles(xs, gate_rows, tile_e, tile_first, tile_valid, w1, b1, w2, b2, tm):
    r, d = xs.shape
    e, _, f2 = w1.shape
    f = f2 // 2
    nt = r // tm
    grid_spec = pltpu.PrefetchScalarGridSpec(
        num_scalar_prefetch=3,
        grid=(nt,),
        in_specs=[pl.BlockSpec((tm, d), lambda i, te, tf, tv: (i, 0)),
                  pl.BlockSpec((1, d, f2), lambda i, te, tf, tv: (te[i], 0, 0)),
                  pl.BlockSpec((1, 1, f2), lambda i, te, tf, tv: (te[i], 0, 0)),
                  pl.BlockSpec((1, f, d), lambda i, te, tf, tv: (te[i], 0, 0)),
                  pl.BlockSpec((1, 1, d), lambda i, te, tf, tv: (te[i], 0, 0)),
                  pl.BlockSpec((tm, 1), lambda i, te, tf, tv: (i, 0))],
        out_specs=pl.BlockSpec((tm, d), lambda i, te, tf, tv: (i, 0)),
        scratch_shapes=[pltpu.VMEM((d, f2), BF16), pltpu.VMEM((f, d), BF16)],
    )
    return pl.pallas_call(
        functools.partial(_experts_kernel, fc=MOE_FC),
        grid_spec=grid_spec,
        out_shape=jax.ShapeDtypeStruct((r, d), F32),
        compiler_params=_cparams(("arbitrary",)),
        name="moe_experts",
    )(tile_e, tile_first, tile_valid, xs, w1, b1.reshape(e, 1, f2), w2, b2.reshape(e, 1, d), gate_rows.reshape(r, 1))


def _combine_kernel(pos_ref, ys_hbm, o_ref, buf, sem):
    ct = o_ref.shape[0]

    def issue(r, carry):
        for kk in range(TOP_K):
            p = pos_ref[0, 0, r * TOP_K + kk]
            pltpu.make_async_copy(ys_hbm.at[pl.ds(p, 1)], buf.at[kk, pl.ds(r, 1)], sem).start()
        return carry

    lax.fori_loop(0, ct, issue, 0)
    for kk in range(TOP_K):
        pltpu.make_async_copy(ys_hbm.at[pl.ds(0, ct)], buf.at[kk], sem).wait()
    o_ref[...] = (buf[0] + buf[1]) + (buf[2] + buf[3])


def _combine_rows(ys, pos, ct):
    t = pos.shape[0]
    d = ys.shape[1]
    nsteps = t // ct
    return pl.pallas_call(
        _combine_kernel,
        grid=(nsteps,),
        in_specs=[pl.BlockSpec((1, 1, ct * TOP_K), lambda i: (i, 0, 0), memory_space=pltpu.SMEM),
                  pl.BlockSpec(memory_space=pl.ANY)],
        out_specs=pl.BlockSpec((ct, d), lambda i: (i, 0)),
        out_shape=jax.ShapeDtypeStruct((t, d), F32),
        scratch_shapes=[pltpu.VMEM((TOP_K, ct, d), F32), pltpu.SemaphoreType.DMA],
        compiler_params=_cparams(("arbitrary",)),
        name="moe_combine",
    )(pos.reshape(nsteps, 1, ct * TOP_K), ys)


def _route_plan(idx4, gates4, tm):
    t = idx4.shape[0]
    e = N_EXPERTS
    r_max = t * TOP_K + e * tm
    nt = r_max // tm
    onehot = (idx4[:, :, None] == jnp.arange(e, dtype=jnp.int32)[None, None, :]).astype(jnp.int32)
    member = jnp.sum(onehot, axis=1)
    csum = jnp.cumsum(member, axis=0)
    cnt = csum[-1]
    excl = csum - member
    cnt_pad = ((cnt + tm - 1) // tm) * tm
    ends = jnp.cumsum(cnt_pad)
    base = ends - cnt_pad
    pos = jnp.sum(onehot * (excl + base[None, :])[:, None, :], axis=-1)
    flat = pos.reshape(-1)
    tok = jnp.repeat(jnp.arange(t, dtype=jnp.int32), TOP_K)
    src = jnp.zeros((r_max,), jnp.int32).at[flat].set(tok, unique_indices=True)
    gate_rows = jnp.zeros((r_max,), F32).at[flat].set(gates4.reshape(-1), unique_indices=True)
    tile_start = jnp.arange(nt, dtype=jnp.int32) * tm
    tile_e = jnp.minimum(jnp.searchsorted(ends, tile_start, side="right").astype(jnp.int32), e - 1)
    tile_valid = (tile_start < ends[-1]).astype(jnp.int32)
    tile_first = jnp.concatenate([jnp.ones((1,), jnp.int32), (tile_e[1:] != tile_e[:-1]).astype(jnp.int32)])
    return pos.astype(jnp.int32), src, gate_rows, tile_e, tile_first, tile_valid


def _moe(h2, idx4, gates4, w1, b1, w2, b2):
    pos, src, gate_rows, tile_e, tile_first, tile_valid = _route_plan(idx4, gates4, MOE_TM)
    xs = _gather_rows(h2, src, GATHER_ROWS)
    ys = _expert_tiles(xs, gate_rows, tile_e, tile_first, tile_valid, w1, b1, w2, b2, MOE_TM)
    return _combine_rows(ys, pos, COMBINE_ROWS)


def _axial_rope_tables(n_tok, rot_dim):
    t = jnp.arange(n_tok, dtype=jnp.int32)
    row = (t // GRID_W).astype(F32)
    col = (t % GRID_W).astype(F32)
    n_freq = rot_dim // 4
    freqs = ROPE_THETA ** (-jnp.arange(n_freq, dtype=F32) / n_freq)
    ang = jnp.concatenate([row[:, None] * freqs, col[:, None] * freqs], axis=-1)
    return jnp.cos(ang), jnp.sin(ang)


def _apply_rope(x, cos, sin):
    xf = x.reshape(*x.shape[:-1], -1, 2)
    x1, x2 = xf[..., 0], xf[..., 1]
    cs, sn = cos[:, None, :], sin[:, None, :]
    out = jnp.stack([x1 * cs - x2 * sn, x1 * sn + x2 * cs], axis=-1)
    return out.reshape(x.shape)


def _heads(t, n):
    return t.reshape(t.shape[0], t.shape[1], n, t.shape[2] // n)


def _head_major(t):
    return jnp.transpose(t, (0, 2, 1, 3))


def _permute_w_in(w_in):
    offs = (0,) + IN_OFFSETS
    seg = {i: (offs[i], offs[i] + IN_SIZES[i]) for i in range(len(IN_SIZES))}
    order = [0, 1, 2, 3, 4, 5, 6, 7, 8, 9, 12, 13, 14, 10, 11]
    cols = np.concatenate([np.arange(*seg[i]) for i in order])
    w = jnp.take(w_in, jnp.asarray(cols), axis=1)
    pad = sum(PROJ_GROUPS) - w.shape[1]
    return jnp.pad(w, ((0, 0), (0, pad))).astype(BF16)


def _dn_prep(qkv, small, conv_w, a_log, dt_bias):
    b, n, _ = qkv.shape
    pad = DN_CONV // 2
    xp = jnp.pad(qkv, ((0, 0), (pad, pad), (0, 0)))
    conv = sum(xp[:, i:i + n, :] * conv_w[i][None, None, :] for i in range(DN_CONV))
    act = jax.nn.silu(conv)
    q, k, v = jnp.split(act, 3, axis=-1)

    def l2(t):
        th = _heads(t, DN_HEADS)
        return (th * lax.rsqrt(jnp.sum(th * th, axis=-1, keepdims=True) + EPS)).reshape(b, n, DN_W)

    beta_raw = small[..., MLA_ROPE:MLA_ROPE + 2 * DN_HEADS]
    a_raw = small[..., MLA_ROPE + 2 * DN_HEADS:MLA_ROPE + 4 * DN_HEADS]
    beta = jax.nn.sigmoid(beta_raw)
    g = -jnp.exp(a_log.reshape(-1)) * jax.nn.softplus(a_raw + dt_bias.reshape(-1))
    lane_pad = ((0, 0), (0, 0), (0, V7X_LANES - 2 * DN_HEADS))
    return l2(q), l2(k), v, jnp.pad(g, lane_pad), jnp.pad(beta, lane_pad)


def _bidir(q, k, v, g, beta, s0_f, s0_b):
    o_f, s_f = _delta_scan(q, k, v, g, beta, s0_f, 0)
    o_b, s_b = _delta_scan(q, k, v, g, beta, s0_b, 1)
    return o_f + o_b, s_f, s_b


def _mla_kv(ckv, kpe, ckv_g, w_ukv, kn_g, rope):
    b, n, _ = ckv.shape
    kv = _norm_mm(ckv.reshape(b * n, MLA_KV_RANK), ckv_g, w_ukv, min(1024, b * n)).reshape(b, n, MLA_HEADS, MLA_NOPE + MLA_V)
    k_pe = jnp.broadcast_to(kpe[:, :, None, :], (b, n, MLA_HEADS, MLA_ROPE))
    k = _rms(jnp.concatenate([kv[..., :MLA_NOPE], k_pe], axis=-1), kn_g)
    if rope is not None:
        k = jnp.concatenate([k[..., :MLA_NOPE], _apply_rope(k[..., MLA_NOPE:], *rope)], axis=-1)
    return k, kv[..., MLA_NOPE:]


def _mla_q(cq, cq_g, w_uq, qn_g, rope):
    b, n, _ = cq.shape
    q = _norm_mm(cq.reshape(b * n, MLA_Q_RANK), cq_g, w_uq, min(1024, b * n)).reshape(b, n, MLA_HEADS, MLA_QK)
    q = _rms(q, qn_g)
    if rope is not None:
        q = jnp.concatenate([q[..., :MLA_NOPE], _apply_rope(q[..., MLA_NOPE:], *rope)], axis=-1)
    return q


def _attn_bnhd(q, k, v, groups):
    b, nq, hq, dk = q.shape
    hkv = k.shape[2]
    qh = _head_major(q * (dk ** -0.5)).astype(BF16).reshape(b, hkv, groups, nq, dk)
    o = _attention(qh, _head_major(k).astype(BF16), _head_major(v).astype(BF16))
    dv = o.shape[-1]
    return jnp.transpose(o.reshape(b, hq, nq, dv), (0, 2, 1, 3)).reshape(b, nq, hq * dv)


def kernel(x, c, ctx, c_ctx, ada_w, ada_b, norm1_g, norm2_g, w_in, w_out, na_qn_g, na_kn_g, na_rel_bias, gqa_qn_g, gqa_kn_g, dn_conv_w, dn_a_log, dn_dt_bias, dn_out_g, mla_cq_g, mla_ckv_g, mla_w_uq, mla_w_ukv, mla_qn_g, mla_kn_g, router_w, router_b, exp_w1, exp_b1, exp_w2, exp_b2):
    b, n, d = x.shape
    l = ctx.shape[1]
    depth = ada_w.shape[0]
    grp = GQA_Q_HEADS // GQA_KV_HEADS
    rope_g = _axial_rope_tables(n, HEAD_DIM)
    rope_m = _axial_rope_tables(n, MLA_ROPE)
    cond = jnp.concatenate([jax.nn.silu(c), jax.nn.silu(c_ctx)[None], jnp.zeros((16 - b - 1, d), F32)], axis=0)
    s0 = jnp.zeros((b, DN_HEADS, HEAD_DIM, HEAD_DIM), F32)
    scale = HEAD_DIM ** -0.5

    for ly in range(depth):
        with_ctx = ly < depth - 1
        mod_all = _mm_bias(cond, ada_w[ly], ada_b[ly], 1024)
        mod = jnp.split(mod_all[:b], N_ADA, axis=-1)
        mod_c = [jnp.broadcast_to(m_, (b, d)) for m_ in jnp.split(mod_all[b:b + 1], N_ADA, axis=-1)]
        w_perm = _permute_w_in(w_in[ly])
        p = _in_proj(x, norm1_g[ly], mod[0], mod[1], w_perm, PROJ_TM)
        pc = _in_proj(ctx, norm1_g[ly], mod_c[0], mod_c[1], w_perm, l)
        bias_tab = _na_bias_table(na_rel_bias[ly], n // GRID_W)

        qa = _rms(_heads(p[0], NA_HEADS), na_qn_g[ly])
        ka = _rms(_heads(p[1], NA_HEADS), na_kn_g[ly])
        kac = _rms(_heads(pc[1], NA_HEADS), na_kn_g[ly])
        ya = _na_attention((qa * scale).reshape(b, n, NA_W).astype(BF16), ka.reshape(b, n, NA_W).astype(BF16),
                           p[2].astype(BF16), kac.reshape(b, l, NA_W).astype(BF16), pc[2].astype(BF16), bias_tab)

        qb = _apply_rope(_rms(_heads(p[3], GQA_Q_HEADS), gqa_qn_g[ly]), *rope_g)
        kb = _apply_rope(_rms(_heads(p[4][..., :GQA_KVW], GQA_KV_HEADS), gqa_kn_g[ly]), *rope_g)
        vb = _heads(p[4][..., GQA_KVW:], GQA_KV_HEADS)
        kbc = _rms(_heads(pc[4][..., :GQA_KVW], GQA_KV_HEADS), gqa_kn_g[ly])
        vbc = _heads(pc[4][..., GQA_KVW:], GQA_KV_HEADS)
        yb = _attn_bnhd(qb, jnp.concatenate([kb, kbc], axis=1), jnp.concatenate([vb, vbc], axis=1), grp)

        qdc, kdc, vdc, gdc, bdc = _dn_prep(pc[5], pc[9], dn_conv_w[ly], dn_a_log[ly], dn_dt_bias[ly])
        odc, s_f, s_b = _bidir(qdc, kdc, vdc, gdc, bdc, s0, s0)
        qd, kd, vd, gd, bd = _dn_prep(p[5], p[9], dn_conv_w[ly], dn_a_log[ly], dn_dt_bias[ly])
        od, _, _ = _bidir(qd, kd, vd, gd, bd, s_f, s_b)
        yc = (_rms(_heads(od, DN_HEADS), dn_out_g[ly]) * jax.nn.silu(_heads(p[6], DN_HEADS))).reshape(b, n, DN_W).astype(BF16)

        qm = _mla_q(p[7], mla_cq_g[ly], mla_w_uq[ly], mla_qn_g[ly], rope_m)
        km, vm = _mla_kv(p[8], p[9][..., :MLA_ROPE], mla_ckv_g[ly], mla_w_ukv[ly], mla_kn_g[ly], rope_m)
        kmc, vmc = _mla_kv(pc[8], pc[9][..., :MLA_ROPE], mla_ckv_g[ly], mla_w_ukv[ly], mla_kn_g[ly], None)
        yd = _attn_bnhd(qm, jnp.concatenate([km, kmc], axis=1), jnp.concatenate([vm, vmc], axis=1), 1)

        rw_pad = jnp.pad(router_w[ly], ((0, 0), (0, V7X_LANES - N_EXPERTS)))
        rb_pad = jnp.pad(router_b[ly], (0, V7X_LANES - N_EXPERTS), constant_values=NEG_INF).reshape(1, V7X_LANES)
        w_out_b = w_out[ly].astype(BF16)
        x, h2, idx, gts = _out_proj(x, (ya, yb, yc, yd), w_out_b, mod[2], norm2_g[ly], mod[3], mod[4],
                                    rw_pad, rb_pad, PROJ_TM)
        tok = h2.reshape(b * n, d)
        idx4 = idx.reshape(b * n, V7X_LANES)[:, :TOP_K]
        gates4 = gts.reshape(b * n, V7X_LANES)[:, :TOP_K]

        if with_ctx:
            qac = _rms(_heads(pc[0], NA_HEADS), na_qn_g[ly])
            yac = _attn_bnhd(qac, kac, _heads(pc[2], NA_HEADS), 1)
            qbc = _rms(_heads(pc[3], GQA_Q_HEADS), gqa_qn_g[ly])
            ybc = _attn_bnhd(qbc, kbc, vbc, grp)
            ycc = (_rms(_heads(odc, DN_HEADS), dn_out_g[ly]) * jax.nn.silu(_heads(pc[6], DN_HEADS))).reshape(b, l, DN_W).astype(BF16)
            qmc = _mla_q(pc[7], mla_cq_g[ly], mla_w_uq[ly], mla_qn_g[ly], None)
            ydc = _attn_bnhd(qmc, kmc, vmc, 1)
            ctx, h2c, idxc, gtsc = _out_proj(ctx, (yac, ybc, ycc, ydc), w_out_b, mod_c[2], norm2_g[ly], mod_c[3], mod_c[4],
                                             rw_pad, rb_pad, l)
            tok = jnp.concatenate([tok, h2c.reshape(b * l, d)], axis=0)
            idx4 = jnp.concatenate([idx4, idxc.reshape(b * l, V7X_LANES)[:, :TOP_K]], axis=0)
            gates4 = jnp.concatenate([gates4, gtsc.reshape(b * l, V7X_LANES)[:, :TOP_K]], axis=0)

        f = _moe(tok, idx4, gates4, exp_w1[ly], exp_b1[ly], exp_w2[ly], exp_b2[ly])
        x = x + mod[5][:, None] * f[:b * n].reshape(b, n, d)
        if with_ctx:
            ctx = ctx + mod_c[5][:, None] * f[b * n:].reshape(b, l, d)
    return x
```

```python
import functools
import math

import jax
import jax.numpy as jnp
import numpy as np
from jax import lax
from jax.experimental import pallas as pl
from jax.experimental.pallas import tpu as pltpu

F32 = jnp.float32
BF16 = jnp.bfloat16
HIGHEST = lax.Precision.HIGHEST

GRID_W = 64
HEAD_DIM = 64
NA_HEADS = 4
NA_WIN_R = 8
NA_WIN_C = 16
GQA_Q_HEADS = 4
GQA_KV_HEADS = 2
DN_HEADS = 4
DN_CONV = 5
DN_CHUNK = 64
MLA_HEADS = 4
MLA_Q_RANK = 256
MLA_KV_RANK = 128
MLA_NOPE = 64
MLA_ROPE = 32
MLA_V = 64
MLA_QK = MLA_NOPE + MLA_ROPE
N_EXPERTS = 32
TOP_K = 4
SWIGLU_LIMIT = 7.0
SWIGLU_ALPHA = 1.702
ROPE_THETA = 10000.0
EPS = 1e-6
NEG_INF = -1e30
N_ADA = 6

NA_W = NA_HEADS * HEAD_DIM
GQA_QW = GQA_Q_HEADS * HEAD_DIM
GQA_KVW = GQA_KV_HEADS * HEAD_DIM
DN_W = DN_HEADS * HEAD_DIM
MLA_W = MLA_HEADS * MLA_V
IN_SIZES = (NA_W, NA_W, NA_W, GQA_QW, GQA_KVW, GQA_KVW, DN_W, DN_W, DN_W, DN_W, 2 * DN_HEADS, 2 * DN_HEADS,
            MLA_Q_RANK, MLA_KV_RANK, MLA_ROPE)
IN_OFFSETS = tuple(int(o) for o in np.cumsum(IN_SIZES)[:-1])

V7X_VMEM_BYTES = 64 * 1024 * 1024
V7X_LANES = 128
VMEM_LIMIT = V7X_VMEM_BYTES - 8 * 1024 * 1024

PROJ_TM = 512
ATTN_ROWS = 256
ATTN_TK = 256
DN_BLOCK_CHUNKS = 8
DN_PREP_INTERLEAVE = 2
MOE_TM = 512
MOE_FC = 512
MOE_ISSUE_UNROLL = 8
COMBINE_ROWS = 256
COMBINE_ISSUE_UNROLL = 2

PROJ_GROUPS = (NA_W, NA_W, NA_W, GQA_QW, 2 * GQA_KVW, 3 * DN_W, DN_W, MLA_Q_RANK, MLA_KV_RANK, V7X_LANES)
SMALL_USED = MLA_ROPE + 4 * DN_HEADS


def _cparams(semantics):
    return pltpu.CompilerParams(dimension_semantics=semantics, vmem_limit_bytes=VMEM_LIMIT)


def _rms(x, g):
    return x * lax.rsqrt(jnp.mean(x * x, axis=-1, keepdims=True) + EPS) * g


def _mm_bias_kernel(x_ref, w_ref, b_ref, o_ref):
    o_ref[...] = jnp.dot(x_ref[...].astype(BF16), w_ref[...].astype(BF16), preferred_element_type=F32) + b_ref[...]


def _mm_bias(x, w, b, tn):
    m, k = x.shape
    n = w.shape[1]
    return pl.pallas_call(
        _mm_bias_kernel,
        grid=(n // tn,),
        in_specs=[pl.BlockSpec((m, k), lambda j: (0, 0)),
                  pl.BlockSpec((k, tn), lambda j: (0, j)),
                  pl.BlockSpec((1, tn), lambda j: (0, j))],
        out_specs=pl.BlockSpec((m, tn), lambda j: (0, j)),
        out_shape=jax.ShapeDtypeStruct((m, n), F32),
        compiler_params=_cparams(("parallel",)),
        name="ada_mm",
    )(x, w, b.reshape(1, n))


def _norm_mm_kernel(x_ref, g_ref, w_ref, o_ref):
    y = _rms(x_ref[...], g_ref[...])
    o_ref[...] = jnp.dot(y.astype(BF16), w_ref[...].astype(BF16), preferred_element_type=F32)


def _norm_mm(x, g, w, tm):
    m, k = x.shape
    n = w.shape[1]
    return pl.pallas_call(
        _norm_mm_kernel,
        grid=(m // tm,),
        in_specs=[pl.BlockSpec((tm, k), lambda i: (i, 0)),
                  pl.BlockSpec((1, k), lambda i: (0, 0)),
                  pl.BlockSpec((k, n), lambda i: (0, 0))],
        out_specs=pl.BlockSpec((tm, n), lambda i: (i, 0)),
        out_shape=jax.ShapeDtypeStruct((m, n), F32),
        compiler_params=_cparams(("parallel",)),
        name="norm_mm",
    )(x, g.reshape(1, k), w)


def _in_proj_kernel(x_ref, g_ref, sh_ref, sc_ref, w_ref, *out_refs):
    h = _rms(x_ref[0], g_ref[...]) * (1.0 + sc_ref[0]) + sh_ref[0]
    hb = h.astype(BF16)
    off = 0
    for o_ref in out_refs:
        wd = o_ref.shape[-1]
        o_ref[0] = jnp.dot(hb, w_ref[:, off:off + wd], preferred_element_type=F32)
        off += wd


def _in_proj(x, g, shift, scale, w_perm, tm):
    b, n, d = x.shape
    wtot = w_perm.shape[1]
    vec = pl.BlockSpec((1, 1, d), lambda i, j: (i, 0, 0))
    return pl.pallas_call(
        _in_proj_kernel,
        grid=(b, n // tm),
        in_specs=[pl.BlockSpec((1, tm, d), lambda i, j: (i, j, 0)),
                  pl.BlockSpec((1, d), lambda i, j: (0, 0)),
                  vec, vec,
                  pl.BlockSpec((d, wtot), lambda i, j: (0, 0))],
        out_specs=[pl.BlockSpec((1, tm, wd), lambda i, j: (i, j, 0)) for wd in PROJ_GROUPS],
        out_shape=[jax.ShapeDtypeStruct((b, n, wd), F32) for wd in PROJ_GROUPS],
        compiler_params=_cparams(("parallel", "parallel")),
        name="in_proj",
    )(x, g.reshape(1, d), shift.reshape(b, 1, d), scale.reshape(b, 1, d), w_perm)


def _attn_kernel(q_ref, k_ref, v_ref, o_ref, s_scr, *, tk, dv):
    grp, tq, dk = q_ref.shape[2:]
    rows = grp * tq
    n_keys = k_ref.shape[2]
    q = q_ref[0, 0].reshape(rows, dk)
    nk = n_keys // tk
    mx = None
    for j in range(nk):
        s = lax.dot_general(q, k_ref[0, 0, j * tk:(j + 1) * tk, :], (((1,), (1,)), ((), ())),
                            preferred_element_type=F32)
        s_scr[:, j * tk:(j + 1) * tk] = s
        for t in range(tk // V7X_LANES):
            slab = s[:, t * V7X_LANES:(t + 1) * V7X_LANES]
            mx = slab if mx is None else jnp.maximum(mx, slab)
    mb = jnp.broadcast_to(jnp.max(mx, axis=-1, keepdims=True), (rows, V7X_LANES))
    acc = None
    for j in range(nk):
        slabs = [jnp.exp(s_scr[:, j * tk + t * V7X_LANES:j * tk + (t + 1) * V7X_LANES] - mb)
                 for t in range(tk // V7X_LANES)]
        p = jnp.concatenate(slabs, axis=-1).astype(BF16)
        part = jnp.dot(p, v_ref[0, 0, j * tk:(j + 1) * tk, :], preferred_element_type=F32)
        acc = part if acc is None else acc + part
    o = acc[:, :dv] / acc[:, dv:dv + 1]
    o_ref[0, 0] = o.reshape(grp, tq, dv).astype(o_ref.dtype)


def _attention(q, k, v1, dv, tk=ATTN_TK):
    b, hkv, grp, nq, dk = q.shape
    m = k.shape[2]
    tq = min(ATTN_ROWS // grp, nq)
    tk = min(tk, m)
    rows = grp * tq
    return pl.pallas_call(
        functools.partial(_attn_kernel, tk=tk, dv=dv),
        grid=(b, hkv, nq // tq),
        in_specs=[pl.BlockSpec((1, 1, grp, tq, dk), lambda i, h, j: (i, h, 0, j, 0)),
                  pl.BlockSpec((1, 1, m, dk), lambda i, h, j: (i, h, 0, 0)),
                  pl.BlockSpec((1, 1, m, V7X_LANES), lambda i, h, j: (i, h, 0, 0))],
        out_specs=pl.BlockSpec((1, 1, grp, tq, dv), lambda i, h, j: (i, h, 0, j, 0)),
        out_shape=jax.ShapeDtypeStruct((b, hkv, grp, nq, dv), BF16),
        scratch_shapes=[pltpu.VMEM((rows, m), F32)],
        compiler_params=_cparams(("parallel", "parallel", "parallel")),
        name="attention",
    )(q, k, v1)


def _na_kernel(q_ref, k_ref, v_ref, kc_ref, vc_ref, bias_ref, o_ref, *, rows):
    r = pl.program_id(1)
    r0 = jnp.clip(r - NA_WIN_R // 2, 0, rows - NA_WIN_R)
    off = pl.multiple_of(r0 * GRID_W, GRID_W)
    kwin = k_ref[0, pl.ds(off, NA_WIN_R * GRID_W), :]
    vwin = v_ref[0, pl.ds(off, NA_WIN_R * GRID_W), :]
    q = q_ref[0]
    kc = kc_ref[0]
    vc = vc_ref[0]
    outs = []
    for h in range(NA_HEADS):
        sl = slice(h * HEAD_DIM, (h + 1) * HEAD_DIM)
        qh = q[:, sl]
        s_loc = lax.dot_general(qh, kwin[:, sl], (((1,), (1,)), ((), ())), preferred_element_type=F32) + bias_ref[0, h]
        s_ctx = lax.dot_general(qh, kc[:, sl], (((1,), (1,)), ((), ())), preferred_element_type=F32)
        m = jnp.maximum(jnp.max(s_loc, axis=-1, keepdims=True), jnp.max(s_ctx, axis=-1, keepdims=True))
        p_loc = jnp.exp(s_loc - m)
        p_ctx = jnp.exp(s_ctx - m)
        den = jnp.sum(p_loc, axis=-1, keepdims=True) + jnp.sum(p_ctx, axis=-1, keepdims=True)
        o = (jnp.dot(p_loc.astype(BF16), vwin[:, sl], preferred_element_type=F32)
             + jnp.dot(p_ctx.astype(BF16), vc[:, sl], preferred_element_type=F32))
        outs.append(o / den)
    o_ref[0] = jnp.concatenate(outs, axis=-1).astype(o_ref.dtype)


def _na_bias_table(rel_bias, rows):
    kr = NA_WIN_R
    cpos = np.arange(GRID_W)
    c0 = np.clip(cpos - NA_WIN_C // 2, 0, GRID_W - NA_WIN_C)
    col_ok = (cpos[None, :] >= c0[:, None]) & (cpos[None, :] < c0[:, None] + NA_WIN_C)
    dc = np.clip(cpos[None, :] - cpos[:, None], -(NA_WIN_C - 1), NA_WIN_C - 1) + (NA_WIN_C - 1)
    n_dr, n_dc = rel_bias.shape[1:]
    select = np.zeros((n_dc, GRID_W * GRID_W), np.float32)
    select[dc.ravel(), np.arange(GRID_W * GRID_W)] = 1.0
    base = jnp.dot(rel_bias.reshape(NA_HEADS * n_dr, n_dc), jnp.asarray(select), precision=HIGHEST)
    base = jnp.where(col_ok[None, None], base.reshape(NA_HEADS, n_dr, GRID_W, GRID_W), NEG_INF)
    tabs = [jnp.transpose(base[:, o:o + kr], (0, 2, 1, 3)).reshape(NA_HEADS, GRID_W, kr * GRID_W) for o in range(kr)]
    return jnp.stack(tabs).astype(F32)


def _na_attention(q, k, v, kc, vc, bias_tab):
    b, n, w = q.shape
    l = kc.shape[1]
    rows = n // GRID_W
    assert rows >= NA_WIN_R

    def bias_map(i, r):
        r0 = jnp.clip(r - NA_WIN_R // 2, 0, rows - NA_WIN_R)
        return (r0 - r + NA_WIN_R - 1, 0, 0, 0)

    return pl.pallas_call(
        functools.partial(_na_kernel, rows=rows),
        grid=(b, rows),
        in_specs=[pl.BlockSpec((1, GRID_W, w), lambda i, r: (i, r, 0)),
                  pl.BlockSpec((1, n, w), lambda i, r: (i, 0, 0)),
                  pl.BlockSpec((1, n, w), lambda i, r: (i, 0, 0)),
                  pl.BlockSpec((1, l, w), lambda i, r: (i, 0, 0)),
                  pl.BlockSpec((1, l, w), lambda i, r: (i, 0, 0)),
                  pl.BlockSpec((1, NA_HEADS, GRID_W, NA_WIN_R * GRID_W), bias_map)],
        out_specs=pl.BlockSpec((1, GRID_W, w), lambda i, r: (i, r, 0)),
        out_shape=jax.ShapeDtypeStruct((b, n, w), BF16),
        compiler_params=_cparams(("parallel", "arbitrary")),
        name="na_attention",
    )(q, k, v, kc, vc, bias_tab)


DN_LEVELS = int(math.log2(DN_CHUNK))


def _dn_constants():
    c, w = DN_CHUNK, DN_W
    ii = np.arange(c)[:, None]
    jj = (np.arange(w) % c)[None, :]
    hh = (np.arange(w) // HEAD_DIM)[None, :]
    tri, expand, neg, strict, lvl = [], [], [], [], []
    for d in range(2):
        rev = d == 1
        incl = (ii <= jj) if rev else (ii >= jj)
        neg.append(np.where(incl, 0.0, NEG_INF))
        strict.append((ii < jj) if rev else (ii > jj))
        t = np.arange(c)
        tri.append((t[:, None] <= t[None, :]) if rev else (t[:, None] >= t[None, :]))
        expand.append(np.arange(V7X_LANES)[:, None] == hh + d * DN_HEADS)
        for lv in range(DN_LEVELS):
            same = (ii >> (lv + 1)) == (jj >> (lv + 1))
            hi_i = ((ii >> lv) & 1) == 1
            hi_j = ((jj >> lv) & 1) == 1
            lvl.append((same & ~hi_i & hi_j) if rev else (same & hi_i & ~hi_j))
    r2 = np.arange(w)
    block = (r2[:, None] // HEAD_DIM) == (r2[None, :] // HEAD_DIM)
    f = lambda a, dt: jnp.asarray(np.stack(a).astype(np.float32), dtype=dt)
    return dict(tri=f(tri, F32), expand=f(expand, F32), neg=f(neg, F32), strict=f(strict, F32),
                lvl=f(lvl, BF16).reshape(2, DN_LEVELS, c, w), eye=jnp.asarray((ii == jj).astype(np.float32)),
                block16=jnp.asarray(block.astype(np.float32), dtype=BF16), block32=jnp.asarray(block.astype(np.float32)))


def _dn_block_diag(y16, block16):
    return jnp.concatenate([y16] * DN_HEADS, axis=0) * block16


def _dn_prep_kernel(q_ref, k_ref, v_ref, g_ref, b_ref, tri_ref, exp_ref, neg_ref, strict_ref, lvl_ref, eye_ref,
                    blk_ref, *out_refs, chunks):
    c = DN_CHUNK
    nt = (((1,), (1,)), ((), ()))

    def bdmm(x, y):
        return jnp.dot(x.astype(BF16), _dn_block_diag(y.astype(BF16), blk_ref[...]), preferred_element_type=F32)

    def chunk_group(gi, carry):
        probs = []
        for sub in range(DN_PREP_INTERLEAVE):
            ci = gi * DN_PREP_INTERLEAVE + sub
            rows = pl.ds(pl.multiple_of(ci * c, c), c)
            q = q_ref[0, rows, :] * (HEAD_DIM ** -0.5)
            k = k_ref[0, rows, :]
            v = v_ref[0, rows, :]
            gcol = g_ref[0, rows, :]
            bcol = b_ref[0, rows, :]
            kbd = _dn_block_diag(k.astype(BF16), blk_ref[...])
            kk = lax.dot_general(k.astype(BF16), kbd, nt, preferred_element_type=F32)
            qk = lax.dot_general(q.astype(BF16), kbd, nt, preferred_element_type=F32)
            for d in range(2):
                gc = jnp.dot(tri_ref[d], gcol, preferred_element_type=F32, precision=HIGHEST)
                gcx = jnp.dot(gc, exp_ref[d], preferred_element_type=F32, precision=HIGHEST)
                bx = jnp.dot(bcol, exp_ref[d], preferred_element_type=F32, precision=HIGHEST)
                gr = jnp.sum(gcx * eye_ref[...], axis=0, keepdims=True)
                decay = jnp.exp(gcx - gr + neg_ref[d])
                a16 = (kk * bx * decay * strict_ref[d]).astype(BF16)
                probs.append(dict(d=d, ci=ci, rows=rows, q=q, k=k, v=v, qk=qk, gcx=gcx, bx=bx, decay=decay, a16=a16,
                                  t=eye_ref[...]))
        for lv in range(DN_LEVELS):
            for p in probs:
                p["x"] = jnp.dot(p["t"].astype(BF16), _dn_block_diag(p["a16"] * lvl_ref[p["d"], lv], blk_ref[...]),
                                 preferred_element_type=F32)
            for p in probs:
                p["t"] = p["t"] - bdmm(p["x"], p["t"])
        for p in probs:
            d, rows, gcx = p["d"], p["rows"], p["gcx"]
            u_ref, w_ref, a_ref, qg_ref, kd_ref, eg_ref = out_refs[6 * d:6 * d + 6]
            last = 0 if d == 1 else c - 1
            e_gc = jnp.exp(gcx)
            g_last = gcx[last:last + 1, :]
            u_ref[0, rows, :] = bdmm(p["t"], p["v"] * p["bx"]).astype(u_ref.dtype)
            w_ref[0, rows, :] = bdmm(p["t"], p["k"] * p["bx"] * e_gc).astype(w_ref.dtype)
            a_ref[0, rows, :] = (p["qk"] * p["decay"]).astype(a_ref.dtype)
            qg_ref[0, rows, :] = (p["q"] * e_gc).astype(qg_ref.dtype)
            kd_ref[0, rows, :] = (p["k"] * jnp.exp(g_last - gcx)).astype(kd_ref.dtype)
            eg_ref[0, p["ci"]] = jnp.exp(g_last)
        return carry

    lax.fori_loop(0, chunks // DN_PREP_INTERLEAVE, chunk_group, 0)


def _dn_scan_kernel(*refs, chunks):
    ins = (refs[0:6], refs[6:12])
    s0_refs = refs[12:14]
    b16_ref, b32_ref = refs[14:16]
    o_refs = refs[16:18]
    sf_refs = refs[18:20]
    s_scr = refs[20]
    c = DN_CHUNK
    blk = pl.program_id(1)

    @pl.when(blk == 0)
    def _():
        s_scr[0] = s0_refs[0][0]
        s_scr[1] = s0_refs[1][0]

    def chunk(ci, carry):
        st = []
        for d in range(2):
            idx = (chunks - 1 - ci) if d == 1 else ci
            st.append(dict(idx=idx, rows=pl.ds(pl.multiple_of(idx * c, c), c), s=s_scr[d]))
        for d, p in enumerate(st):
            u_ref, w_ref, a_ref, qg_ref, kd_ref, eg_ref = ins[d]
            wq = jnp.concatenate([w_ref[0, p["rows"], :], qg_ref[0, p["rows"], :]], axis=0)
            p["r"] = jnp.dot(wq, p["s"].astype(BF16), preferred_element_type=F32)
        for d, p in enumerate(st):
            u_ref = ins[d][0]
            p["vn"] = (u_ref[0, p["rows"], :].astype(F32) - p["r"][:c]).astype(BF16)
        for d, p in enumerate(st):
            kd_ref, eg_ref = ins[d][4], ins[d][5]
            kv = lax.dot_general(kd_ref[0, p["rows"], :], p["vn"], (((0,), (0,)), ((), ())), preferred_element_type=F32)
            s_scr[d] = p["s"] * eg_ref[0, p["idx"]] + kv * b32_ref[...]
        for d, p in enumerate(st):
            a_ref = ins[d][2]
            o_refs[d][0, p["rows"], :] = p["r"][c:] + jnp.dot(a_ref[0, p["rows"], :],
                                                              _dn_block_diag(p["vn"], b16_ref[...]),
                                                              preferred_element_type=F32)
        return carry

    lax.fori_loop(0, chunks, chunk, 0)

    @pl.when(blk == pl.num_programs(1) - 1)
    def _():
        sf_refs[0][0] = s_scr[0]
        sf_refs[1][0] = s_scr[1]


def _delta_bidir(q, k, v, g, beta, s0_f, s0_b):
    b, n, w = q.shape
    nchunks = n // DN_CHUNK
    chunks = min(DN_BLOCK_CHUNKS, nchunks)
    nblk = nchunks // chunks
    bt = chunks * DN_CHUNK

    tok = pl.BlockSpec((1, bt, w), lambda i, j: (i, j, 0))
    small = pl.BlockSpec((1, bt, V7X_LANES), lambda i, j: (i, j, 0))
    egl = pl.BlockSpec((1, chunks, 1, w), lambda i, j: (i, j, 0, 0))
    per_dir_shapes = [jax.ShapeDtypeStruct((b, n, w), BF16)] * 5 + [jax.ShapeDtypeStruct((b, nchunks, 1, w), F32)]
    cst = _dn_constants()

    def const_spec(a):
        nd = a.ndim
        return pl.BlockSpec(a.shape, lambda i, j: (0,) * nd)

    prep_consts = [cst[name] for name in ("tri", "expand", "neg", "strict", "lvl", "eye", "block16")]
    prep = pl.pallas_call(
        functools.partial(_dn_prep_kernel, chunks=chunks),
        grid=(b, nblk),
        in_specs=[tok, tok, tok, small, small] + [const_spec(a) for a in prep_consts],
        out_specs=([tok] * 5 + [egl]) * 2,
        out_shape=per_dir_shapes * 2,
        compiler_params=_cparams(("parallel", "parallel")),
        name="delta_prep",
    )(q, k, v, g, beta, *prep_consts)

    tok_r = pl.BlockSpec((1, bt, w), lambda i, j: (i, nblk - 1 - j, 0))
    egl_r = pl.BlockSpec((1, chunks, 1, w), lambda i, j: (i, nblk - 1 - j, 0, 0))
    state = pl.BlockSpec((1, w, w), lambda i, j: (i, 0, 0))
    scan_consts = [cst["block16"], cst["block32"]]
    o_f, o_b, s_f, s_b = pl.pallas_call(
        functools.partial(_dn_scan_kernel, chunks=chunks),
        grid=(b, nblk),
        in_specs=[tok] * 5 + [egl] + [tok_r] * 5 + [egl_r] + [state, state] + [const_spec(a) for a in scan_consts],
        out_specs=[tok, tok_r, state, state],
        out_shape=[jax.ShapeDtypeStruct((b, n, w), F32)] * 2 + [jax.ShapeDtypeStruct((b, w, w), F32)] * 2,
        scratch_shapes=[pltpu.VMEM((2, w, w), F32)],
        compiler_params=_cparams(("parallel", "arbitrary")),
        name="delta_scan",
    )(*prep, s0_f, s0_b, *scan_consts)
    return o_f + o_b, s_f, s_b


def _out_proj_kernel(x_ref, ya_ref, yb_ref, yc_ref, yd_ref, w_ref, gate_ref, g2_ref, sh_ref, sc_ref, rw_ref, rb_ref,
                     xo_ref, h2_ref, idx_ref, gt_ref):
    acc = None
    for i, y_ref in enumerate((ya_ref, yb_ref, yc_ref, yd_ref)):
        wd = y_ref.shape[-1]
        part = jnp.dot(y_ref[0], w_ref[i * wd:(i + 1) * wd, :], preferred_element_type=F32)
        acc = part if acc is None else acc + part
    xn = x_ref[0] + gate_ref[0] * acc
    xo_ref[0] = xn
    h2 = _rms(xn, g2_ref[...]) * (1.0 + sc_ref[0]) + sh_ref[0]
    h2_ref[0] = h2
    logits = jnp.dot(h2, rw_ref[...], preferred_element_type=F32, precision=HIGHEST) + rb_ref[...]
    lane = lax.broadcasted_iota(jnp.int32, logits.shape, 1).astype(F32)
    vals, idxs = [], []
    cur = logits
    for _ in range(TOP_K):
        mx = jnp.max(cur, axis=-1, keepdims=True)
        ik = jnp.min(jnp.where(cur == mx, lane, float(V7X_LANES)), axis=-1, keepdims=True)
        vals.append(mx)
        idxs.append(ik)
        cur = jnp.where(lane == ik, -jnp.inf, cur)
    es = [jnp.exp(vv - vals[0]) for vv in vals]
    den = es[0] + es[1] + es[2] + es[3]
    idx_out = jnp.zeros(logits.shape, F32)
    gate_out = jnp.zeros(logits.shape, F32)
    for kk in range(TOP_K):
        idx_out = jnp.where(lane == float(kk), idxs[kk], idx_out)
        gate_out = jnp.where(lane == float(kk), es[kk] / den, gate_out)
    idx_ref[0] = idx_out.astype(jnp.int32)
    gt_ref[0] = gate_out


def _out_proj(x, ys, w_out, gate, g2, shift, scale, rw_pad, rb_pad, tm):
    b, n, d = x.shape
    yw = ys[0].shape[-1]
    vec = pl.BlockSpec((1, 1, d), lambda i, j: (i, 0, 0))
    tok = pl.BlockSpec((1, tm, d), lambda i, j: (i, j, 0))
    ytok = pl.BlockSpec((1, tm, yw), lambda i, j: (i, j, 0))
    ltok = pl.BlockSpec((1, tm, V7X_LANES), lambda i, j: (i, j, 0))
    return pl.pallas_call(
        _out_proj_kernel,
        grid=(b, n // tm),
        in_specs=[tok, ytok, ytok, ytok, ytok,
                  pl.BlockSpec((4 * yw, d), lambda i, j: (0, 0)),
                  vec,
                  pl.BlockSpec((1, d), lambda i, j: (0, 0)),
                  vec, vec,
                  pl.BlockSpec((d, V7X_LANES), lambda i, j: (0, 0)),
                  pl.BlockSpec((1, V7X_LANES), lambda i, j: (0, 0))],
        out_specs=[tok, tok, ltok, ltok],
        out_shape=[jax.ShapeDtypeStruct((b, n, d), F32), jax.ShapeDtypeStruct((b, n, d), F32),
                   jax.ShapeDtypeStruct((b, n, V7X_LANES), jnp.int32), jax.ShapeDtypeStruct((b, n, V7X_LANES), F32)],
        compiler_params=_cparams(("parallel", "parallel")),
        name="out_proj",
    )(x, ys[0], ys[1], ys[2], ys[3], w_out, gate.reshape(b, 1, d), g2.reshape(1, d),
      shift.reshape(b, 1, d), scale.reshape(b, 1, d), rw_pad, rb_pad)


def _experts_kernel(te_ref, tf_ref, tv_ref, src0_ref, srcn_ref, h_hbm, w1_ref, b1_ref, w2_ref, b2_ref, o_ref,
                    xbuf, w1b, w2b, sem, *, fc):
    i = pl.program_id(0)
    nt = pl.num_programs(0)
    tm = o_ref.shape[0]
    f = w2_ref.shape[1]
    slot = i % 2

    def row_copy(src_ref, r, dst_slot):
        return pltpu.make_async_copy(h_hbm.at[pl.ds(src_ref[0, 0, r], 1)], xbuf.at[dst_slot, pl.ds(r, 1)],
                                     sem.at[dst_slot])

    def issue(src_ref, dst_slot):
        def body(r8, carry):
            for u in range(MOE_ISSUE_UNROLL):
                row_copy(src_ref, r8 * MOE_ISSUE_UNROLL + u, dst_slot).start()
            return carry
        lax.fori_loop(0, tm // MOE_ISSUE_UNROLL, body, 0)

    @pl.when(jnp.logical_and(i == 0, tv_ref[0] != 0))
    def _():
        issue(src0_ref, 0)

    nxt = jnp.minimum(i + 1, nt - 1)

    @pl.when(jnp.logical_and(i + 1 < nt, tv_ref[nxt] != 0))
    def _():
        issue(srcn_ref, 1 - slot)

    @pl.when(tv_ref[i] == 0)
    def _():
        o_ref[...] = jnp.zeros(o_ref.shape, o_ref.dtype)

    @pl.when(tv_ref[i] != 0)
    def _():
        @pl.when(tf_ref[i] != 0)
        def _():
            w1b[...] = w1_ref[0].astype(BF16)
            w2b[...] = w2_ref[0].astype(BF16)

        pltpu.make_async_copy(h_hbm.at[pl.ds(0, tm)], xbuf.at[slot], sem.at[slot]).wait()
        xb = xbuf[slot].astype(BF16)
        acc = None
        for j in range(f // fc):
            glu = jnp.dot(xb, w1b[:, j * fc:(j + 1) * fc], preferred_element_type=F32) + b1_ref[0, :, j * fc:(j + 1) * fc]
            lin = (jnp.dot(xb, w1b[:, f + j * fc:f + (j + 1) * fc], preferred_element_type=F32)
                   + b1_ref[0, :, f + j * fc:f + (j + 1) * fc])
            glu = jnp.minimum(glu, SWIGLU_LIMIT)
            lin = jnp.clip(lin, -SWIGLU_LIMIT, SWIGLU_LIMIT)
            act = glu * jax.nn.sigmoid(SWIGLU_ALPHA * glu) * (lin + 1.0)
            part = jnp.dot(act.astype(BF16), w2b[j * fc:(j + 1) * fc, :], preferred_element_type=F32)
            acc = part if acc is None else acc + part
        o_ref[...] = acc + b2_ref[0]


def _expert_tiles(h, src, tile_e, tile_first, tile_valid, w1, b1, w2, b2, tm):
    t, d = h.shape
    r = src.shape[0]
    e, _, f2 = w1.shape
    f = f2 // 2
    nt = r // tm
    grid_spec = pltpu.PrefetchScalarGridSpec(
        num_scalar_prefetch=3,
        grid=(nt,),
        in_specs=[pl.BlockSpec((1, 1, tm), lambda i, te, tf, tv: (0, 0, 0), memory_space=pltpu.SMEM),
                  pl.BlockSpec((1, 1, tm), lambda i, te, tf, tv: (jnp.minimum(i + 1, nt - 1), 0, 0),
                               memory_space=pltpu.SMEM),
                  pl.BlockSpec(memory_space=pl.ANY),
                  pl.BlockSpec((1, d, f2), lambda i, te, tf, tv: (te[i], 0, 0)),
                  pl.BlockSpec((1, 1, f2), lambda i, te, tf, tv: (te[i], 0, 0)),
                  pl.BlockSpec((1, f, d), lambda i, te, tf, tv: (te[i], 0, 0)),
                  pl.BlockSpec((1, 1, d), lambda i, te, tf, tv: (te[i], 0, 0))],
        out_specs=pl.BlockSpec((tm, d), lambda i, te, tf, tv: (i, 0)),
        scratch_shapes=[pltpu.VMEM((2, tm, d), F32), pltpu.VMEM((d, f2), BF16), pltpu.VMEM((f, d), BF16),
                        pltpu.SemaphoreType.DMA((2,))],
    )
    src3 = src.reshape(nt, 1, tm)
    return pl.pallas_call(
        functools.partial(_experts_kernel, fc=MOE_FC),
        grid_spec=grid_spec,
        out_shape=jax.ShapeDtypeStruct((r, d), F32),
        compiler_params=_cparams(("arbitrary",)),
        name="moe_experts",
    )(tile_e, tile_first, tile_valid, src3, src3, h, w1, b1.reshape(e, 1, f2), w2, b2.reshape(e, 1, d))


def _combine_kernel(pos_ref, ys_hbm, gate_ref, o_ref, buf, sem):
    ct = o_ref.shape[0]

    def issue(r2, carry):
        for u in range(COMBINE_ISSUE_UNROLL):
            r = r2 * COMBINE_ISSUE_UNROLL + u
            for kk in range(TOP_K):
                p = pos_ref[0, 0, r * TOP_K + kk]
                pltpu.make_async_copy(ys_hbm.at[pl.ds(p, 1)], buf.at[kk, pl.ds(r, 1)], sem).start()
        return carry

    lax.fori_loop(0, ct // COMBINE_ISSUE_UNROLL, issue, 0)
    for kk in range(TOP_K):
        pltpu.make_async_copy(ys_hbm.at[pl.ds(0, ct)], buf.at[kk], sem).wait()
    g = gate_ref[...]
    o_ref[...] = ((g[:, 0:1] * buf[0] + g[:, 1:2] * buf[1]) + (g[:, 2:3] * buf[2] + g[:, 3:4] * buf[3]))


def _combine_rows(ys, pos, gates, ct):
    t = pos.shape[0]
    d = ys.shape[1]
    nsteps = t // ct
    return pl.pallas_call(
        _combine_kernel,
        grid=(nsteps,),
        in_specs=[pl.BlockSpec((1, 1, ct * TOP_K), lambda i: (i, 0, 0), memory_space=pltpu.SMEM),
                  pl.BlockSpec(memory_space=pl.ANY),
                  pl.BlockSpec((ct, V7X_LANES), lambda i: (i, 0))],
        out_specs=pl.BlockSpec((ct, d), lambda i: (i, 0)),
        out_shape=jax.ShapeDtypeStruct((t, d), F32),
        scratch_shapes=[pltpu.VMEM((TOP_K, ct, d), F32), pltpu.SemaphoreType.DMA],
        compiler_params=_cparams(("arbitrary",)),
        name="moe_combine",
    )(pos.reshape(nsteps, 1, ct * TOP_K), ys, gates)


def _route_plan(idx4, tm):
    t = idx4.shape[0]
    e = N_EXPERTS
    r_max = t * TOP_K + e * tm
    nt = r_max // tm
    onehot = (idx4[:, :, None] == jnp.arange(e, dtype=jnp.int32)[None, None, :]).astype(jnp.int32)
    member = jnp.sum(onehot, axis=1)
    csum = jnp.cumsum(member, axis=0)
    cnt = csum[-1]
    excl = csum - member
    cnt_pad = ((cnt + tm - 1) // tm) * tm
    ends = jnp.cumsum(cnt_pad)
    base = ends - cnt_pad
    pos = jnp.sum(onehot * (excl + base[None, :])[:, None, :], axis=-1)
    flat = pos.reshape(-1)
    tok = jnp.repeat(jnp.arange(t, dtype=jnp.int32), TOP_K)
    src = jnp.zeros((r_max,), jnp.int32).at[flat].set(tok, unique_indices=True)
    tile_start = jnp.arange(nt, dtype=jnp.int32) * tm
    tile_e = jnp.minimum(jnp.sum((ends[None, :] <= tile_start[:, None]).astype(jnp.int32), axis=1), e - 1)
    tile_valid = (tile_start < ends[-1]).astype(jnp.int32)
    tile_first = jnp.concatenate([jnp.ones((1,), jnp.int32), (tile_e[1:] != tile_e[:-1]).astype(jnp.int32)])
    return pos.astype(jnp.int32), src, tile_e, tile_first, tile_valid


def _moe(h2, idx4, gates, w1, b1, w2, b2):
    pos, src, tile_e, tile_first, tile_valid = _route_plan(idx4, MOE_TM)
    ys = _expert_tiles(h2, src, tile_e, tile_first, tile_valid, w1, b1, w2, b2, MOE_TM)
    return _combine_rows(ys, pos, gates, COMBINE_ROWS)


def _axial_rope_tables(n_tok, rot_dim):
    t = jnp.arange(n_tok, dtype=jnp.int32)
    row = (t // GRID_W).astype(F32)
    col = (t % GRID_W).astype(F32)
    n_freq = rot_dim // 4
    freqs = ROPE_THETA ** (-jnp.arange(n_freq, dtype=F32) / n_freq)
    ang = jnp.concatenate([row[:, None] * freqs, col[:, None] * freqs], axis=-1)
    return jnp.cos(ang), jnp.sin(ang)


def _apply_rope(x, cos, sin):
    xf = x.reshape(*x.shape[:-1], -1, 2)
    x1, x2 = xf[..., 0], xf[..., 1]
    cs, sn = cos[:, None, :], sin[:, None, :]
    out = jnp.stack([x1 * cs - x2 * sn, x1 * sn + x2 * cs], axis=-1)
    return out.reshape(x.shape)


def _heads(t, n):
    return t.reshape(t.shape[0], t.shape[1], n, t.shape[2] // n)


def _head_major(t):
    return jnp.transpose(t, (0, 2, 1, 3))


def _permute_w_in(w_in):
    offs = (0,) + IN_OFFSETS
    seg = {i: (offs[i], offs[i] + IN_SIZES[i]) for i in range(len(IN_SIZES))}
    order = [0, 1, 2, 3, 4, 5, 6, 7, 8, 9, 12, 13, 14, 10, 11]
    cols = np.concatenate([np.arange(*seg[i]) for i in order])
    w = jnp.take(w_in, jnp.asarray(cols), axis=1)
    pad = sum(PROJ_GROUPS) - w.shape[1]
    return jnp.pad(w, ((0, 0), (0, pad))).astype(BF16)


def _dn_prep(qkv, small, conv_w, a_log, dt_bias):
    b, n, _ = qkv.shape
    pad = DN_CONV // 2
    xp = jnp.pad(qkv, ((0, 0), (pad, pad), (0, 0)))
    conv = sum(xp[:, i:i + n, :] * conv_w[i][None, None, :] for i in range(DN_CONV))
    act = jax.nn.silu(conv)
    q, k, v = jnp.split(act, 3, axis=-1)

    def l2(t):
        th = _heads(t, DN_HEADS)
        return (th * lax.rsqrt(jnp.sum(th * th, axis=-1, keepdims=True) + EPS)).reshape(b, n, DN_W)

    beta_raw = small[..., MLA_ROPE:MLA_ROPE + 2 * DN_HEADS]
    a_raw = small[..., MLA_ROPE + 2 * DN_HEADS:MLA_ROPE + 4 * DN_HEADS]
    beta = jax.nn.sigmoid(beta_raw)
    g = -jnp.exp(a_log.reshape(-1)) * jax.nn.softplus(a_raw + dt_bias.reshape(-1))
    lane_pad = ((0, 0), (0, 0), (0, V7X_LANES - 2 * DN_HEADS))
    return l2(q), l2(k), v, jnp.pad(g, lane_pad), jnp.pad(beta, lane_pad)


def _mla_kv(ckv, kpe, ckv_g, w_ukv, kn_g, rope):
    b, n, _ = ckv.shape
    kv = _norm_mm(ckv.reshape(b * n, MLA_KV_RANK), ckv_g, w_ukv, min(1024, b * n)).reshape(b, n, MLA_HEADS, MLA_NOPE + MLA_V)
    k_pe = jnp.broadcast_to(kpe[:, :, None, :], (b, n, MLA_HEADS, MLA_ROPE))
    k = _rms(jnp.concatenate([kv[..., :MLA_NOPE], k_pe], axis=-1), kn_g)
    if rope is not None:
        k = jnp.concatenate([k[..., :MLA_NOPE], _apply_rope(k[..., MLA_NOPE:], *rope)], axis=-1)
    return k, kv[..., MLA_NOPE:]


def _mla_q(cq, cq_g, w_uq, qn_g, rope):
    b, n, _ = cq.shape
    q = _norm_mm(cq.reshape(b * n, MLA_Q_RANK), cq_g, w_uq, min(1024, b * n)).reshape(b, n, MLA_HEADS, MLA_QK)
    q = _rms(q, qn_g)
    if rope is not None:
        q = jnp.concatenate([q[..., :MLA_NOPE], _apply_rope(q[..., MLA_NOPE:], *rope)], axis=-1)
    return q


def _attn_bnhd(q, k, v, groups):
    b, nq, hq, dk = q.shape
    hkv = k.shape[2]
    dv = v.shape[-1]
    qh = _head_major(q * (dk ** -0.5)).astype(BF16).reshape(b, hkv, groups, nq, dk)
    ones_col = jnp.concatenate([jnp.ones(v.shape[:-1] + (1,), F32), jnp.zeros(v.shape[:-1] + (V7X_LANES - dv - 1,), F32)], axis=-1)
    v1 = _head_major(jnp.concatenate([v, ones_col], axis=-1)).astype(BF16)
    o = _attention(qh, _head_major(k).astype(BF16), v1, dv)
    return jnp.transpose(o.reshape(b, hq, nq, dv), (0, 2, 1, 3)).reshape(b, nq, hq * dv)


def kernel(x, c, ctx, c_ctx, ada_w, ada_b, norm1_g, norm2_g, w_in, w_out, na_qn_g, na_kn_g, na_rel_bias, gqa_qn_g, gqa_kn_g, dn_conv_w, dn_a_log, dn_dt_bias, dn_out_g, mla_cq_g, mla_ckv_g, mla_w_uq, mla_w_ukv, mla_qn_g, mla_kn_g, router_w, router_b, exp_w1, exp_b1, exp_w2, exp_b2):
    b, n, d = x.shape
    l = ctx.shape[1]
    depth = ada_w.shape[0]
    grp = GQA_Q_HEADS // GQA_KV_HEADS
    rope_g = _axial_rope_tables(n, HEAD_DIM)
    rope_m = _axial_rope_tables(n, MLA_ROPE)
    cond = jnp.concatenate([jax.nn.silu(c), jax.nn.silu(c_ctx)[None], jnp.zeros((16 - b - 1, d), F32)], axis=0)
    s0 = jnp.zeros((b, DN_W, DN_W), F32)
    scale = HEAD_DIM ** -0.5

    for ly in range(depth):
        with_ctx = ly < depth - 1
        mod_all = _mm_bias(cond, ada_w[ly], ada_b[ly], 1024)
        mod = jnp.split(mod_all[:b], N_ADA, axis=-1)
        mod_c = [jnp.broadcast_to(m_, (b, d)) for m_ in jnp.split(mod_all[b:b + 1], N_ADA, axis=-1)]
        w_perm = _permute_w_in(w_in[ly])
        p = _in_proj(x, norm1_g[ly], mod[0], mod[1], w_perm, PROJ_TM)
        pc = _in_proj(ctx, norm1_g[ly], mod_c[0], mod_c[1], w_perm, l)
        bias_tab = _na_bias_table(na_rel_bias[ly], n // GRID_W)

        qa = _rms(_heads(p[0], NA_HEADS), na_qn_g[ly])
        ka = _rms(_heads(p[1], NA_HEADS), na_kn_g[ly])
        kac = _rms(_heads(pc[1], NA_HEADS), na_kn_g[ly])
        ya = _na_attention((qa * scale).reshape(b, n, NA_W).astype(BF16), ka.reshape(b, n, NA_W).astype(BF16),
                           p[2].astype(BF16), kac.reshape(b, l, NA_W).astype(BF16), pc[2].astype(BF16), bias_tab)

        qb = _apply_rope(_rms(_heads(p[3], GQA_Q_HEADS), gqa_qn_g[ly]), *rope_g)
        kb = _apply_rope(_rms(_heads(p[4][..., :GQA_KVW], GQA_KV_HEADS), gqa_kn_g[ly]), *rope_g)
        vb = _heads(p[4][..., GQA_KVW:], GQA_KV_HEADS)
        kbc = _rms(_heads(pc[4][..., :GQA_KVW], GQA_KV_HEADS), gqa_kn_g[ly])
        vbc = _heads(pc[4][..., GQA_KVW:], GQA_KV_HEADS)
        yb = _attn_bnhd(qb, jnp.concatenate([kb, kbc], axis=1), jnp.concatenate([vb, vbc], axis=1), grp)

        qdc, kdc, vdc, gdc, bdc = _dn_prep(pc[5], pc[9], dn_conv_w[ly], dn_a_log[ly], dn_dt_bias[ly])
        odc, s_f, s_b = _delta_bidir(qdc, kdc, vdc, gdc, bdc, s0, s0)
        qd, kd, vd, gd, bd = _dn_prep(p[5], p[9], dn_conv_w[ly], dn_a_log[ly], dn_dt_bias[ly])
        od, _, _ = _delta_bidir(qd, kd, vd, gd, bd, s_f, s_b)
        yc = (_rms(_heads(od, DN_HEADS), dn_out_g[ly]) * jax.nn.silu(_heads(p[6], DN_HEADS))).reshape(b, n, DN_W).astype(BF16)

        qm = _mla_q(p[7], mla_cq_g[ly], mla_w_uq[ly], mla_qn_g[ly], rope_m)
        km, vm = _mla_kv(p[8], p[9][..., :MLA_ROPE], mla_ckv_g[ly], mla_w_ukv[ly], mla_kn_g[ly], rope_m)
        kmc, vmc = _mla_kv(pc[8], pc[9][..., :MLA_ROPE], mla_ckv_g[ly], mla_w_ukv[ly], mla_kn_g[ly], None)
        yd = _attn_bnhd(qm, jnp.concatenate([km, kmc], axis=1), jnp.concatenate([vm, vmc], axis=1), 1)

        rw_pad = jnp.pad(router_w[ly], ((0, 0), (0, V7X_LANES - N_EXPERTS)))
        rb_pad = jnp.pad(router_b[ly], (0, V7X_LANES - N_EXPERTS), constant_values=NEG_INF).reshape(1, V7X_LANES)
        w_out_b = w_out[ly].astype(BF16)
        x, h2, idx, gts = _out_proj(x, (ya, yb, yc, yd), w_out_b, mod[2], norm2_g[ly], mod[3], mod[4],
                                    rw_pad, rb_pad, PROJ_TM)
        tok = h2.reshape(b * n, d)
        idx4 = idx.reshape(b * n, V7X_LANES)[:, :TOP_K]
        gates = gts.reshape(b * n, V7X_LANES)

        if with_ctx:
            qac = _rms(_heads(pc[0], NA_HEADS), na_qn_g[ly])
            yac = _attn_bnhd(qac, kac, _heads(pc[2], NA_HEADS), 1)
            qbc = _rms(_heads(pc[3], GQA_Q_HEADS), gqa_qn_g[ly])
            ybc = _attn_bnhd(qbc, kbc, vbc, grp)
            ycc = (_rms(_heads(odc, DN_HEADS), dn_out_g[ly]) * jax.nn.silu(_heads(pc[6], DN_HEADS))).reshape(b, l, DN_W).astype(BF16)
            qmc = _mla_q(pc[7], mla_cq_g[ly], mla_w_uq[ly], mla_qn_g[ly], None)
            ydc = _attn_bnhd(qmc, kmc, vmc, 1)
            ctx, h2c, idxc, gtsc = _out_proj(ctx, (yac, ybc, ycc, ydc), w_out_b, mod_c[2], norm2_g[ly], mod_c[3], mod_c[4],
                                             rw_pad, rb_pad, l)
            tok = jnp.concatenate([tok, h2c.reshape(b * l, d)], axis=0)
            idx4 = jnp.concatenate([idx4, idxc.reshape(b * l, V7X_LANES)[:, :TOP_K]], axis=0)
            gates = jnp.concatenate([gates, gtsc.reshape(b * l, V7X_LANES)], axis=0)

        f = _moe(tok, idx4, gates, exp_w1[ly], exp_b1[ly], exp_w2[ly], exp_b2[ly])
        x = x + mod[5][:, None] * f[:b * n].reshape(b, n, d)
        if with_ctx:
            ctx = ctx + mod_c[5][:, None] * f[b * n:].reshape(b, l, d)
    return x
```

```python
import functools
import math

import jax
import jax.numpy as jnp
import numpy as np
from jax import lax
from jax.experimental import pallas as pl
from jax.experimental.pallas import tpu as pltpu

F32 = jnp.float32
BF16 = jnp.bfloat16
HIGHEST = lax.Precision.HIGHEST

GRID_W = 64
HEAD_DIM = 64
NA_HEADS = 4
NA_WIN_R = 8
NA_WIN_C = 16
GQA_Q_HEADS = 4
GQA_KV_HEADS = 2
DN_HEADS = 4
DN_CONV = 5
DN_CHUNK = 64
MLA_HEADS = 4
MLA_Q_RANK = 256
MLA_KV_RANK = 128
MLA_NOPE = 64
MLA_ROPE = 32
MLA_V = 64
MLA_QK = MLA_NOPE + MLA_ROPE
N_EXPERTS = 32
TOP_K = 4
SWIGLU_LIMIT = 7.0
SWIGLU_ALPHA = 1.702
ROPE_THETA = 10000.0
EPS = 1e-6
NEG_INF = -1e30
N_ADA = 6

NA_W = NA_HEADS * HEAD_DIM
GQA_QW = GQA_Q_HEADS * HEAD_DIM
GQA_KVW = GQA_KV_HEADS * HEAD_DIM
DN_W = DN_HEADS * HEAD_DIM
MLA_W = MLA_HEADS * MLA_V
IN_SIZES = (NA_W, NA_W, NA_W, GQA_QW, GQA_KVW, GQA_KVW, DN_W, DN_W, DN_W, DN_W, 2 * DN_HEADS, 2 * DN_HEADS,
            MLA_Q_RANK, MLA_KV_RANK, MLA_ROPE)
IN_OFFSETS = tuple(int(o) for o in np.cumsum(IN_SIZES)[:-1])

V7X_VMEM_BYTES = 64 * 1024 * 1024
V7X_LANES = 128
VMEM_LIMIT = V7X_VMEM_BYTES - 8 * 1024 * 1024

PROJ_TM = 512
ATTN_ROWS = 256
ATTN_TK = 256
NA_GROUP = 8
NA_KEY_ROWS = 16
DN_BLOCK_CHUNKS = 8
DN_PREP_INTERLEAVE = 2
MOE_TM = 512
MOE_FC = 512
MOE_ISSUE_UNROLL = 8
COMBINE_ROWS = 256
COMBINE_ISSUE_UNROLL = 2

PROJ_GROUPS = (NA_W, NA_W, NA_W, GQA_QW, 2 * GQA_KVW, 3 * DN_W, DN_W, MLA_Q_RANK, MLA_KV_RANK, V7X_LANES)
SMALL_USED = MLA_ROPE + 4 * DN_HEADS


def _cparams(semantics):
    return pltpu.CompilerParams(dimension_semantics=semantics, vmem_limit_bytes=VMEM_LIMIT)


def _rms(x, g):
    return x * lax.rsqrt(jnp.mean(x * x, axis=-1, keepdims=True) + EPS) * g


def _mm_bias_kernel(x_ref, w_ref, b_ref, o_ref):
    o_ref[...] = jnp.dot(x_ref[...].astype(BF16), w_ref[...].astype(BF16), preferred_element_type=F32) + b_ref[...]


def _mm_bias(x, w, b, tn):
    m, k = x.shape
    n = w.shape[1]
    return pl.pallas_call(
        _mm_bias_kernel,
        grid=(n // tn,),
        in_specs=[pl.BlockSpec((m, k), lambda j: (0, 0)),
                  pl.BlockSpec((k, tn), lambda j: (0, j)),
                  pl.BlockSpec((1, tn), lambda j: (0, j))],
        out_specs=pl.BlockSpec((m, tn), lambda j: (0, j)),
        out_shape=jax.ShapeDtypeStruct((m, n), F32),
        compiler_params=_cparams(("parallel",)),
        name="ada_mm",
    )(x, w, b.reshape(1, n))


def _norm_mm_kernel(x_ref, g_ref, w_ref, o_ref):
    y = _rms(x_ref[...], g_ref[...])
    o_ref[...] = jnp.dot(y.astype(BF16), w_ref[...].astype(BF16), preferred_element_type=F32)


def _norm_mm(x, g, w, tm):
    m, k = x.shape
    n = w.shape[1]
    return pl.pallas_call(
        _norm_mm_kernel,
        grid=(m // tm,),
        in_specs=[pl.BlockSpec((tm, k), lambda i: (i, 0)),
                  pl.BlockSpec((1, k), lambda i: (0, 0)),
                  pl.BlockSpec((k, n), lambda i: (0, 0))],
        out_specs=pl.BlockSpec((tm, n), lambda i: (i, 0)),
        out_shape=jax.ShapeDtypeStruct((m, n), F32),
        compiler_params=_cparams(("parallel",)),
        name="norm_mm",
    )(x, g.reshape(1, k), w)


def _in_proj_kernel(x_ref, g_ref, sh_ref, sc_ref, w_ref, *out_refs):
    h = _rms(x_ref[0], g_ref[...]) * (1.0 + sc_ref[0]) + sh_ref[0]
    hb = h.astype(BF16)
    off = 0
    for o_ref in out_refs:
        wd = o_ref.shape[-1]
        o_ref[0] = jnp.dot(hb, w_ref[:, off:off + wd], preferred_element_type=F32)
        off += wd


def _in_proj(x, g, shift, scale, w_perm, tm):
    b, n, d = x.shape
    wtot = w_perm.shape[1]
    vec = pl.BlockSpec((1, 1, d), lambda i, j: (i, 0, 0))
    return pl.pallas_call(
        _in_proj_kernel,
        grid=(b, n // tm),
        in_specs=[pl.BlockSpec((1, tm, d), lambda i, j: (i, j, 0)),
                  pl.BlockSpec((1, d), lambda i, j: (0, 0)),
                  vec, vec,
                  pl.BlockSpec((d, wtot), lambda i, j: (0, 0))],
        out_specs=[pl.BlockSpec((1, tm, wd), lambda i, j: (i, j, 0)) for wd in PROJ_GROUPS],
        out_shape=[jax.ShapeDtypeStruct((b, n, wd), F32) for wd in PROJ_GROUPS],
        compiler_params=_cparams(("parallel", "parallel")),
        name="in_proj",
    )(x, g.reshape(1, d), shift.reshape(b, 1, d), scale.reshape(b, 1, d), w_perm)


def _attn_kernel(q_ref, k_ref, v_ref, o_ref, s_scr, *, tk, dv):
    grp, tq, dk = q_ref.shape[2:]
    rows = grp * tq
    n_keys = k_ref.shape[2]
    q = q_ref[0, 0].reshape(rows, dk)
    nk = n_keys // tk
    mx = None
    for j in range(nk):
        s = lax.dot_general(q, k_ref[0, 0, j * tk:(j + 1) * tk, :], (((1,), (1,)), ((), ())),
                            preferred_element_type=F32)
        s_scr[:, j * tk:(j + 1) * tk] = s
        for t in range(tk // V7X_LANES):
            slab = s[:, t * V7X_LANES:(t + 1) * V7X_LANES]
            mx = slab if mx is None else jnp.maximum(mx, slab)
    mb = jnp.broadcast_to(jnp.max(mx, axis=-1, keepdims=True), (rows, V7X_LANES))
    acc = None
    for j in range(nk):
        slabs = [jnp.exp(s_scr[:, j * tk + t * V7X_LANES:j * tk + (t + 1) * V7X_LANES] - mb)
                 for t in range(tk // V7X_LANES)]
        p = jnp.concatenate(slabs, axis=-1).astype(BF16)
        part = jnp.dot(p, v_ref[0, 0, j * tk:(j + 1) * tk, :], preferred_element_type=F32)
        acc = part if acc is None else acc + part
    o = acc[:, :dv] / acc[:, dv:dv + 1]
    o_ref[0, 0] = o.reshape(grp, tq, dv).astype(o_ref.dtype)


def _attention(q, k, v1, dv, tk=ATTN_TK):
    b, hkv, grp, nq, dk = q.shape
    m = k.shape[2]
    tq = min(ATTN_ROWS // grp, nq)
    tk = min(tk, m)
    rows = grp * tq
    return pl.pallas_call(
        functools.partial(_attn_kernel, tk=tk, dv=dv),
        grid=(b, hkv, nq // tq),
        in_specs=[pl.BlockSpec((1, 1, grp, tq, dk), lambda i, h, j: (i, h, 0, j, 0)),
                  pl.BlockSpec((1, 1, m, dk), lambda i, h, j: (i, h, 0, 0)),
                  pl.BlockSpec((1, 1, m, V7X_LANES), lambda i, h, j: (i, h, 0, 0))],
        out_specs=pl.BlockSpec((1, 1, grp, tq, dv), lambda i, h, j: (i, h, 0, j, 0)),
        out_shape=jax.ShapeDtypeStruct((b, hkv, grp, nq, dv), BF16),
        scratch_shapes=[pltpu.VMEM((rows, m), F32)],
        compiler_params=_cparams(("parallel", "parallel", "parallel")),
        name="attention",
    )(q, k, v1)


def _na_kernel(q_ref, k_ref, v_ref, kc_ref, vc_ref, bias_ref, o_ref, s_scr, *, rows):
    g = pl.program_id(1)
    w0 = jnp.clip(g * NA_GROUP - NA_WIN_R // 2, 0, rows - NA_KEY_ROWS)
    off = pl.multiple_of(w0 * GRID_W, GRID_W)
    nq = NA_GROUP * GRID_W
    nloc = NA_KEY_ROWS * GRID_W
    l = kc_ref.shape[1]
    tk = min(ATTN_TK, l)
    nt = (((1,), (1,)), ((), ()))
    outs = []
    for h in range(NA_HEADS):
        hs = slice(h * V7X_LANES, (h + 1) * V7X_LANES)
        qh = q_ref[0, :, hs]
        mx = None
        for j in range((nloc + l) // tk):
            if j * tk < nloc:
                kj = k_ref[0, pl.ds(off + j * tk, tk), hs]
                s = lax.dot_general(qh, kj, nt, preferred_element_type=F32) + bias_ref[0, h, :, j * tk:(j + 1) * tk]
            else:
                kj = kc_ref[0, j * tk - nloc:(j + 1) * tk - nloc, hs]
                s = lax.dot_general(qh, kj, nt, preferred_element_type=F32)
            s_scr[:, j * tk:(j + 1) * tk] = s
            for t in range(tk // V7X_LANES):
                slab = s[:, t * V7X_LANES:(t + 1) * V7X_LANES]
                mx = slab if mx is None else jnp.maximum(mx, slab)
        mb = jnp.broadcast_to(jnp.max(mx, axis=-1, keepdims=True), (nq, V7X_LANES))
        acc = None
        for j in range((nloc + l) // tk):
            slabs = [jnp.exp(s_scr[:, j * tk + t * V7X_LANES:j * tk + (t + 1) * V7X_LANES] - mb)
                     for t in range(tk // V7X_LANES)]
            p = jnp.concatenate(slabs, axis=-1).astype(BF16)
            if j * tk < nloc:
                vj = v_ref[0, pl.ds(off + j * tk, tk), hs]
            else:
                vj = vc_ref[0, j * tk - nloc:(j + 1) * tk - nloc, hs]
            part = jnp.dot(p, vj, preferred_element_type=F32)
            acc = part if acc is None else acc + part
        outs.append(acc[:, :HEAD_DIM] / acc[:, HEAD_DIM:HEAD_DIM + 1])
    o_ref[0] = jnp.concatenate(outs, axis=-1).astype(o_ref.dtype)


def _na_bias_table(rel_bias, rows):
    kr = NA_WIN_R
    cpos = np.arange(GRID_W)
    c0 = np.clip(cpos - NA_WIN_C // 2, 0, GRID_W - NA_WIN_C)
    col_ok = (cpos[None, :] >= c0[:, None]) & (cpos[None, :] < c0[:, None] + NA_WIN_C)
    dc = np.clip(cpos[None, :] - cpos[:, None], -(NA_WIN_C - 1), NA_WIN_C - 1) + (NA_WIN_C - 1)
    n_dr, n_dc = rel_bias.shape[1:]
    select = np.zeros((n_dc, GRID_W * GRID_W), np.float32)
    select[dc.ravel(), np.arange(GRID_W * GRID_W)] = 1.0
    base = jnp.dot(rel_bias.reshape(NA_HEADS * n_dr, n_dc), jnp.asarray(select), precision=HIGHEST)
    base = jnp.where(col_ok[None, None], base.reshape(NA_HEADS, n_dr, GRID_W, GRID_W), NEG_INF)
    ngroups = rows // NA_GROUP
    pick = np.zeros((3, NA_GROUP, NA_KEY_ROWS, n_dr + 1), np.float32)
    for var, g in enumerate((0, min(1, ngroups - 1), ngroups - 1)):
        w0 = int(np.clip(g * NA_GROUP - kr // 2, 0, rows - NA_KEY_ROWS))
        for qr in range(NA_GROUP):
            r = g * NA_GROUP + qr
            r0 = int(np.clip(r - kr // 2, 0, rows - kr))
            for kl in range(NA_KEY_ROWS):
                key_row = w0 + kl
                pick[var, qr, kl, key_row - r + kr - 1 if r0 <= key_row < r0 + kr else n_dr] = 1.0
    base_ext = jnp.concatenate([jnp.transpose(base, (1, 0, 2, 3)).reshape(n_dr, -1),
                                jnp.full((1, NA_HEADS * GRID_W * GRID_W), NEG_INF, F32)], axis=0)
    tab = jnp.dot(jnp.asarray(pick.reshape(-1, n_dr + 1)), base_ext, precision=HIGHEST)
    tab = tab.reshape(3, NA_GROUP, NA_KEY_ROWS, NA_HEADS, GRID_W, GRID_W)
    return jnp.transpose(tab, (0, 3, 1, 4, 2, 5)).reshape(3, NA_HEADS, NA_GROUP * GRID_W, NA_KEY_ROWS * GRID_W)


def _na_attention(q, k, v, kc, vc, bias_tab):
    b, n = q.shape[:2]
    l = kc.shape[1]
    rows = n // GRID_W
    assert rows % NA_GROUP == 0 and rows >= NA_KEY_ROWS
    ngroups = rows // NA_GROUP
    w = NA_HEADS * V7X_LANES
    nq = NA_GROUP * GRID_W

    def lanes(t, ones):
        fill = jnp.zeros(t.shape[:-1] + (V7X_LANES - HEAD_DIM,), F32)
        if ones:
            fill = fill.at[..., 0].set(1.0)
        return jnp.concatenate([t, fill], axis=-1).reshape(t.shape[0], t.shape[1], w).astype(BF16)

    def bias_map(i, g):
        return (jnp.where(g == 0, 0, jnp.where(g == ngroups - 1, 2, 1)), 0, 0, 0)

    return pl.pallas_call(
        functools.partial(_na_kernel, rows=rows),
        grid=(b, ngroups),
        in_specs=[pl.BlockSpec((1, nq, w), lambda i, g: (i, g, 0)),
                  pl.BlockSpec((1, n, w), lambda i, g: (i, 0, 0)),
                  pl.BlockSpec((1, n, w), lambda i, g: (i, 0, 0)),
                  pl.BlockSpec((1, l, w), lambda i, g: (i, 0, 0)),
                  pl.BlockSpec((1, l, w), lambda i, g: (i, 0, 0)),
                  pl.BlockSpec((1, NA_HEADS, nq, NA_KEY_ROWS * GRID_W), bias_map)],
        out_specs=pl.BlockSpec((1, nq, NA_W), lambda i, g: (i, g, 0)),
        out_shape=jax.ShapeDtypeStruct((b, n, NA_W), BF16),
        scratch_shapes=[pltpu.VMEM((nq, NA_KEY_ROWS * GRID_W + l), F32)],
        compiler_params=_cparams(("parallel", "arbitrary")),
        name="na_attention",
    )(lanes(q, False), lanes(k, False), lanes(v, True), lanes(kc, False), lanes(vc, True), bias_tab)


DN_LEVELS = int(math.log2(DN_CHUNK))


def _dn_constants():
    c, w = DN_CHUNK, DN_W
    ii = np.arange(c)[:, None]
    jj = (np.arange(w) % c)[None, :]
    hh = (np.arange(w) // HEAD_DIM)[None, :]
    tri, expand, neg, strict, lvl = [], [], [], [], []
    for d in range(2):
        rev = d == 1
        incl = (ii <= jj) if rev else (ii >= jj)
        neg.append(np.where(incl, 0.0, NEG_INF))
        strict.append((ii < jj) if rev else (ii > jj))
        t = np.arange(c)
        tri.append((t[:, None] <= t[None, :]) if rev else (t[:, None] >= t[None, :]))
        expand.append(np.arange(V7X_LANES)[:, None] == hh + d * DN_HEADS)
        for lv in range(DN_LEVELS):
            same = (ii >> (lv + 1)) == (jj >> (lv + 1))
            hi_i = ((ii >> lv) & 1) == 1
            hi_j = ((jj >> lv) & 1) == 1
            lvl.append((same & ~hi_i & hi_j) if rev else (same & hi_i & ~hi_j))
    r2 = np.arange(w)
    block = (r2[:, None] // HEAD_DIM) == (r2[None, :] // HEAD_DIM)
    f = lambda a, dt: jnp.asarray(np.stack(a).astype(np.float32), dtype=dt)
    return dict(tri=f(tri, F32), expand=f(expand, F32), neg=f(neg, F32), strict=f(strict, F32),
                lvl=f(lvl, BF16).reshape(2, DN_LEVELS, c, w), eye=jnp.asarray((ii == jj).astype(np.float32)),
                block16=jnp.asarray(block.astype(np.float32), dtype=BF16), block32=jnp.asarray(block.astype(np.float32)))


def _dn_block_diag(y16, block16):
    return jnp.concatenate([y16] * DN_HEADS, axis=0) * block16


def _dn_prep_kernel(q_ref, k_ref, v_ref, g_ref, b_ref, tri_ref, exp_ref, neg_ref, strict_ref, lvl_ref, eye_ref,
                    blk_ref, *out_refs, chunks):
    c = DN_CHUNK
    nt = (((1,), (1,)), ((), ()))

    def bdmm(x, y):
        return jnp.dot(x.astype(BF16), _dn_block_diag(y.astype(BF16), blk_ref[...]), preferred_element_type=F32)

    def chunk_group(gi, carry):
        probs = []
        for sub in range(DN_PREP_INTERLEAVE):
            ci = gi * DN_PREP_INTERLEAVE + sub
            rows = pl.ds(pl.multiple_of(ci * c, c), c)
            q = q_ref[0, rows, :] * (HEAD_DIM ** -0.5)
            k = k_ref[0, rows, :]
            v = v_ref[0, rows, :]
            gcol = g_ref[0, rows, :]
            bcol = b_ref[0, rows, :]
            kbd = _dn_block_diag(k.astype(BF16), blk_ref[...])
            kk = lax.dot_general(k.astype(BF16), kbd, nt, preferred_element_type=F32)
            qk = lax.dot_general(q.astype(BF16), kbd, nt, preferred_element_type=F32)
            for d in range(2):
                gc = jnp.dot(tri_ref[d], gcol, preferred_element_type=F32, precision=HIGHEST)
                gcx = jnp.dot(gc, exp_ref[d], preferred_element_type=F32, precision=HIGHEST)
                bx = jnp.dot(bcol, exp_ref[d], preferred_element_type=F32, precision=HIGHEST)
                gr = jnp.sum(gcx * eye_ref[...], axis=0, keepdims=True)
                decay = jnp.exp(gcx - gr + neg_ref[d])
                a16 = (kk * bx * decay * strict_ref[d]).astype(BF16)
                probs.append(dict(d=d, ci=ci, rows=rows, q=q, k=k, v=v, qk=qk, gcx=gcx, bx=bx, decay=decay, a16=a16,
                                  t=eye_ref[...]))
        for lv in range(DN_LEVELS):
            for p in probs:
                p["x"] = jnp.dot(p["t"].astype(BF16), _dn_block_diag(p["a16"] * lvl_ref[p["d"], lv], blk_ref[...]),
                                 preferred_element_type=F32)
            for p in probs:
                p["t"] = p["t"] - bdmm(p["x"], p["t"])
        for p in probs:
            d, rows, gcx = p["d"], p["rows"], p["gcx"]
            u_ref, w_ref, a_ref, qg_ref, kd_ref, eg_ref = out_refs[6 * d:6 * d + 6]
            last = 0 if d == 1 else c - 1
            e_gc = jnp.exp(gcx)
            g_last = gcx[last:last + 1, :]
            u_ref[0, rows, :] = bdmm(p["t"], p["v"] * p["bx"]).astype(u_ref.dtype)
            w_ref[0, rows, :] = bdmm(p["t"], p["k"] * p["bx"] * e_gc).astype(w_ref.dtype)
            a_ref[0, rows, :] = (p["qk"] * p["decay"]).astype(a_ref.dtype)
            qg_ref[0, rows, :] = (p["q"] * e_gc).astype(qg_ref.dtype)
            kd_ref[0, rows, :] = (p["k"] * jnp.exp(g_last - gcx)).astype(kd_ref.dtype)
            eg_ref[0, p["ci"]] = jnp.exp(g_last)
        return carry

    lax.fori_loop(0, chunks // DN_PREP_INTERLEAVE, chunk_group, 0)


def _dn_scan_kernel(*refs, chunks):
    ins = (refs[0:6], refs[6:12])
    s0_refs = refs[12:14]
    b16_ref, b32_ref = refs[14:16]
    o_refs = refs[16:18]
    sf_refs = refs[18:20]
    s_scr = refs[20]
    c = DN_CHUNK
    blk = pl.program_id(1)

    @pl.when(blk == 0)
    def _():
        s_scr[0] = s0_refs[0][0]
        s_scr[1] = s0_refs[1][0]

    def chunk(ci, carry):
        st = []
        for d in range(2):
            idx = (chunks - 1 - ci) if d == 1 else ci
            st.append(dict(idx=idx, rows=pl.ds(pl.multiple_of(idx * c, c), c), s=s_scr[d]))
        for d, p in enumerate(st):
            u_ref, w_ref, a_ref, qg_ref, kd_ref, eg_ref = ins[d]
            wq = jnp.concatenate([w_ref[0, p["rows"], :], qg_ref[0, p["rows"], :]], axis=0)
            p["r"] = jnp.dot(wq, p["s"].astype(BF16), preferred_element_type=F32)
        for d, p in enumerate(st):
            u_ref = ins[d][0]
            p["vn"] = (u_ref[0, p["rows"], :].astype(F32) - p["r"][:c]).astype(BF16)
        for d, p in enumerate(st):
            kd_ref, eg_ref = ins[d][4], ins[d][5]
            kv = lax.dot_general(kd_ref[0, p["rows"], :], p["vn"], (((0,), (0,)), ((), ())), preferred_element_type=F32)
            s_scr[d] = p["s"] * eg_ref[0, p["idx"]] + kv * b32_ref[...]
        for d, p in enumerate(st):
            a_ref = ins[d][2]
            o_refs[d][0, p["rows"], :] = p["r"][c:] + jnp.dot(a_ref[0, p["rows"], :],
                                                              _dn_block_diag(p["vn"], b16_ref[...]),
                                                              preferred_element_type=F32)
        return carry

    lax.fori_loop(0, chunks, chunk, 0)

    @pl.when(blk == pl.num_programs(1) - 1)
    def _():
        sf_refs[0][0] = s_scr[0]
        sf_refs[1][0] = s_scr[1]


def _delta_bidir(q, k, v, g, beta, s0_f, s0_b):
    b, n, w = q.shape
    nchunks = n // DN_CHUNK
    chunks = min(DN_BLOCK_CHUNKS, nchunks)
    nblk = nchunks // chunks
    bt = chunks * DN_CHUNK

    tok = pl.BlockSpec((1, bt, w), lambda i, j: (i, j, 0))
    small = pl.BlockSpec((1, bt, V7X_LANES), lambda i, j: (i, j, 0))
    egl = pl.BlockSpec((1, chunks, 1, w), lambda i, j: (i, j, 0, 0))
    per_dir_shapes = [jax.ShapeDtypeStruct((b, n, w), BF16)] * 5 + [jax.ShapeDtypeStruct((b, nchunks, 1, w), F32)]
    cst = _dn_constants()

    def const_spec(a):
        nd = a.ndim
        return pl.BlockSpec(a.shape, lambda i, j: (0,) * nd)

    prep_consts = [cst[name] for name in ("tri", "expand", "neg", "strict", "lvl", "eye", "block16")]
    prep = pl.pallas_call(
        functools.partial(_dn_prep_kernel, chunks=chunks),
        grid=(b, nblk),
        in_specs=[tok, tok, tok, small, small] + [const_spec(a) for a in prep_consts],
        out_specs=([tok] * 5 + [egl]) * 2,
        out_shape=per_dir_shapes * 2,
        compiler_params=_cparams(("parallel", "parallel")),
        name="delta_prep",
    )(q, k, v, g, beta, *prep_consts)

    tok_r = pl.BlockSpec((1, bt, w), lambda i, j: (i, nblk - 1 - j, 0))
    egl_r = pl.BlockSpec((1, chunks, 1, w), lambda i, j: (i, nblk - 1 - j, 0, 0))
    state = pl.BlockSpec((1, w, w), lambda i, j: (i, 0, 0))
    scan_consts = [cst["block16"], cst["block32"]]
    o_f, o_b, s_f, s_b = pl.pallas_call(
        functools.partial(_dn_scan_kernel, chunks=chunks),
        grid=(b, nblk),
        in_specs=[tok] * 5 + [egl] + [tok_r] * 5 + [egl_r] + [state, state] + [const_spec(a) for a in scan_consts],
        out_specs=[tok, tok_r, state, state],
        out_shape=[jax.ShapeDtypeStruct((b, n, w), F32)] * 2 + [jax.ShapeDtypeStruct((b, w, w), F32)] * 2,
        scratch_shapes=[pltpu.VMEM((2, w, w), F32)],
        compiler_params=_cparams(("parallel", "arbitrary")),
        name="delta_scan",
    )(*prep, s0_f, s0_b, *scan_consts)
    return o_f + o_b, s_f, s_b


def _out_proj_kernel(x_ref, ya_ref, yb_ref, yc_ref, yd_ref, w_ref, gate_ref, g2_ref, sh_ref, sc_ref, rw_ref, rb_ref,
                     xo_ref, h2_ref, idx_ref, gt_ref):
    acc = None
    for i, y_ref in enumerate((ya_ref, yb_ref, yc_ref, yd_ref)):
        wd = y_ref.shape[-1]
        part = jnp.dot(y_ref[0], w_ref[i * wd:(i + 1) * wd, :], preferred_element_type=F32)
        acc = part if acc is None else acc + part
    xn = x_ref[0] + gate_ref[0] * acc
    xo_ref[0] = xn
    h2 = _rms(xn, g2_ref[...]) * (1.0 + sc_ref[0]) + sh_ref[0]
    h2_ref[0] = h2
    logits = jnp.dot(h2, rw_ref[...], preferred_element_type=F32, precision=HIGHEST) + rb_ref[...]
    lane = lax.broadcasted_iota(jnp.int32, logits.shape, 1).astype(F32)
    vals, idxs = [], []
    cur = logits
    for _ in range(TOP_K):
        mx = jnp.max(cur, axis=-1, keepdims=True)
        ik = jnp.min(jnp.where(cur == mx, lane, float(V7X_LANES)), axis=-1, keepdims=True)
        vals.append(mx)
        idxs.append(ik)
        cur = jnp.where(lane == ik, -jnp.inf, cur)
    es = [jnp.exp(vv - vals[0]) for vv in vals]
    den = es[0] + es[1] + es[2] + es[3]
    idx_out = jnp.zeros(logits.shape, F32)
    gate_out = jnp.zeros(logits.shape, F32)
    for kk in range(TOP_K):
        idx_out = jnp.where(lane == float(kk), idxs[kk], idx_out)
        gate_out = jnp.where(lane == float(kk), es[kk] / den, gate_out)
    idx_ref[0] = idx_out.astype(jnp.int32)
    gt_ref[0] = gate_out


def _out_proj(x, ys, w_out, gate, g2, shift, scale, rw_pad, rb_pad, tm):
    b, n, d = x.shape
    yw = ys[0].shape[-1]
    vec = pl.BlockSpec((1, 1, d), lambda i, j: (i, 0, 0))
    tok = pl.BlockSpec((1, tm, d), lambda i, j: (i, j, 0))
    ytok = pl.BlockSpec((1, tm, yw), lambda i, j: (i, j, 0))
    ltok = pl.BlockSpec((1, tm, V7X_LANES), lambda i, j: (i, j, 0))
    return pl.pallas_call(
        _out_proj_kernel,
        grid=(b, n // tm),
        in_specs=[tok, ytok, ytok, ytok, ytok,
                  pl.BlockSpec((4 * yw, d), lambda i, j: (0, 0)),
                  vec,
                  pl.BlockSpec((1, d), lambda i, j: (0, 0)),
                  vec, vec,
                  pl.BlockSpec((d, V7X_LANES), lambda i, j: (0, 0)),
                  pl.BlockSpec((1, V7X_LANES), lambda i, j: (0, 0))],
        out_specs=[tok, tok, ltok, ltok],
        out_shape=[jax.ShapeDtypeStruct((b, n, d), F32), jax.ShapeDtypeStruct((b, n, d), F32),
                   jax.ShapeDtypeStruct((b, n, V7X_LANES), jnp.int32), jax.ShapeDtypeStruct((b, n, V7X_LANES), F32)],
        compiler_params=_cparams(("parallel", "parallel")),
        name="out_proj",
    )(x, ys[0], ys[1], ys[2], ys[3], w_out, gate.reshape(b, 1, d), g2.reshape(1, d),
      shift.reshape(b, 1, d), scale.reshape(b, 1, d), rw_pad, rb_pad)


def _experts_kernel(te_ref, tf_ref, tv_ref, src0_ref, srcn_ref, h_hbm, w1_ref, b1_ref, w2_ref, b2_ref, o_ref,
                    xbuf, w1b, w2b, sem, *, fc):
    i = pl.program_id(0)
    nt = pl.num_programs(0)
    tm = o_ref.shape[0]
    f = w2_ref.shape[1]
    slot = i % 2

    def row_copy(src_ref, r, dst_slot):
        return pltpu.make_async_copy(h_hbm.at[pl.ds(src_ref[0, 0, r], 1)], xbuf.at[dst_slot, pl.ds(r, 1)],
                                     sem.at[dst_slot])

    def issue(src_ref, dst_slot):
        def body(r8, carry):
            for u in range(MOE_ISSUE_UNROLL):
                row_copy(src_ref, r8 * MOE_ISSUE_UNROLL + u, dst_slot).start()
            return carry
        lax.fori_loop(0, tm // MOE_ISSUE_UNROLL, body, 0)

    def wait_tile(s):
        pltpu.make_async_copy(h_hbm.at[pl.ds(0, tm)], xbuf.at[s], sem.at[s]).wait()

    @pl.when(jnp.logical_and(i == 0, tv_ref[0] != 0))
    def _():
        issue(src0_ref, 0)

    @pl.when(tv_ref[i] == 0)
    def _():
        @pl.when(jnp.logical_and(i > 0, tv_ref[jnp.maximum(i - 1, 0)] != 0))
        def _():
            wait_tile(slot)
        o_ref[...] = jnp.zeros(o_ref.shape, o_ref.dtype)

    @pl.when(tv_ref[i] != 0)
    def _():
        issue(srcn_ref, 1 - slot)

        @pl.when(tf_ref[i] != 0)
        def _():
            w1b[...] = w1_ref[0].astype(BF16)
            w2b[...] = w2_ref[0].astype(BF16)

        wait_tile(slot)
        xb = xbuf[slot].astype(BF16)
        acc = None
        for j in range(f // fc):
            glu = jnp.dot(xb, w1b[:, j * fc:(j + 1) * fc], preferred_element_type=F32) + b1_ref[0, :, j * fc:(j + 1) * fc]
            lin = (jnp.dot(xb, w1b[:, f + j * fc:f + (j + 1) * fc], preferred_element_type=F32)
                   + b1_ref[0, :, f + j * fc:f + (j + 1) * fc])
            glu = jnp.minimum(glu, SWIGLU_LIMIT)
            lin = jnp.clip(lin, -SWIGLU_LIMIT, SWIGLU_LIMIT)
            act = glu * jax.nn.sigmoid(SWIGLU_ALPHA * glu) * (lin + 1.0)
            part = jnp.dot(act.astype(BF16), w2b[j * fc:(j + 1) * fc, :], preferred_element_type=F32)
            acc = part if acc is None else acc + part
        o_ref[...] = acc + b2_ref[0]

        @pl.when(i == nt - 1)
        def _():
            wait_tile(1 - slot)


def _expert_tiles(h, src, tile_e, tile_first, tile_valid, w1, b1, w2, b2, tm):
    t, d = h.shape
    r = src.shape[0]
    e, _, f2 = w1.shape
    f = f2 // 2
    nt = r // tm
    grid_spec = pltpu.PrefetchScalarGridSpec(
        num_scalar_prefetch=3,
        grid=(nt,),
        in_specs=[pl.BlockSpec((1, 1, tm), lambda i, te, tf, tv: (0, 0, 0), memory_space=pltpu.SMEM),
                  pl.BlockSpec((1, 1, tm), lambda i, te, tf, tv: (i + 1, 0, 0), memory_space=pltpu.SMEM),
                  pl.BlockSpec(memory_space=pl.ANY),
                  pl.BlockSpec((1, d, f2), lambda i, te, tf, tv: (te[i], 0, 0)),
                  pl.BlockSpec((1, 1, f2), lambda i, te, tf, tv: (te[i], 0, 0)),
                  pl.BlockSpec((1, f, d), lambda i, te, tf, tv: (te[i], 0, 0)),
                  pl.BlockSpec((1, 1, d), lambda i, te, tf, tv: (te[i], 0, 0))],
        out_specs=pl.BlockSpec((tm, d), lambda i, te, tf, tv: (i, 0)),
        scratch_shapes=[pltpu.VMEM((2, tm, d), F32), pltpu.VMEM((d, f2), BF16), pltpu.VMEM((f, d), BF16),
                        pltpu.SemaphoreType.DMA((2,))],
    )
    src3 = jnp.concatenate([src, jnp.zeros((tm,), src.dtype)]).reshape(nt + 1, 1, tm)
    return pl.pallas_call(
        functools.partial(_experts_kernel, fc=MOE_FC),
        grid_spec=grid_spec,
        out_shape=jax.ShapeDtypeStruct((r, d), F32),
        compiler_params=_cparams(("arbitrary",)),
        name="moe_experts",
    )(tile_e, tile_first, tile_valid, src3, src3, h, w1, b1.reshape(e, 1, f2), w2, b2.reshape(e, 1, d))


def _combine_kernel(pos_ref, ys_hbm, gate_ref, x_ref, mod_ref, o_ref, buf, sem):
    ct = o_ref.shape[0]

    def issue(r2, carry):
        for u in range(COMBINE_ISSUE_UNROLL):
            r = r2 * COMBINE_ISSUE_UNROLL + u
            for kk in range(TOP_K):
                p = pos_ref[0, 0, r * TOP_K + kk]
                pltpu.make_async_copy(ys_hbm.at[pl.ds(p, 1)], buf.at[kk, pl.ds(r, 1)], sem).start()
        return carry

    lax.fori_loop(0, ct // COMBINE_ISSUE_UNROLL, issue, 0)
    for kk in range(TOP_K):
        pltpu.make_async_copy(ys_hbm.at[pl.ds(0, ct)], buf.at[kk], sem).wait()
    g = gate_ref[...]
    mix = (g[:, 0:1] * buf[0] + g[:, 1:2] * buf[1]) + (g[:, 2:3] * buf[2] + g[:, 3:4] * buf[3])
    o_ref[...] = x_ref[...] + mod_ref[0] * mix


def _combine_rows(ys, pos, gates, x, mod, ct):
    b, n, d = x.shape
    t = b * n
    ct = min(ct, n)
    nsteps = t // ct
    per_batch = n // ct
    out = pl.pallas_call(
        _combine_kernel,
        grid=(nsteps,),
        in_specs=[pl.BlockSpec((1, 1, ct * TOP_K), lambda i: (i, 0, 0), memory_space=pltpu.SMEM),
                  pl.BlockSpec(memory_space=pl.ANY),
                  pl.BlockSpec((ct, V7X_LANES), lambda i: (i, 0)),
                  pl.BlockSpec((ct, d), lambda i: (i, 0)),
                  pl.BlockSpec((1, 1, d), lambda i: (i // per_batch, 0, 0))],
        out_specs=pl.BlockSpec((ct, d), lambda i: (i, 0)),
        out_shape=jax.ShapeDtypeStruct((t, d), F32),
        scratch_shapes=[pltpu.VMEM((TOP_K, ct, d), F32), pltpu.SemaphoreType.DMA],
        compiler_params=_cparams(("arbitrary",)),
        name="moe_combine",
    )(pos.reshape(nsteps, 1, ct * TOP_K), ys, gates, x.reshape(t, d), mod.reshape(b, 1, d))
    return out.reshape(b, n, d)


def _route_plan(idx4, tm):
    t = idx4.shape[0]
    e = N_EXPERTS
    r_max = t * TOP_K + e * tm
    nt = r_max // tm
    onehot = (idx4[:, :, None] == jnp.arange(e, dtype=jnp.int32)[None, None, :]).astype(jnp.int32)
    member = jnp.sum(onehot, axis=1)
    csum = jnp.cumsum(member, axis=0)
    cnt = csum[-1]
    excl = csum - member
    cnt_pad = ((cnt + tm - 1) // tm) * tm
    ends = jnp.cumsum(cnt_pad)
    base = ends - cnt_pad
    pos = jnp.sum(onehot * (excl + base[None, :])[:, None, :], axis=-1)
    flat = pos.reshape(-1)
    tok = jnp.repeat(jnp.arange(t, dtype=jnp.int32), TOP_K)
    src = jnp.zeros((r_max,), jnp.int32).at[flat].set(tok, unique_indices=True)
    tile_start = jnp.arange(nt, dtype=jnp.int32) * tm
    tile_e = jnp.minimum(jnp.sum((ends[None, :] <= tile_start[:, None]).astype(jnp.int32), axis=1), e - 1)
    tile_valid = (tile_start < ends[-1]).astype(jnp.int32)
    tile_first = jnp.concatenate([jnp.ones((1,), jnp.int32), (tile_e[1:] != tile_e[:-1]).astype(jnp.int32)])
    return pos.astype(jnp.int32), src, tile_e, tile_first, tile_valid


def _moe_residual(h2, idx4, gates, w1, b1, w2, b2, streams):
    pos, src, tile_e, tile_first, tile_valid = _route_plan(idx4, MOE_TM)
    ys = _expert_tiles(h2, src, tile_e, tile_first, tile_valid, w1, b1, w2, b2, MOE_TM)
    outs, start = [], 0
    for x, mod in streams:
        cnt = x.shape[0] * x.shape[1]
        outs.append(_combine_rows(ys, pos[start:start + cnt], gates[start:start + cnt], x, mod, COMBINE_ROWS))
        start += cnt
    return outs


def _axial_rope_tables(n_tok, rot_dim):
    t = jnp.arange(n_tok, dtype=jnp.int32)
    row = (t // GRID_W).astype(F32)
    col = (t % GRID_W).astype(F32)
    n_freq = rot_dim // 4
    freqs = ROPE_THETA ** (-jnp.arange(n_freq, dtype=F32) / n_freq)
    ang = jnp.concatenate([row[:, None] * freqs, col[:, None] * freqs], axis=-1)
    return jnp.cos(ang), jnp.sin(ang)


def _apply_rope(x, cos, sin):
    xf = x.reshape(*x.shape[:-1], -1, 2)
    x1, x2 = xf[..., 0], xf[..., 1]
    cs, sn = cos[:, None, :], sin[:, None, :]
    out = jnp.stack([x1 * cs - x2 * sn, x1 * sn + x2 * cs], axis=-1)
    return out.reshape(x.shape)


def _heads(t, n):
    return t.reshape(t.shape[0], t.shape[1], n, t.shape[2] // n)


def _head_major(t):
    return jnp.transpose(t, (0, 2, 1, 3))


def _permute_w_in(w_in):
    offs = (0,) + IN_OFFSETS
    seg = {i: (offs[i], offs[i] + IN_SIZES[i]) for i in range(len(IN_SIZES))}
    order = [0, 1, 2, 3, 4, 5, 6, 7, 8, 9, 12, 13, 14, 10, 11]
    cols = np.concatenate([np.arange(*seg[i]) for i in order])
    w = jnp.take(w_in, jnp.asarray(cols), axis=1)
    pad = sum(PROJ_GROUPS) - w.shape[1]
    return jnp.pad(w, ((0, 0), (0, pad))).astype(BF16)


def _dn_prep(qkv, small, conv_w, a_log, dt_bias):
    b, n, _ = qkv.shape
    pad = DN_CONV // 2
    xp = jnp.pad(qkv, ((0, 0), (pad, pad), (0, 0)))
    conv = sum(xp[:, i:i + n, :] * conv_w[i][None, None, :] for i in range(DN_CONV))
    act = jax.nn.silu(conv)
    q, k, v = jnp.split(act, 3, axis=-1)

    def l2(t):
        th = _heads(t, DN_HEADS)
        return (th * lax.rsqrt(jnp.sum(th * th, axis=-1, keepdims=True) + EPS)).reshape(b, n, DN_W)

    beta_raw = small[..., MLA_ROPE:MLA_ROPE + 2 * DN_HEADS]
    a_raw = small[..., MLA_ROPE + 2 * DN_HEADS:MLA_ROPE + 4 * DN_HEADS]
    beta = jax.nn.sigmoid(beta_raw)
    g = -jnp.exp(a_log.reshape(-1)) * jax.nn.softplus(a_raw + dt_bias.reshape(-1))
    lane_pad = ((0, 0), (0, 0), (0, V7X_LANES - 2 * DN_HEADS))
    return l2(q), l2(k), v, jnp.pad(g, lane_pad), jnp.pad(beta, lane_pad)


def _mla_kv(ckv, kpe, ckv_g, w_ukv, kn_g, rope):
    b, n, _ = ckv.shape
    kv = _norm_mm(ckv.reshape(b * n, MLA_KV_RANK), ckv_g, w_ukv, min(1024, b * n)).reshape(b, n, MLA_HEADS, MLA_NOPE + MLA_V)
    k_pe = jnp.broadcast_to(kpe[:, :, None, :], (b, n, MLA_HEADS, MLA_ROPE))
    k = _rms(jnp.concatenate([kv[..., :MLA_NOPE], k_pe], axis=-1), kn_g)
    if rope is not None:
        k = jnp.concatenate([k[..., :MLA_NOPE], _apply_rope(k[..., MLA_NOPE:], *rope)], axis=-1)
    return k, kv[..., MLA_NOPE:]


def _mla_q(cq, cq_g, w_uq, qn_g, rope):
    b, n, _ = cq.shape
    q = _norm_mm(cq.reshape(b * n, MLA_Q_RANK), cq_g, w_uq, min(1024, b * n)).reshape(b, n, MLA_HEADS, MLA_QK)
    q = _rms(q, qn_g)
    if rope is not None:
        q = jnp.concatenate([q[..., :MLA_NOPE], _apply_rope(q[..., MLA_NOPE:], *rope)], axis=-1)
    return q


def _attn_bnhd(q, k, v, groups):
    b, nq, hq, dk = q.shape
    hkv = k.shape[2]
    dv = v.shape[-1]
    qh = _head_major(q * (dk ** -0.5)).astype(BF16).reshape(b, hkv, groups, nq, dk)
    ones_col = jnp.concatenate([jnp.ones(v.shape[:-1] + (1,), F32), jnp.zeros(v.shape[:-1] + (V7X_LANES - dv - 1,), F32)], axis=-1)
    v1 = _head_major(jnp.concatenate([v, ones_col], axis=-1)).astype(BF16)
    o = _attention(qh, _head_major(k).astype(BF16), v1, dv)
    return jnp.transpose(o.reshape(b, hq, nq, dv), (0, 2, 1, 3)).reshape(b, nq, hq * dv)


def kernel(x, c, ctx, c_ctx, ada_w, ada_b, norm1_g, norm2_g, w_in, w_out, na_qn_g, na_kn_g, na_rel_bias, gqa_qn_g, gqa_kn_g, dn_conv_w, dn_a_log, dn_dt_bias, dn_out_g, mla_cq_g, mla_ckv_g, mla_w_uq, mla_w_ukv, mla_qn_g, mla_kn_g, router_w, router_b, exp_w1, exp_b1, exp_w2, exp_b2):
    b, n, d = x.shape
    l = ctx.shape[1]
    depth = ada_w.shape[0]
    grp = GQA_Q_HEADS // GQA_KV_HEADS
    rope_g = _axial_rope_tables(n, HEAD_DIM)
    rope_m = _axial_rope_tables(n, MLA_ROPE)
    cond = jnp.concatenate([jax.nn.silu(c), jax.nn.silu(c_ctx)[None], jnp.zeros((16 - b - 1, d), F32)], axis=0)
    s0 = jnp.zeros((b, DN_W, DN_W), F32)
    scale = HEAD_DIM ** -0.5

    for ly in range(depth):
        with_ctx = ly < depth - 1
        mod_all = _mm_bias(cond, ada_w[ly], ada_b[ly], 1024)
        mod = jnp.split(mod_all[:b], N_ADA, axis=-1)
        mod_c = [jnp.broadcast_to(m_, (b, d)) for m_ in jnp.split(mod_all[b:b + 1], N_ADA, axis=-1)]
        w_perm = _permute_w_in(w_in[ly])
        p = _in_proj(x, norm1_g[ly], mod[0], mod[1], w_perm, PROJ_TM)
        pc = _in_proj(ctx, norm1_g[ly], mod_c[0], mod_c[1], w_perm, l)
        bias_tab = _na_bias_table(na_rel_bias[ly], n // GRID_W)

        qa = _rms(_heads(p[0], NA_HEADS), na_qn_g[ly])
        ka = _rms(_heads(p[1], NA_HEADS), na_kn_g[ly])
        kac = _rms(_heads(pc[1], NA_HEADS), na_kn_g[ly])
        ya = _na_attention(qa * scale, ka, _heads(p[2], NA_HEADS), kac, _heads(pc[2], NA_HEADS), bias_tab)

        qb = _apply_rope(_rms(_heads(p[3], GQA_Q_HEADS), gqa_qn_g[ly]), *rope_g)
        kb = _apply_rope(_rms(_heads(p[4][..., :GQA_KVW], GQA_KV_HEADS), gqa_kn_g[ly]), *rope_g)
        vb = _heads(p[4][..., GQA_KVW:], GQA_KV_HEADS)
        kbc = _rms(_heads(pc[4][..., :GQA_KVW], GQA_KV_HEADS), gqa_kn_g[ly])
        vbc = _heads(pc[4][..., GQA_KVW:], GQA_KV_HEADS)
        yb = _attn_bnhd(qb, jnp.concatenate([kb, kbc], axis=1), jnp.concatenate([vb, vbc], axis=1), grp)

        qdc, kdc, vdc, gdc, bdc = _dn_prep(pc[5], pc[9], dn_conv_w[ly], dn_a_log[ly], dn_dt_bias[ly])
        odc, s_f, s_b = _delta_bidir(qdc, kdc, vdc, gdc, bdc, s0, s0)
        qd, kd, vd, gd, bd = _dn_prep(p[5], p[9], dn_conv_w[ly], dn_a_log[ly], dn_dt_bias[ly])
        od, _, _ = _delta_bidir(qd, kd, vd, gd, bd, s_f, s_b)
        yc = (_rms(_heads(od, DN_HEADS), dn_out_g[ly]) * jax.nn.silu(_heads(p[6], DN_HEADS))).reshape(b, n, DN_W).astype(BF16)

        qm = _mla_q(p[7], mla_cq_g[ly], mla_w_uq[ly], mla_qn_g[ly], rope_m)
        km, vm = _mla_kv(p[8], p[9][..., :MLA_ROPE], mla_ckv_g[ly], mla_w_ukv[ly], mla_kn_g[ly], rope_m)
        kmc, vmc = _mla_kv(pc[8], pc[9][..., :MLA_ROPE], mla_ckv_g[ly], mla_w_ukv[ly], mla_kn_g[ly], None)
        yd = _attn_bnhd(qm, jnp.concatenate([km, kmc], axis=1), jnp.concatenate([vm, vmc], axis=1), 1)

        rw_pad = jnp.pad(router_w[ly], ((0, 0), (0, V7X_LANES - N_EXPERTS)))
        rb_pad = jnp.pad(router_b[ly], (0, V7X_LANES - N_EXPERTS), constant_values=NEG_INF).reshape(1, V7X_LANES)
        w_out_b = w_out[ly].astype(BF16)
        x, h2, idx, gts = _out_proj(x, (ya, yb, yc, yd), w_out_b, mod[2], norm2_g[ly], mod[3], mod[4],
                                    rw_pad, rb_pad, PROJ_TM)
        tok = h2.reshape(b * n, d)
        idx4 = idx.reshape(b * n, V7X_LANES)[:, :TOP_K]
        gates = gts.reshape(b * n, V7X_LANES)

        if with_ctx:
            qac = _rms(_heads(pc[0], NA_HEADS), na_qn_g[ly])
            yac = _attn_bnhd(qac, kac, _heads(pc[2], NA_HEADS), 1)
            qbc = _rms(_heads(pc[3], GQA_Q_HEADS), gqa_qn_g[ly])
            ybc = _attn_bnhd(qbc, kbc, vbc, grp)
            ycc = (_rms(_heads(odc, DN_HEADS), dn_out_g[ly]) * jax.nn.silu(_heads(pc[6], DN_HEADS))).reshape(b, l, DN_W).astype(BF16)
            qmc = _mla_q(pc[7], mla_cq_g[ly], mla_w_uq[ly], mla_qn_g[ly], None)
            ydc = _attn_bnhd(qmc, kmc, vmc, 1)
            ctx, h2c, idxc, gtsc = _out_proj(ctx, (yac, ybc, ycc, ydc), w_out_b, mod_c[2], norm2_g[ly], mod_c[3], mod_c[4],
                                             rw_pad, rb_pad, l)
            tok = jnp.concatenate([tok, h2c.reshape(b * l, d)], axis=0)
            idx4 = jnp.concatenate([idx4, idxc.reshape(b * l, V7X_LANES)[:, :TOP_K]], axis=0)
            gates = jnp.concatenate([gates, gtsc.reshape(b * l, V7X_LANES)], axis=0)

        streams = [(x, mod[5])] + ([(ctx, mod_c[5])] if with_ctx else [])
        outs = _moe_residual(tok, idx4, gates, exp_w1[ly], exp_b1[ly], exp_w2[ly], exp_b2[ly], streams)
        x = outs[0]
        if with_ctx:
            ctx = outs[1]
    return x
```

```python
import functools
import math

import jax
import jax.numpy as jnp
import numpy as np
from jax import lax
from jax.experimental import pallas as pl
from jax.experimental.pallas import tpu as pltpu

F32 = jnp.float32
BF16 = jnp.bfloat16
HIGHEST = lax.Precision.HIGHEST

GRID_W = 64
HEAD_DIM = 64
NA_HEADS = 4
NA_WIN_R = 8
NA_WIN_C = 16
GQA_Q_HEADS = 4
GQA_KV_HEADS = 2
DN_HEADS = 4
DN_CONV = 5
DN_CHUNK = 64
MLA_HEADS = 4
MLA_Q_RANK = 256
MLA_KV_RANK = 128
MLA_NOPE = 64
MLA_ROPE = 32
MLA_V = 64
MLA_QK = MLA_NOPE + MLA_ROPE
N_EXPERTS = 32
TOP_K = 4
SWIGLU_LIMIT = 7.0
SWIGLU_ALPHA = 1.702
ROPE_THETA = 10000.0
EPS = 1e-6
NEG_INF = -1e30
N_ADA = 6

NA_W = NA_HEADS * HEAD_DIM
GQA_QW = GQA_Q_HEADS * HEAD_DIM
GQA_KVW = GQA_KV_HEADS * HEAD_DIM
DN_W = DN_HEADS * HEAD_DIM
MLA_W = MLA_HEADS * MLA_V
IN_SIZES = (NA_W, NA_W, NA_W, GQA_QW, GQA_KVW, GQA_KVW, DN_W, DN_W, DN_W, DN_W, 2 * DN_HEADS, 2 * DN_HEADS,
            MLA_Q_RANK, MLA_KV_RANK, MLA_ROPE)
IN_OFFSETS = tuple(int(o) for o in np.cumsum(IN_SIZES)[:-1])

V7X_VMEM_BYTES = 64 * 1024 * 1024
V7X_LANES = 128
VMEM_LIMIT = V7X_VMEM_BYTES - 8 * 1024 * 1024

PROJ_TM = 512
ATTN_ROWS = 256
ATTN_TK = 256
NA_GROUP = 8
NA_KEY_ROWS = 16
DN_BLOCK_CHUNKS = 8
DN_PREP_INTERLEAVE = 2
MOE_TM = 512
MOE_FC = 512
MOE_ISSUE_UNROLL = 8
COMBINE_ROWS = 256
COMBINE_ISSUE_UNROLL = 2

PROJ_GROUPS = (NA_HEADS * V7X_LANES, NA_HEADS * V7X_LANES, NA_HEADS * V7X_LANES, GQA_Q_HEADS * V7X_LANES,
               GQA_KV_HEADS * V7X_LANES, GQA_KV_HEADS * V7X_LANES, 3 * DN_W, DN_W, MLA_Q_RANK, MLA_KV_RANK, V7X_LANES)
SMALL_USED = MLA_ROPE + 4 * DN_HEADS


def _cparams(semantics):
    return pltpu.CompilerParams(dimension_semantics=semantics, vmem_limit_bytes=VMEM_LIMIT)


def _rms(x, g):
    return x * lax.rsqrt(jnp.mean(x * x, axis=-1, keepdims=True) + EPS) * g


def _mm_bias_kernel(x_ref, w_ref, b_ref, o_ref):
    o_ref[...] = jnp.dot(x_ref[...].astype(BF16), w_ref[...].astype(BF16), preferred_element_type=F32) + b_ref[...]


def _mm_bias(x, w, b, tn):
    m, k = x.shape
    n = w.shape[1]
    return pl.pallas_call(
        _mm_bias_kernel,
        grid=(n // tn,),
        in_specs=[pl.BlockSpec((m, k), lambda j: (0, 0)),
                  pl.BlockSpec((k, tn), lambda j: (0, j)),
                  pl.BlockSpec((1, tn), lambda j: (0, j))],
        out_specs=pl.BlockSpec((m, tn), lambda j: (0, j)),
        out_shape=jax.ShapeDtypeStruct((m, n), F32),
        compiler_params=_cparams(("parallel",)),
        name="ada_mm",
    )(x, w, b.reshape(1, n))


def _norm_mm_kernel(x_ref, g_ref, w_ref, o_ref):
    y = _rms(x_ref[...], g_ref[...])
    o_ref[...] = jnp.dot(y.astype(BF16), w_ref[...].astype(BF16), preferred_element_type=F32)


def _norm_mm(x, g, w, tm):
    m, k = x.shape
    n = w.shape[1]
    return pl.pallas_call(
        _norm_mm_kernel,
        grid=(m // tm,),
        in_specs=[pl.BlockSpec((tm, k), lambda i: (i, 0)),
                  pl.BlockSpec((1, k), lambda i: (0, 0)),
                  pl.BlockSpec((k, n), lambda i: (0, 0))],
        out_specs=pl.BlockSpec((tm, n), lambda i: (i, 0)),
        out_shape=jax.ShapeDtypeStruct((m, n), F32),
        compiler_params=_cparams(("parallel",)),
        name="norm_mm",
    )(x, g.reshape(1, k), w)


def _in_proj_kernel(x_ref, g_ref, sh_ref, sc_ref, w_ref, *out_refs):
    h = _rms(x_ref[0], g_ref[...]) * (1.0 + sc_ref[0]) + sh_ref[0]
    hb = h.astype(BF16)
    off = 0
    for o_ref in out_refs:
        wd = o_ref.shape[-1]
        o_ref[0] = jnp.dot(hb, w_ref[:, off:off + wd], preferred_element_type=F32)
        off += wd


def _in_proj(x, g, shift, scale, w_perm, tm):
    b, n, d = x.shape
    wtot = w_perm.shape[1]
    vec = pl.BlockSpec((1, 1, d), lambda i, j: (i, 0, 0))
    return pl.pallas_call(
        _in_proj_kernel,
        grid=(b, n // tm),
        in_specs=[pl.BlockSpec((1, tm, d), lambda i, j: (i, j, 0)),
                  pl.BlockSpec((1, d), lambda i, j: (0, 0)),
                  vec, vec,
                  pl.BlockSpec((d, wtot), lambda i, j: (0, 0))],
        out_specs=[pl.BlockSpec((1, tm, wd), lambda i, j: (i, j, 0)) for wd in PROJ_GROUPS],
        out_shape=[jax.ShapeDtypeStruct((b, n, wd), F32) for wd in PROJ_GROUPS],
        compiler_params=_cparams(("parallel", "parallel")),
        name="in_proj",
    )(x, g.reshape(1, d), shift.reshape(b, 1, d), scale.reshape(b, 1, d), w_perm)


def _attn_kernel(q_ref, k_ref, v_ref, o_ref, s_scr, *, tk, hq, hkv, dv):
    grp = hq // hkv
    tq = q_ref.shape[1]
    rows = grp * tq
    n_keys = k_ref.shape[1]
    nk = n_keys // tk
    nt = (((1,), (1,)), ((), ()))
    outs = [None] * hq
    for g in range(hkv):
        ks = slice(g * V7X_LANES, (g + 1) * V7X_LANES)
        q = jnp.concatenate([q_ref[0, :, (g * grp + r) * V7X_LANES:(g * grp + r + 1) * V7X_LANES] for r in range(grp)],
                            axis=0)
        mx = None
        for j in range(nk):
            s = lax.dot_general(q, k_ref[0, j * tk:(j + 1) * tk, ks], nt, preferred_element_type=F32)
            s_scr[:, j * tk:(j + 1) * tk] = s
            for t in range(tk // V7X_LANES):
                slab = s[:, t * V7X_LANES:(t + 1) * V7X_LANES]
                mx = slab if mx is None else jnp.maximum(mx, slab)
        mb = jnp.broadcast_to(jnp.max(mx, axis=-1, keepdims=True), (rows, V7X_LANES))
        acc = None
        for j in range(nk):
            slabs = [jnp.exp(s_scr[:, j * tk + t * V7X_LANES:j * tk + (t + 1) * V7X_LANES] - mb)
                     for t in range(tk // V7X_LANES)]
            p = jnp.concatenate(slabs, axis=-1).astype(BF16)
            part = jnp.dot(p, v_ref[0, j * tk:(j + 1) * tk, ks], preferred_element_type=F32)
            acc = part if acc is None else acc + part
        o = acc[:, :dv] / acc[:, dv:dv + 1]
        for r in range(grp):
            outs[g * grp + r] = o[r * tq:(r + 1) * tq]
    o_ref[0] = jnp.concatenate(outs, axis=-1).astype(o_ref.dtype)


def _attention(q, k, v1, hq, hkv, dv, tk=ATTN_TK):
    b, nq, _ = q.shape
    m = k.shape[1]
    grp = hq // hkv
    tq = min(ATTN_ROWS // grp, nq)
    tk = min(tk, m)
    return pl.pallas_call(
        functools.partial(_attn_kernel, tk=tk, hq=hq, hkv=hkv, dv=dv),
        grid=(b, nq // tq),
        in_specs=[pl.BlockSpec((1, tq, hq * V7X_LANES), lambda i, j: (i, j, 0)),
                  pl.BlockSpec((1, m, hkv * V7X_LANES), lambda i, j: (i, 0, 0)),
                  pl.BlockSpec((1, m, hkv * V7X_LANES), lambda i, j: (i, 0, 0))],
        out_specs=pl.BlockSpec((1, tq, hq * dv), lambda i, j: (i, j, 0)),
        out_shape=jax.ShapeDtypeStruct((b, nq, hq * dv), BF16),
        scratch_shapes=[pltpu.VMEM((grp * tq, m), F32)],
        compiler_params=_cparams(("parallel", "parallel")),
        name="attention",
    )(q, k, v1)


def _na_kernel(q_ref, k_ref, v_ref, kc_ref, vc_ref, bias_ref, o_ref, s_scr, *, rows):
    g = pl.program_id(1)
    w0 = jnp.clip(g * NA_GROUP - NA_WIN_R // 2, 0, rows - NA_KEY_ROWS)
    off = pl.multiple_of(w0 * GRID_W, GRID_W)
    nq = NA_GROUP * GRID_W
    nloc = NA_KEY_ROWS * GRID_W
    l = kc_ref.shape[1]
    tk = min(ATTN_TK, l)
    nt = (((1,), (1,)), ((), ()))
    outs = []
    for h in range(NA_HEADS):
        hs = slice(h * V7X_LANES, (h + 1) * V7X_LANES)
        qh = q_ref[0, :, hs]
        mx = None
        for j in range((nloc + l) // tk):
            if j * tk < nloc:
                kj = k_ref[0, pl.ds(off + j * tk, tk), hs]
                s = lax.dot_general(qh, kj, nt, preferred_element_type=F32) + bias_ref[0, h, :, j * tk:(j + 1) * tk]
            else:
                kj = kc_ref[0, j * tk - nloc:(j + 1) * tk - nloc, hs]
                s = lax.dot_general(qh, kj, nt, preferred_element_type=F32)
            s_scr[:, j * tk:(j + 1) * tk] = s
            for t in range(tk // V7X_LANES):
                slab = s[:, t * V7X_LANES:(t + 1) * V7X_LANES]
                mx = slab if mx is None else jnp.maximum(mx, slab)
        mb = jnp.broadcast_to(jnp.max(mx, axis=-1, keepdims=True), (nq, V7X_LANES))
        acc = None
        for j in range((nloc + l) // tk):
            slabs = [jnp.exp(s_scr[:, j * tk + t * V7X_LANES:j * tk + (t + 1) * V7X_LANES] - mb)
                     for t in range(tk // V7X_LANES)]
            p = jnp.concatenate(slabs, axis=-1).astype(BF16)
            if j * tk < nloc:
                vj = v_ref[0, pl.ds(off + j * tk, tk), hs]
            else:
                vj = vc_ref[0, j * tk - nloc:(j + 1) * tk - nloc, hs]
            part = jnp.dot(p, vj, preferred_element_type=F32)
            acc = part if acc is None else acc + part
        outs.append(acc[:, :HEAD_DIM] / acc[:, HEAD_DIM:HEAD_DIM + 1])
    o_ref[0] = jnp.concatenate(outs, axis=-1).astype(o_ref.dtype)


def _na_bias_table(rel_bias, rows):
    kr = NA_WIN_R
    cpos = np.arange(GRID_W)
    c0 = np.clip(cpos - NA_WIN_C // 2, 0, GRID_W - NA_WIN_C)
    col_ok = (cpos[None, :] >= c0[:, None]) & (cpos[None, :] < c0[:, None] + NA_WIN_C)
    dc = np.clip(cpos[None, :] - cpos[:, None], -(NA_WIN_C - 1), NA_WIN_C - 1) + (NA_WIN_C - 1)
    n_dr, n_dc = rel_bias.shape[1:]
    select = np.zeros((n_dc, GRID_W * GRID_W), np.float32)
    select[dc.ravel(), np.arange(GRID_W * GRID_W)] = 1.0
    base = jnp.dot(rel_bias.reshape(NA_HEADS * n_dr, n_dc), jnp.asarray(select), precision=HIGHEST)
    base = jnp.where(col_ok[None, None], base.reshape(NA_HEADS, n_dr, GRID_W, GRID_W), NEG_INF)
    ngroups = rows // NA_GROUP
    pick = np.zeros((3, NA_GROUP, NA_KEY_ROWS, n_dr + 1), np.float32)
    for var, g in enumerate((0, min(1, ngroups - 1), ngroups - 1)):
        w0 = int(np.clip(g * NA_GROUP - kr // 2, 0, rows - NA_KEY_ROWS))
        for qr in range(NA_GROUP):
            r = g * NA_GROUP + qr
            r0 = int(np.clip(r - kr // 2, 0, rows - kr))
            for kl in range(NA_KEY_ROWS):
                key_row = w0 + kl
                pick[var, qr, kl, key_row - r + kr - 1 if r0 <= key_row < r0 + kr else n_dr] = 1.0
    base_ext = jnp.concatenate([jnp.transpose(base, (1, 0, 2, 3)).reshape(n_dr, -1),
                                jnp.full((1, NA_HEADS * GRID_W * GRID_W), NEG_INF, F32)], axis=0)
    tab = jnp.dot(jnp.asarray(pick.reshape(-1, n_dr + 1)), base_ext, precision=HIGHEST)
    tab = tab.reshape(3, NA_GROUP, NA_KEY_ROWS, NA_HEADS, GRID_W, GRID_W)
    return jnp.transpose(tab, (0, 3, 1, 4, 2, 5)).reshape(3, NA_HEADS, NA_GROUP * GRID_W, NA_KEY_ROWS * GRID_W)


def _na_attention(q, k, v, kc, vc, bias_tab):
    b, n, w = q.shape
    l = kc.shape[1]
    rows = n // GRID_W
    assert rows % NA_GROUP == 0 and rows >= NA_KEY_ROWS
    ngroups = rows // NA_GROUP
    nq = NA_GROUP * GRID_W

    def bias_map(i, g):
        return (jnp.where(g == 0, 0, jnp.where(g == ngroups - 1, 2, 1)), 0, 0, 0)

    return pl.pallas_call(
        functools.partial(_na_kernel, rows=rows),
        grid=(b, ngroups),
        in_specs=[pl.BlockSpec((1, nq, w), lambda i, g: (i, g, 0)),
                  pl.BlockSpec((1, n, w), lambda i, g: (i, 0, 0)),
                  pl.BlockSpec((1, n, w), lambda i, g: (i, 0, 0)),
                  pl.BlockSpec((1, l, w), lambda i, g: (i, 0, 0)),
                  pl.BlockSpec((1, l, w), lambda i, g: (i, 0, 0)),
                  pl.BlockSpec((1, NA_HEADS, nq, NA_KEY_ROWS * GRID_W), bias_map)],
        out_specs=pl.BlockSpec((1, nq, NA_W), lambda i, g: (i, g, 0)),
        out_shape=jax.ShapeDtypeStruct((b, n, NA_W), BF16),
        scratch_shapes=[pltpu.VMEM((nq, NA_KEY_ROWS * GRID_W + l), F32)],
        compiler_params=_cparams(("parallel", "arbitrary")),
        name="na_attention",
    )(q, k, v, kc, vc, bias_tab)


DN_LEVELS = int(math.log2(DN_CHUNK))


def _dn_constants():
    c, w = DN_CHUNK, DN_W
    ii = np.arange(c)[:, None]
    jj = (np.arange(w) % c)[None, :]
    hh = (np.arange(w) // HEAD_DIM)[None, :]
    tri, expand, neg, strict, lvl = [], [], [], [], []
    for d in range(2):
        rev = d == 1
        incl = (ii <= jj) if rev else (ii >= jj)
        neg.append(np.where(incl, 0.0, NEG_INF))
        strict.append((ii < jj) if rev else (ii > jj))
        t = np.arange(c)
        tri.append((t[:, None] <= t[None, :]) if rev else (t[:, None] >= t[None, :]))
        expand.append(np.arange(V7X_LANES)[:, None] == hh + d * DN_HEADS)
        for lv in range(DN_LEVELS):
            same = (ii >> (lv + 1)) == (jj >> (lv + 1))
            hi_i = ((ii >> lv) & 1) == 1
            hi_j = ((jj >> lv) & 1) == 1
            lvl.append((same & ~hi_i & hi_j) if rev else (same & hi_i & ~hi_j))
    r2 = np.arange(w)
    block = (r2[:, None] // HEAD_DIM) == (r2[None, :] // HEAD_DIM)
    f = lambda a, dt: jnp.asarray(np.stack(a).astype(np.float32), dtype=dt)
    return dict(tri=f(tri, F32), expand=f(expand, F32), neg=f(neg, F32), strict=f(strict, F32),
                lvl=f(lvl, BF16).reshape(2, DN_LEVELS, c, w), eye=jnp.asarray((ii == jj).astype(np.float32)),
                block16=jnp.asarray(block.astype(np.float32), dtype=BF16), block32=jnp.asarray(block.astype(np.float32)))


def _dn_block_diag(y16, block16):
    return jnp.concatenate([y16] * DN_HEADS, axis=0) * block16


def _dn_prep_kernel(q_ref, k_ref, v_ref, g_ref, b_ref, tri_ref, exp_ref, neg_ref, strict_ref, lvl_ref, eye_ref,
                    blk_ref, *out_refs, chunks):
    c = DN_CHUNK
    nt = (((1,), (1,)), ((), ()))

    def bdmm(x, y):
        return jnp.dot(x.astype(BF16), _dn_block_diag(y.astype(BF16), blk_ref[...]), preferred_element_type=F32)

    def chunk_group(gi, carry):
        probs = []
        for sub in range(DN_PREP_INTERLEAVE):
            ci = gi * DN_PREP_INTERLEAVE + sub
            rows = pl.ds(pl.multiple_of(ci * c, c), c)
            q = q_ref[0, rows, :] * (HEAD_DIM ** -0.5)
            k = k_ref[0, rows, :]
            v = v_ref[0, rows, :]
            gcol = g_ref[0, rows, :]
            bcol = b_ref[0, rows, :]
            kbd = _dn_block_diag(k.astype(BF16), blk_ref[...])
            kk = lax.dot_general(k.astype(BF16), kbd, nt, preferred_element_type=F32)
            qk = lax.dot_general(q.astype(BF16), kbd, nt, preferred_element_type=F32)
            for d in range(2):
                gc = jnp.dot(tri_ref[d], gcol, preferred_element_type=F32, precision=HIGHEST)
                gcx = jnp.dot(gc, exp_ref[d], preferred_element_type=F32, precision=HIGHEST)
                bx = jnp.dot(bcol, exp_ref[d], preferred_element_type=F32, precision=HIGHEST)
                gr = jnp.sum(gcx * eye_ref[...], axis=0, keepdims=True)
                decay = jnp.exp(gcx - gr + neg_ref[d])
                a16 = (kk * bx * decay * strict_ref[d]).astype(BF16)
                probs.append(dict(d=d, ci=ci, rows=rows, q=q, k=k, v=v, qk=qk, gcx=gcx, bx=bx, decay=decay, a16=a16,
                                  t=eye_ref[...]))
        for lv in range(DN_LEVELS):
            for p in probs:
                p["x"] = jnp.dot(p["t"].astype(BF16), _dn_block_diag(p["a16"] * lvl_ref[p["d"], lv], blk_ref[...]),
                                 preferred_element_type=F32)
            for p in probs:
                p["t"] = p["t"] - bdmm(p["x"], p["t"])
        for p in probs:
            d, rows, gcx = p["d"], p["rows"], p["gcx"]
            u_ref, w_ref, a_ref, qg_ref, kd_ref, eg_ref = out_refs[6 * d:6 * d + 6]
            last = 0 if d == 1 else c - 1
            e_gc = jnp.exp(gcx)
            g_last = gcx[last:last + 1, :]
            u_ref[0, rows, :] = bdmm(p["t"], p["v"] * p["bx"]).astype(u_ref.dtype)
            w_ref[0, rows, :] = bdmm(p["t"], p["k"] * p["bx"] * e_gc).astype(w_ref.dtype)
            a_ref[0, rows, :] = (p["qk"] * p["decay"]).astype(a_ref.dtype)
            qg_ref[0, rows, :] = (p["q"] * e_gc).astype(qg_ref.dtype)
            kd_ref[0, rows, :] = (p["k"] * jnp.exp(g_last - gcx)).astype(kd_ref.dtype)
            eg_ref[0, p["ci"]] = jnp.exp(g_last)
        return carry

    lax.fori_loop(0, chunks // DN_PREP_INTERLEAVE, chunk_group, 0)


def _dn_scan_kernel(*refs, chunks):
    ins = (refs[0:6], refs[6:12])
    s0_refs = refs[12:14]
    b16_ref, b32_ref = refs[14:16]
    o_refs = refs[16:18]
    sf_refs = refs[18:20]
    s_scr = refs[20]
    c = DN_CHUNK
    blk = pl.program_id(1)

    @pl.when(blk == 0)
    def _():
        s_scr[0] = s0_refs[0][0]
        s_scr[1] = s0_refs[1][0]

    def chunk(ci, carry):
        st = []
        for d in range(2):
            idx = (chunks - 1 - ci) if d == 1 else ci
            st.append(dict(idx=idx, rows=pl.ds(pl.multiple_of(idx * c, c), c), s=s_scr[d]))
        for d, p in enumerate(st):
            u_ref, w_ref, a_ref, qg_ref, kd_ref, eg_ref = ins[d]
            wq = jnp.concatenate([w_ref[0, p["rows"], :], qg_ref[0, p["rows"], :]], axis=0)
            p["r"] = jnp.dot(wq, p["s"].astype(BF16), preferred_element_type=F32)
        for d, p in enumerate(st):
            u_ref = ins[d][0]
            p["vn"] = (u_ref[0, p["rows"], :].astype(F32) - p["r"][:c]).astype(BF16)
        for d, p in enumerate(st):
            kd_ref, eg_ref = ins[d][4], ins[d][5]
            kv = lax.dot_general(kd_ref[0, p["rows"], :], p["vn"], (((0,), (0,)), ((), ())), preferred_element_type=F32)
            s_scr[d] = p["s"] * eg_ref[0, p["idx"]] + kv * b32_ref[...]
        for d, p in enumerate(st):
            a_ref = ins[d][2]
            o_refs[d][0, p["rows"], :] = p["r"][c:] + jnp.dot(a_ref[0, p["rows"], :],
                                                              _dn_block_diag(p["vn"], b16_ref[...]),
                                                              preferred_element_type=F32)
        return carry

    lax.fori_loop(0, chunks, chunk, 0)

    @pl.when(blk == pl.num_programs(1) - 1)
    def _():
        sf_refs[0][0] = s_scr[0]
        sf_refs[1][0] = s_scr[1]


def _delta_bidir(q, k, v, g, beta, s0_f, s0_b):
    b, n, w = q.shape
    nchunks = n // DN_CHUNK
    chunks = min(DN_BLOCK_CHUNKS, nchunks)
    nblk = nchunks // chunks
    bt = chunks * DN_CHUNK

    tok = pl.BlockSpec((1, bt, w), lambda i, j: (i, j, 0))
    small = pl.BlockSpec((1, bt, V7X_LANES), lambda i, j: (i, j, 0))
    egl = pl.BlockSpec((1, chunks, 1, w), lambda i, j: (i, j, 0, 0))
    per_dir_shapes = [jax.ShapeDtypeStruct((b, n, w), BF16)] * 5 + [jax.ShapeDtypeStruct((b, nchunks, 1, w), F32)]
    cst = _dn_constants()

    def const_spec(a):
        nd = a.ndim
        return pl.BlockSpec(a.shape, lambda i, j: (0,) * nd)

    prep_consts = [cst[name] for name in ("tri", "expand", "neg", "strict", "lvl", "eye", "block16")]
    prep = pl.pallas_call(
        functools.partial(_dn_prep_kernel, chunks=chunks),
        grid=(b, nblk),
        in_specs=[tok, tok, tok, small, small] + [const_spec(a) for a in prep_consts],
        out_specs=([tok] * 5 + [egl]) * 2,
        out_shape=per_dir_shapes * 2,
        compiler_params=_cparams(("parallel", "parallel")),
        name="delta_prep",
    )(q, k, v, g, beta, *prep_consts)

    tok_r = pl.BlockSpec((1, bt, w), lambda i, j: (i, nblk - 1 - j, 0))
    egl_r = pl.BlockSpec((1, chunks, 1, w), lambda i, j: (i, nblk - 1 - j, 0, 0))
    state = pl.BlockSpec((1, w, w), lambda i, j: (i, 0, 0))
    scan_consts = [cst["block16"], cst["block32"]]
    o_f, o_b, s_f, s_b = pl.pallas_call(
        functools.partial(_dn_scan_kernel, chunks=chunks),
        grid=(b, nblk),
        in_specs=[tok] * 5 + [egl] + [tok_r] * 5 + [egl_r] + [state, state] + [const_spec(a) for a in scan_consts],
        out_specs=[tok, tok_r, state, state],
        out_shape=[jax.ShapeDtypeStruct((b, n, w), F32)] * 2 + [jax.ShapeDtypeStruct((b, w, w), F32)] * 2,
        scratch_shapes=[pltpu.VMEM((2, w, w), F32)],
        compiler_params=_cparams(("parallel", "arbitrary")),
        name="delta_scan",
    )(*prep, s0_f, s0_b, *scan_consts)
    return o_f, o_b, s_f, s_b


def _out_proj_kernel(x_ref, ya_ref, yb_ref, of_ref, ob_ref, dgate_ref, yd_ref, hm_ref, dg_ref, w_ref, gate_ref, g2_ref,
                     sh_ref, sc_ref, rw_ref, rb_ref, xo_ref, h2_ref, idx_ref, gt_ref):
    od = of_ref[0] + ob_ref[0]
    ms = jnp.dot(od * od, hm_ref[...], preferred_element_type=F32, precision=HIGHEST)
    yc = (od * lax.rsqrt(ms + EPS) * dg_ref[...] * jax.nn.silu(dgate_ref[0])).astype(BF16)
    acc = None
    for i, y in enumerate((ya_ref[0], yb_ref[0], yc, yd_ref[0])):
        wd = y.shape[-1]
        part = jnp.dot(y, w_ref[i * wd:(i + 1) * wd, :], preferred_element_type=F32)
        acc = part if acc is None else acc + part
    xn = x_ref[0] + gate_ref[0] * acc
    xo_ref[0] = xn
    h2 = _rms(xn, g2_ref[...]) * (1.0 + sc_ref[0]) + sh_ref[0]
    h2_ref[0] = h2
    logits = jnp.dot(h2, rw_ref[...], preferred_element_type=F32, precision=HIGHEST) + rb_ref[...]
    lane = lax.broadcasted_iota(jnp.int32, logits.shape, 1).astype(F32)
    vals, idxs = [], []
    cur = logits
    for _ in range(TOP_K):
        mx = jnp.max(cur, axis=-1, keepdims=True)
        ik = jnp.min(jnp.where(cur == mx, lane, float(V7X_LANES)), axis=-1, keepdims=True)
        vals.append(mx)
        idxs.append(ik)
        cur = jnp.where(lane == ik, -jnp.inf, cur)
    es = [jnp.exp(vv - vals[0]) for vv in vals]
    den = es[0] + es[1] + es[2] + es[3]
    idx_out = jnp.zeros(logits.shape, F32)
    gate_out = jnp.zeros(logits.shape, F32)
    for kk in range(TOP_K):
        idx_out = jnp.where(lane == float(kk), idxs[kk], idx_out)
        gate_out = jnp.where(lane == float(kk), es[kk] / den, gate_out)
    idx_ref[0] = idx_out.astype(jnp.int32)
    gt_ref[0] = gate_out


def _out_proj(x, ya, yb, o_f, o_b, dn_gate, dn_out_g, yd, w_out, gate, g2, shift, scale, rw_pad, rb_pad, tm):
    b, n, d = x.shape
    yw = ya.shape[-1]
    vec = pl.BlockSpec((1, 1, d), lambda i, j: (i, 0, 0))
    tok = pl.BlockSpec((1, tm, d), lambda i, j: (i, j, 0))
    ytok = pl.BlockSpec((1, tm, yw), lambda i, j: (i, j, 0))
    ltok = pl.BlockSpec((1, tm, V7X_LANES), lambda i, j: (i, j, 0))
    head = np.arange(yw) // HEAD_DIM
    head_mean = jnp.asarray((head[:, None] == head[None, :]).astype(np.float32) / HEAD_DIM)
    return pl.pallas_call(
        _out_proj_kernel,
        grid=(b, n // tm),
        in_specs=[tok, ytok, ytok, ytok, ytok, ytok, ytok,
                  pl.BlockSpec((yw, yw), lambda i, j: (0, 0)),
                  pl.BlockSpec((1, yw), lambda i, j: (0, 0)),
                  pl.BlockSpec((4 * yw, d), lambda i, j: (0, 0)),
                  vec,
                  pl.BlockSpec((1, d), lambda i, j: (0, 0)),
                  vec, vec,
                  pl.BlockSpec((d, V7X_LANES), lambda i, j: (0, 0)),
                  pl.BlockSpec((1, V7X_LANES), lambda i, j: (0, 0))],
        out_specs=[tok, tok, ltok, ltok],
        out_shape=[jax.ShapeDtypeStruct((b, n, d), F32), jax.ShapeDtypeStruct((b, n, d), F32),
                   jax.ShapeDtypeStruct((b, n, V7X_LANES), jnp.int32), jax.ShapeDtypeStruct((b, n, V7X_LANES), F32)],
        compiler_params=_cparams(("parallel", "parallel")),
        name="out_proj",
    )(x, ya, yb, o_f, o_b, dn_gate, yd, head_mean, jnp.tile(dn_out_g, DN_HEADS).reshape(1, yw), w_out,
      gate.reshape(b, 1, d), g2.reshape(1, d), shift.reshape(b, 1, d), scale.reshape(b, 1, d), rw_pad, rb_pad)


def _experts_kernel(te_ref, tf_ref, tv_ref, src0_ref, srcn_ref, h_hbm, w1_ref, b1_ref, w2_ref, b2_ref, o_ref,
                    xbuf, w1b, w2b, sem, *, fc):
    i = pl.program_id(0)
    nt = pl.num_programs(0)
    tm = o_ref.shape[0]
    f = w2_ref.shape[1]
    slot = i % 2

    def row_copy(src_ref, r, dst_slot):
        return pltpu.make_async_copy(h_hbm.at[pl.ds(src_ref[0, 0, r], 1)], xbuf.at[dst_slot, pl.ds(r, 1)],
                                     sem.at[dst_slot])

    def issue(src_ref, dst_slot):
        def body(r8, carry):
            for u in range(MOE_ISSUE_UNROLL):
                row_copy(src_ref, r8 * MOE_ISSUE_UNROLL + u, dst_slot).start()
            return carry
        lax.fori_loop(0, tm // MOE_ISSUE_UNROLL, body, 0)

    def wait_tile(s):
        pltpu.make_async_copy(h_hbm.at[pl.ds(0, tm)], xbuf.at[s], sem.at[s]).wait()

    @pl.when(jnp.logical_and(i == 0, tv_ref[0] != 0))
    def _():
        issue(src0_ref, 0)

    @pl.when(tv_ref[i] == 0)
    def _():
        @pl.when(jnp.logical_and(i > 0, tv_ref[jnp.maximum(i - 1, 0)] != 0))
        def _():
            wait_tile(slot)
        o_ref[...] = jnp.zeros(o_ref.shape, o_ref.dtype)

    @pl.when(tv_ref[i] != 0)
    def _():
        issue(srcn_ref, 1 - slot)

        @pl.when(tf_ref[i] != 0)
        def _():
            w1b[...] = w1_ref[0].astype(BF16)
            w2b[...] = w2_ref[0].astype(BF16)

        wait_tile(slot)
        xb = xbuf[slot].astype(BF16)
        acc = None
        for j in range(f // fc):
            glu = jnp.dot(xb, w1b[:, j * fc:(j + 1) * fc], preferred_element_type=F32) + b1_ref[0, :, j * fc:(j + 1) * fc]
            lin = (jnp.dot(xb, w1b[:, f + j * fc:f + (j + 1) * fc], preferred_element_type=F32)
                   + b1_ref[0, :, f + j * fc:f + (j + 1) * fc])
            glu = jnp.minimum(glu, SWIGLU_LIMIT)
            lin = jnp.clip(lin, -SWIGLU_LIMIT, SWIGLU_LIMIT)
            act = glu * jax.nn.sigmoid(SWIGLU_ALPHA * glu) * (lin + 1.0)
            part = jnp.dot(act.astype(BF16), w2b[j * fc:(j + 1) * fc, :], preferred_element_type=F32)
            acc = part if acc is None else acc + part
        o_ref[...] = acc + b2_ref[0]

        @pl.when(i == nt - 1)
        def _():
            wait_tile(1 - slot)


def _expert_tiles(h, src, tile_e, tile_first, tile_valid, w1, b1, w2, b2, tm):
    t, d = h.shape
    r = src.shape[0]
    e, _, f2 = w1.shape
    f = f2 // 2
    nt = r // tm
    grid_spec = pltpu.PrefetchScalarGridSpec(
        num_scalar_prefetch=3,
        grid=(nt,),
        in_specs=[pl.BlockSpec((1, 1, tm), lambda i, te, tf, tv: (0, 0, 0), memory_space=pltpu.SMEM),
                  pl.BlockSpec((1, 1, tm), lambda i, te, tf, tv: (i + 1, 0, 0), memory_space=pltpu.SMEM),
                  pl.BlockSpec(memory_space=pl.ANY),
                  pl.BlockSpec((1, d, f2), lambda i, te, tf, tv: (te[i], 0, 0)),
                  pl.BlockSpec((1, 1, f2), lambda i, te, tf, tv: (te[i], 0, 0)),
                  pl.BlockSpec((1, f, d), lambda i, te, tf, tv: (te[i], 0, 0)),
                  pl.BlockSpec((1, 1, d), lambda i, te, tf, tv: (te[i], 0, 0))],
        out_specs=pl.BlockSpec((tm, d), lambda i, te, tf, tv: (i, 0)),
        scratch_shapes=[pltpu.VMEM((2, tm, d), F32), pltpu.VMEM((d, f2), BF16), pltpu.VMEM((f, d), BF16),
                        pltpu.SemaphoreType.DMA((2,))],
    )
    src3 = jnp.concatenate([src, jnp.zeros((tm,), src.dtype)]).reshape(nt + 1, 1, tm)
    return pl.pallas_call(
        functools.partial(_experts_kernel, fc=MOE_FC),
        grid_spec=grid_spec,
        out_shape=jax.ShapeDtypeStruct((r, d), F32),
        compiler_params=_cparams(("arbitrary",)),
        name="moe_experts",
    )(tile_e, tile_first, tile_valid, src3, src3, h, w1, b1.reshape(e, 1, f2), w2, b2.reshape(e, 1, d))


def _combine_kernel(pos_ref, ys_hbm, gate_ref, x_ref, mod_ref, o_ref, buf, sem):
    ct = o_ref.shape[0]

    def issue(r2, carry):
        for u in range(COMBINE_ISSUE_UNROLL):
            r = r2 * COMBINE_ISSUE_UNROLL + u
            for kk in range(TOP_K):
                p = pos_ref[0, 0, r * TOP_K + kk]
                pltpu.make_async_copy(ys_hbm.at[pl.ds(p, 1)], buf.at[kk, pl.ds(r, 1)], sem).start()
        return carry

    lax.fori_loop(0, ct // COMBINE_ISSUE_UNROLL, issue, 0)
    for kk in range(TOP_K):
        pltpu.make_async_copy(ys_hbm.at[pl.ds(0, ct)], buf.at[kk], sem).wait()
    g = gate_ref[...]
    mix = (g[:, 0:1] * buf[0] + g[:, 1:2] * buf[1]) + (g[:, 2:3] * buf[2] + g[:, 3:4] * buf[3])
    o_ref[...] = x_ref[...] + mod_ref[0] * mix


def _combine_rows(ys, pos, gates, x, mod, ct):
    b, n, d = x.shape
    t = b * n
    ct = min(ct, n)
    nsteps = t // ct
    per_batch = n // ct
    out = pl.pallas_call(
        _combine_kernel,
        grid=(nsteps,),
        in_specs=[pl.BlockSpec((1, 1, ct * TOP_K), lambda i: (i, 0, 0), memory_space=pltpu.SMEM),
                  pl.BlockSpec(memory_space=pl.ANY),
                  pl.BlockSpec((ct, V7X_LANES), lambda i: (i, 0)),
                  pl.BlockSpec((ct, d), lambda i: (i, 0)),
                  pl.BlockSpec((1, 1, d), lambda i: (i // per_batch, 0, 0))],
        out_specs=pl.BlockSpec((ct, d), lambda i: (i, 0)),
        out_shape=jax.ShapeDtypeStruct((t, d), F32),
        scratch_shapes=[pltpu.VMEM((TOP_K, ct, d), F32), pltpu.SemaphoreType.DMA],
        compiler_params=_cparams(("arbitrary",)),
        name="moe_combine",
    )(pos.reshape(nsteps, 1, ct * TOP_K), ys, gates, x.reshape(t, d), mod.reshape(b, 1, d))
    return out.reshape(b, n, d)


def _route_plan(idx4, tm):
    t = idx4.shape[0]
    e = N_EXPERTS
    r_max = t * TOP_K + e * tm
    nt = r_max // tm
    onehot = (idx4[:, :, None] == jnp.arange(e, dtype=jnp.int32)[None, None, :]).astype(jnp.int32)
    member = jnp.sum(onehot, axis=1)
    csum = jnp.cumsum(member, axis=0)
    cnt = csum[-1]
    excl = csum - member
    cnt_pad = ((cnt + tm - 1) // tm) * tm
    ends = jnp.cumsum(cnt_pad)
    base = ends - cnt_pad
    pos = jnp.sum(onehot * (excl + base[None, :])[:, None, :], axis=-1)
    flat = pos.reshape(-1)
    tok = jnp.repeat(jnp.arange(t, dtype=jnp.int32), TOP_K)
    src = jnp.zeros((r_max,), jnp.int32).at[flat].set(tok, unique_indices=True)
    tile_start = jnp.arange(nt, dtype=jnp.int32) * tm
    tile_e = jnp.minimum(jnp.sum((ends[None, :] <= tile_start[:, None]).astype(jnp.int32), axis=1), e - 1)
    tile_valid = (tile_start < ends[-1]).astype(jnp.int32)
    tile_first = jnp.concatenate([jnp.ones((1,), jnp.int32), (tile_e[1:] != tile_e[:-1]).astype(jnp.int32)])
    return pos.astype(jnp.int32), src, tile_e, tile_first, tile_valid


def _moe_residual(h2, idx4, gates, w1, b1, w2, b2, streams):
    pos, src, tile_e, tile_first, tile_valid = _route_plan(idx4, MOE_TM)
    ys = _expert_tiles(h2, src, tile_e, tile_first, tile_valid, w1, b1, w2, b2, MOE_TM)
    outs, start = [], 0
    for x, mod in streams:
        cnt = x.shape[0] * x.shape[1]
        outs.append(_combine_rows(ys, pos[start:start + cnt], gates[start:start + cnt], x, mod, COMBINE_ROWS))
        start += cnt
    return outs


def _axial_rope_tables(n_tok, rot_dim):
    t = jnp.arange(n_tok, dtype=jnp.int32)
    row = (t // GRID_W).astype(F32)
    col = (t % GRID_W).astype(F32)
    n_freq = rot_dim // 4
    freqs = ROPE_THETA ** (-jnp.arange(n_freq, dtype=F32) / n_freq)
    ang = jnp.concatenate([row[:, None] * freqs, col[:, None] * freqs], axis=-1)
    return jnp.cos(ang), jnp.sin(ang)


def _rope_lane_tables(cos, sin, start):
    n, nf = cos.shape
    c2 = jnp.repeat(cos, 2, axis=1)
    s2 = jnp.stack([-sin, sin], axis=-1).reshape(n, 2 * nf)
    tail = V7X_LANES - start - 2 * nf
    c = jnp.concatenate([jnp.ones((n, start), F32), c2, jnp.ones((n, tail), F32)], axis=1)
    s = jnp.concatenate([jnp.zeros((n, start), F32), s2, jnp.zeros((n, tail), F32)], axis=1)
    return c, s


def _rope_groups(t4, tables):
    c, s = tables
    even = (jnp.arange(V7X_LANES) % 2) == 0
    swapped = jnp.where(even, jnp.roll(t4, -1, axis=-1), jnp.roll(t4, 1, axis=-1))
    return t4 * c[None, :, None, :] + swapped * s[None, :, None, :]


def _groups(t, nheads):
    return t.reshape(t.shape[0], t.shape[1], nheads, V7X_LANES)


def _flat16(t4):
    return t4.reshape(t4.shape[0], t4.shape[1], t4.shape[2] * V7X_LANES).astype(BF16)


def _rms_groups(t4, g):
    width = g.shape[0]
    ms = jnp.sum(t4 * t4, axis=-1, keepdims=True) / width
    return t4 * lax.rsqrt(ms + EPS) * jnp.pad(g, (0, V7X_LANES - width))


def _with_ones(t4, dv):
    return t4 + (jnp.arange(V7X_LANES) == dv).astype(F32)


def _heads(t, n):
    return t.reshape(t.shape[0], t.shape[1], n, t.shape[2] // n)


def _permute_w_in(w_in):
    offs = (0,) + IN_OFFSETS

    def seg(i):
        return np.arange(offs[i], offs[i] + IN_SIZES[i])

    def head_groups(i, nheads):
        cols = seg(i).reshape(nheads, HEAD_DIM)
        return np.concatenate([cols, -np.ones((nheads, V7X_LANES - HEAD_DIM), np.int64)], axis=1).reshape(-1)

    cols = np.concatenate([head_groups(0, NA_HEADS), head_groups(1, NA_HEADS), head_groups(2, NA_HEADS),
                           head_groups(3, GQA_Q_HEADS), head_groups(4, GQA_KV_HEADS), head_groups(5, GQA_KV_HEADS),
                           seg(6), seg(7), seg(8), seg(9), seg(12), seg(13), seg(14), seg(10), seg(11)])
    cols = np.concatenate([cols, -np.ones(sum(PROJ_GROUPS) - cols.size, np.int64)])
    w = jnp.take(w_in, jnp.asarray(np.maximum(cols, 0)), axis=1)
    return jnp.where(jnp.asarray(cols >= 0)[None, :], w, 0.0).astype(BF16)


def _dn_prep(qkv, small, conv_w, a_log, dt_bias):
    b, n, _ = qkv.shape
    pad = DN_CONV // 2
    xp = jnp.pad(qkv, ((0, 0), (pad, pad), (0, 0)))
    conv = sum(xp[:, i:i + n, :] * conv_w[i][None, None, :] for i in range(DN_CONV))
    act = jax.nn.silu(conv)
    q, k, v = jnp.split(act, 3, axis=-1)

    def l2(t):
        th = _heads(t, DN_HEADS)
        return (th * lax.rsqrt(jnp.sum(th * th, axis=-1, keepdims=True) + EPS)).reshape(b, n, DN_W)

    beta_raw = small[..., MLA_ROPE:MLA_ROPE + 2 * DN_HEADS]
    a_raw = small[..., MLA_ROPE + 2 * DN_HEADS:MLA_ROPE + 4 * DN_HEADS]
    beta = jax.nn.sigmoid(beta_raw)
    g = -jnp.exp(a_log.reshape(-1)) * jax.nn.softplus(a_raw + dt_bias.reshape(-1))
    lane_pad = ((0, 0), (0, 0), (0, V7X_LANES - 2 * DN_HEADS))
    return l2(q), l2(k), v, jnp.pad(g, lane_pad), jnp.pad(beta, lane_pad)


def _mla_weights(w_uq, w_ukv):
    wq = jnp.pad(w_uq.reshape(MLA_Q_RANK, MLA_HEADS, MLA_QK), ((0, 0), (0, 0), (0, V7X_LANES - MLA_QK)))
    wkv = w_ukv.reshape(MLA_KV_RANK, MLA_HEADS, MLA_NOPE + MLA_V)
    wk = jnp.pad(wkv[..., :MLA_NOPE], ((0, 0), (0, 0), (0, V7X_LANES - MLA_NOPE)))
    wv = jnp.pad(wkv[..., MLA_NOPE:], ((0, 0), (0, 0), (0, V7X_LANES - MLA_V)))
    width = MLA_HEADS * V7X_LANES
    return wq.reshape(MLA_Q_RANK, width), jnp.concatenate([wk.reshape(MLA_KV_RANK, width), wv.reshape(MLA_KV_RANK, width)], axis=1)


def _mla_kv(ckv, kpe, ckv_g, w_kv, kn_g, rope):
    b, n, _ = ckv.shape
    width = MLA_HEADS * V7X_LANES
    kv = _norm_mm(ckv.reshape(b * n, MLA_KV_RANK), ckv_g, w_kv, min(1024, b * n)).reshape(b, n, 2 * width)
    kpe_lanes = jnp.pad(kpe, ((0, 0), (0, 0), (MLA_NOPE, V7X_LANES - MLA_QK)))
    k = _rms_groups(_groups(kv[..., :width], MLA_HEADS) + kpe_lanes[:, :, None, :], kn_g)
    if rope is not None:
        k = _rope_groups(k, rope)
    return _flat16(k), _flat16(_with_ones(_groups(kv[..., width:], MLA_HEADS), MLA_V))


def _mla_q(cq, cq_g, w_q, qn_g, rope):
    b, n, _ = cq.shape
    q = _norm_mm(cq.reshape(b * n, MLA_Q_RANK), cq_g, w_q, min(1024, b * n)).reshape(b, n, MLA_HEADS, V7X_LANES)
    q = _rms_groups(q, qn_g)
    if rope is not None:
        q = _rope_groups(q, rope)
    return _flat16(q * (MLA_QK ** -0.5))


def kernel(x, c, ctx, c_ctx, ada_w, ada_b, norm1_g, norm2_g, w_in, w_out, na_qn_g, na_kn_g, na_rel_bias, gqa_qn_g, gqa_kn_g, dn_conv_w, dn_a_log, dn_dt_bias, dn_out_g, mla_cq_g, mla_ckv_g, mla_w_uq, mla_w_ukv, mla_qn_g, mla_kn_g, router_w, router_b, exp_w1, exp_b1, exp_w2, exp_b2):
    b, n, d = x.shape
    l = ctx.shape[1]
    depth = ada_w.shape[0]
    rope_g = _rope_lane_tables(*_axial_rope_tables(n, HEAD_DIM), 0)
    rope_m = _rope_lane_tables(*_axial_rope_tables(n, MLA_ROPE), MLA_NOPE)
    cond = jnp.concatenate([jax.nn.silu(c), jax.nn.silu(c_ctx)[None], jnp.zeros((16 - b - 1, d), F32)], axis=0)
    s0 = jnp.zeros((b, DN_W, DN_W), F32)
    scale = HEAD_DIM ** -0.5

    for ly in range(depth):
        with_ctx = ly < depth - 1
        mod_all = _mm_bias(cond, ada_w[ly], ada_b[ly], 1024)
        mod = jnp.split(mod_all[:b], N_ADA, axis=-1)
        mod_c = [jnp.broadcast_to(m_, (b, d)) for m_ in jnp.split(mod_all[b:b + 1], N_ADA, axis=-1)]
        w_perm = _permute_w_in(w_in[ly])
        p = _in_proj(x, norm1_g[ly], mod[0], mod[1], w_perm, PROJ_TM)
        pc = _in_proj(ctx, norm1_g[ly], mod_c[0], mod_c[1], w_perm, l)
        bias_tab = _na_bias_table(na_rel_bias[ly], n // GRID_W)

        qa = _flat16(_rms_groups(_groups(p[0], NA_HEADS), na_qn_g[ly]) * scale)
        ka = _flat16(_rms_groups(_groups(p[1], NA_HEADS), na_kn_g[ly]))
        va = _flat16(_with_ones(_groups(p[2], NA_HEADS), HEAD_DIM))
        kac = _flat16(_rms_groups(_groups(pc[1], NA_HEADS), na_kn_g[ly]))
        vac = _flat16(_with_ones(_groups(pc[2], NA_HEADS), HEAD_DIM))
        ya = _na_attention(qa, ka, va, kac, vac, bias_tab)

        qb = _flat16(_rope_groups(_rms_groups(_groups(p[3], GQA_Q_HEADS), gqa_qn_g[ly]), rope_g) * scale)
        kb = _flat16(_rope_groups(_rms_groups(_groups(p[4], GQA_KV_HEADS), gqa_kn_g[ly]), rope_g))
        vb = _flat16(_with_ones(_groups(p[5], GQA_KV_HEADS), HEAD_DIM))
        kbc = _flat16(_rms_groups(_groups(pc[4], GQA_KV_HEADS), gqa_kn_g[ly]))
        vbc = _flat16(_with_ones(_groups(pc[5], GQA_KV_HEADS), HEAD_DIM))
        yb = _attention(qb, jnp.concatenate([kb, kbc], axis=1), jnp.concatenate([vb, vbc], axis=1),
                        GQA_Q_HEADS, GQA_KV_HEADS, HEAD_DIM)

        qdc, kdc, vdc, gdc, bdc = _dn_prep(pc[6], pc[10], dn_conv_w[ly], dn_a_log[ly], dn_dt_bias[ly])
        odc_f, odc_b, s_f, s_b = _delta_bidir(qdc, kdc, vdc, gdc, bdc, s0, s0)
        qd, kd, vd, gd, bd = _dn_prep(p[6], p[10], dn_conv_w[ly], dn_a_log[ly], dn_dt_bias[ly])
        od_f, od_b, _, _ = _delta_bidir(qd, kd, vd, gd, bd, s_f, s_b)

        w_q, w_kv = _mla_weights(mla_w_uq[ly], mla_w_ukv[ly])
        qm = _mla_q(p[8], mla_cq_g[ly], w_q, mla_qn_g[ly], rope_m)
        km, vm = _mla_kv(p[9], p[10][..., :MLA_ROPE], mla_ckv_g[ly], w_kv, mla_kn_g[ly], rope_m)
        kmc, vmc = _mla_kv(pc[9], pc[10][..., :MLA_ROPE], mla_ckv_g[ly], w_kv, mla_kn_g[ly], None)
        yd = _attention(qm, jnp.concatenate([km, kmc], axis=1), jnp.concatenate([vm, vmc], axis=1),
                        MLA_HEADS, MLA_HEADS, MLA_V)

        rw_pad = jnp.pad(router_w[ly], ((0, 0), (0, V7X_LANES - N_EXPERTS)))
        rb_pad = jnp.pad(router_b[ly], (0, V7X_LANES - N_EXPERTS), constant_values=NEG_INF).reshape(1, V7X_LANES)
        w_out_b = w_out[ly].astype(BF16)
        x, h2, idx, gts = _out_proj(x, ya, yb, od_f, od_b, p[7], dn_out_g[ly], yd, w_out_b, mod[2], norm2_g[ly],
                                    mod[3], mod[4], rw_pad, rb_pad, PROJ_TM)
        tok = h2.reshape(b * n, d)
        idx4 = idx.reshape(b * n, V7X_LANES)[:, :TOP_K]
        gates = gts.reshape(b * n, V7X_LANES)

        if with_ctx:
            qac = _flat16(_rms_groups(_groups(pc[0], NA_HEADS), na_qn_g[ly]) * scale)
            yac = _attention(qac, kac, vac, NA_HEADS, NA_HEADS, HEAD_DIM)
            qbc = _flat16(_rms_groups(_groups(pc[3], GQA_Q_HEADS), gqa_qn_g[ly]) * scale)
            ybc = _attention(qbc, kbc, vbc, GQA_Q_HEADS, GQA_KV_HEADS, HEAD_DIM)
            qmc = _mla_q(pc[8], mla_cq_g[ly], w_q, mla_qn_g[ly], None)
            ydc = _attention(qmc, kmc, vmc, MLA_HEADS, MLA_HEADS, MLA_V)
            ctx, h2c, idxc, gtsc = _out_proj(ctx, yac, ybc, odc_f, odc_b, pc[7], dn_out_g[ly], ydc, w_out_b, mod_c[2],
                                             norm2_g[ly], mod_c[3], mod_c[4], rw_pad, rb_pad, l)
            tok = jnp.concatenate([tok, h2c.reshape(b * l, d)], axis=0)
            idx4 = jnp.concatenate([idx4, idxc.reshape(b * l, V7X_LANES)[:, :TOP_K]], axis=0)
            gates = jnp.concatenate([gates, gtsc.reshape(b * l, V7X_LANES)], axis=0)

        streams = [(x, mod[5])] + ([(ctx, mod_c[5])] if with_ctx else [])
        outs = _moe_residual(tok, idx4, gates, exp_w1[ly], exp_b1[ly], exp_w2[ly], exp_b2[ly], streams)
        x = outs[0]
        if with_ctx:
            ctx = outs[1]
    return x
```

```python
import functools
import math

import jax
import jax.numpy as jnp
import numpy as np
from jax import lax
from jax.experimental import pallas as pl
from jax.experimental.pallas import tpu as pltpu

F32 = jnp.float32
BF16 = jnp.bfloat16
HIGHEST = lax.Precision.HIGHEST

GRID_W = 64
HEAD_DIM = 64
NA_HEADS = 4
NA_WIN_R = 8
NA_WIN_C = 16
GQA_Q_HEADS = 4
GQA_KV_HEADS = 2
DN_HEADS = 4
DN_CONV = 5
DN_CHUNK = 64
MLA_HEADS = 4
MLA_Q_RANK = 256
MLA_KV_RANK = 128
MLA_NOPE = 64
MLA_ROPE = 32
MLA_V = 64
MLA_QK = MLA_NOPE + MLA_ROPE
N_EXPERTS = 32
TOP_K = 4
SWIGLU_LIMIT = 7.0
SWIGLU_ALPHA = 1.702
ROPE_THETA = 10000.0
EPS = 1e-6
NEG_INF = -1e30
N_ADA = 6

NA_W = NA_HEADS * HEAD_DIM
GQA_QW = GQA_Q_HEADS * HEAD_DIM
GQA_KVW = GQA_KV_HEADS * HEAD_DIM
DN_W = DN_HEADS * HEAD_DIM
MLA_W = MLA_HEADS * MLA_V
IN_SIZES = (NA_W, NA_W, NA_W, GQA_QW, GQA_KVW, GQA_KVW, DN_W, DN_W, DN_W, DN_W, 2 * DN_HEADS, 2 * DN_HEADS,
            MLA_Q_RANK, MLA_KV_RANK, MLA_ROPE)
IN_OFFSETS = tuple(int(o) for o in np.cumsum(IN_SIZES)[:-1])

V7X_VMEM_BYTES = 64 * 1024 * 1024
V7X_LANES = 128
VMEM_LIMIT = V7X_VMEM_BYTES - 8 * 1024 * 1024

PROJ_TM = 512
ATTN_ROWS = 256
ATTN_TK = 256
NA_GROUP = 8
NA_KEY_ROWS = 16
DN_BLOCK_CHUNKS = 8
DN_PREP_INTERLEAVE = 2
MOE_TM = 512
MOE_FC = 512
MOE_ISSUE_UNROLL = 8
COMBINE_ROWS = 256
COMBINE_ISSUE_UNROLL = 2


def _cparams(semantics):
    return pltpu.CompilerParams(dimension_semantics=semantics, vmem_limit_bytes=VMEM_LIMIT)


def _rms(x, g):
    return x * lax.rsqrt(jnp.mean(x * x, axis=-1, keepdims=True) + EPS) * g


def _mm_bias_kernel(x_ref, w_ref, b_ref, o_ref):
    o_ref[...] = jnp.dot(x_ref[...].astype(BF16), w_ref[...].astype(BF16), preferred_element_type=F32) + b_ref[...]


def _mm_bias(x, w, b, tn):
    m, k = x.shape
    n = w.shape[1]
    return pl.pallas_call(
        _mm_bias_kernel,
        grid=(n // tn,),
        in_specs=[pl.BlockSpec((m, k), lambda j: (0, 0)),
                  pl.BlockSpec((k, tn), lambda j: (0, j)),
                  pl.BlockSpec((1, tn), lambda j: (0, j))],
        out_specs=pl.BlockSpec((m, tn), lambda j: (0, j)),
        out_shape=jax.ShapeDtypeStruct((m, n), F32),
        compiler_params=_cparams(("parallel",)),
        name="ada_mm",
    )(x, w, b.reshape(1, n))


def _head_epilogue(x, gain, width, rope, scale, ones_lane):
    lane = lax.broadcasted_iota(jnp.int32, (1, V7X_LANES), 1)
    if gain is not None:
        ms = jnp.sum(x * x, axis=-1, keepdims=True) * (1.0 / width)
        x = x * lax.rsqrt(ms + EPS) * gain
    if rope is not None:
        swapped = jnp.where((lane % 2) == 0, pltpu.roll(x, V7X_LANES - 1, 1), pltpu.roll(x, 1, 1))
        x = x * rope[0] + swapped * rope[1]
    if scale != 1.0:
        x = x * scale
    if ones_lane is not None:
        x = x + (lane == ones_lane).astype(F32)
    return x


def _mla_q_kernel(x_ref, g_ref, w_ref, qn_ref, rc_ref, rs_ref, o_ref, *, use_rope):
    y = _rms(x_ref[...], g_ref[...])
    p = jnp.dot(y.astype(BF16), w_ref[...], preferred_element_type=F32)
    rope = (rc_ref[...], rs_ref[...]) if use_rope else None
    for h in range(MLA_HEADS):
        sl = slice(h * V7X_LANES, (h + 1) * V7X_LANES)
        o_ref[:, sl] = _head_epilogue(p[:, sl], qn_ref[...], MLA_QK, rope, MLA_QK ** -0.5, None).astype(o_ref.dtype)


def _mla_kv_kernel(x_ref, g_ref, w_ref, small_ref, kn_ref, rc_ref, rs_ref, k_ref, v_ref, *, use_rope):
    y = _rms(x_ref[...], g_ref[...])
    p = jnp.dot(y.astype(BF16), w_ref[...], preferred_element_type=F32)
    lane = lax.broadcasted_iota(jnp.int32, (1, V7X_LANES), 1)
    kpe = pltpu.roll(jnp.where(lane < MLA_ROPE, small_ref[...], 0.0), MLA_NOPE, 1)
    rope = (rc_ref[...], rs_ref[...]) if use_rope else None
    width = MLA_HEADS * V7X_LANES
    for h in range(MLA_HEADS):
        sl = slice(h * V7X_LANES, (h + 1) * V7X_LANES)
        k_ref[:, sl] = _head_epilogue(p[:, sl] + kpe, kn_ref[...], MLA_QK, rope, 1.0, None).astype(k_ref.dtype)
        vs = slice(width + h * V7X_LANES, width + (h + 1) * V7X_LANES)
        v_ref[:, sl] = _head_epilogue(p[:, vs], None, MLA_V, None, 1.0, MLA_V).astype(v_ref.dtype)


def _lane_gain(g):
    return jnp.pad(g, (0, V7X_LANES - g.shape[0])).reshape(1, V7X_LANES)


def _mla_q(cq, cq_g, w_q, qn_g, rope, tm=1024):
    b, n, k = cq.shape
    tm = min(tm, n)
    width = MLA_HEADS * V7X_LANES
    per_batch = n // tm
    use_rope = rope is not None
    tabs = rope if use_rope else (jnp.zeros((tm, V7X_LANES), F32),) * 2
    tab_spec = pl.BlockSpec((tm, V7X_LANES), (lambda i: (i % per_batch, 0)) if use_rope else (lambda i: (0, 0)))
    out = pl.pallas_call(
        functools.partial(_mla_q_kernel, use_rope=use_rope),
        grid=(b * n // tm,),
        in_specs=[pl.BlockSpec((tm, k), lambda i: (i, 0)),
                  pl.BlockSpec((1, k), lambda i: (0, 0)),
                  pl.BlockSpec((k, width), lambda i: (0, 0)),
                  pl.BlockSpec((1, V7X_LANES), lambda i: (0, 0)),
                  tab_spec, tab_spec],
        out_specs=pl.BlockSpec((tm, width), lambda i: (i, 0)),
        out_shape=jax.ShapeDtypeStruct((b * n, width), BF16),
        compiler_params=_cparams(("parallel",)),
        name="mla_q",
    )(cq.reshape(b * n, k), cq_g.reshape(1, k), w_q, _lane_gain(qn_g), *tabs)
    return out.reshape(b, n, width)


def _mla_kv(ckv, small, ckv_g, w_kv, kn_g, rope, tm=1024):
    b, n, k = ckv.shape
    tm = min(tm, n)
    width = MLA_HEADS * V7X_LANES
    per_batch = n // tm
    use_rope = rope is not None
    tabs = rope if use_rope else (jnp.zeros((tm, V7X_LANES), F32),) * 2
    tab_spec = pl.BlockSpec((tm, V7X_LANES), (lambda i: (i % per_batch, 0)) if use_rope else (lambda i: (0, 0)))
    tok = pl.BlockSpec((tm, width), lambda i: (i, 0))
    kk, vv = pl.pallas_call(
        functools.partial(_mla_kv_kernel, use_rope=use_rope),
        grid=(b * n // tm,),
        in_specs=[pl.BlockSpec((tm, k), lambda i: (i, 0)),
                  pl.BlockSpec((1, k), lambda i: (0, 0)),
                  pl.BlockSpec((k, 2 * width), lambda i: (0, 0)),
                  pl.BlockSpec((tm, V7X_LANES), lambda i: (i, 0)),
                  pl.BlockSpec((1, V7X_LANES), lambda i: (0, 0)),
                  tab_spec, tab_spec],
        out_specs=[tok, tok],
        out_shape=[jax.ShapeDtypeStruct((b * n, width), BF16)] * 2,
        compiler_params=_cparams(("parallel",)),
        name="mla_kv",
    )(ckv.reshape(b * n, k), ckv_g.reshape(1, k), w_kv, small.reshape(b * n, V7X_LANES), _lane_gain(kn_g), *tabs)
    return kk.reshape(b, n, width), vv.reshape(b, n, width)


PROJ_OUTPUTS = (
    (NA_HEADS * V7X_LANES, NA_HEADS, 0, False, HEAD_DIM ** -0.5, None),
    (NA_HEADS * V7X_LANES, NA_HEADS, 1, False, 1.0, None),
    (NA_HEADS * V7X_LANES, NA_HEADS, None, False, 1.0, HEAD_DIM),
    (GQA_Q_HEADS * V7X_LANES, GQA_Q_HEADS, 2, True, HEAD_DIM ** -0.5, None),
    (GQA_KV_HEADS * V7X_LANES, GQA_KV_HEADS, 3, True, 1.0, None),
    (GQA_KV_HEADS * V7X_LANES, GQA_KV_HEADS, None, False, 1.0, HEAD_DIM),
    (3 * DN_W, 0, None, False, 1.0, None),
    (DN_W, 0, None, False, 1.0, None),
    (MLA_Q_RANK, 0, None, False, 1.0, None),
    (MLA_KV_RANK, 0, None, False, 1.0, None),
    (V7X_LANES, 0, None, False, 1.0, None),
)


def _in_proj_kernel(x_ref, g_ref, sh_ref, sc_ref, w_ref, gains_ref, rc_ref, rs_ref, *out_refs, use_rope):
    h = _rms(x_ref[0], g_ref[...]) * (1.0 + sc_ref[0]) + sh_ref[0]
    hb = h.astype(BF16)
    off = 0
    for o_ref, (wd, nheads, gain_row, rotary, scale, ones_lane) in zip(out_refs, PROJ_OUTPUTS):
        p = jnp.dot(hb, w_ref[:, off:off + wd], preferred_element_type=F32)
        off += wd
        if nheads == 0:
            o_ref[0] = p
            continue
        gain = None if gain_row is None else gains_ref[gain_row:gain_row + 1, :]
        rope = (rc_ref[...], rs_ref[...]) if (rotary and use_rope) else None
        for hh in range(nheads):
            sl = slice(hh * V7X_LANES, (hh + 1) * V7X_LANES)
            o_ref[0, :, sl] = _head_epilogue(p[:, sl], gain, HEAD_DIM, rope, scale, ones_lane).astype(o_ref.dtype)


def _in_proj(x, g, shift, scale, w_perm, gains, rope, tm):
    b, n, d = x.shape
    wtot = w_perm.shape[1]
    vec = pl.BlockSpec((1, 1, d), lambda i, j: (i, 0, 0))
    use_rope = rope is not None
    tabs = rope if use_rope else (jnp.zeros((tm, V7X_LANES), F32),) * 2
    tab_spec = pl.BlockSpec((tm, V7X_LANES), (lambda i, j: (j, 0)) if use_rope else (lambda i, j: (0, 0)))
    return pl.pallas_call(
        functools.partial(_in_proj_kernel, use_rope=use_rope),
        grid=(b, n // tm),
        in_specs=[pl.BlockSpec((1, tm, d), lambda i, j: (i, j, 0)),
                  pl.BlockSpec((1, d), lambda i, j: (0, 0)),
                  vec, vec,
                  pl.BlockSpec((d, wtot), lambda i, j: (0, 0)),
                  pl.BlockSpec(gains.shape, lambda i, j: (0, 0)),
                  tab_spec, tab_spec],
        out_specs=[pl.BlockSpec((1, tm, o[0]), lambda i, j: (i, j, 0)) for o in PROJ_OUTPUTS],
        out_shape=[jax.ShapeDtypeStruct((b, n, o[0]), F32 if o[1] == 0 else BF16) for o in PROJ_OUTPUTS],
        compiler_params=_cparams(("parallel", "parallel")),
        name="in_proj",
    )(x, g.reshape(1, d), shift.reshape(b, 1, d), scale.reshape(b, 1, d), w_perm, gains, *tabs)


def _attn_kernel(q_ref, k_ref, v_ref, o_ref, s_scr, *, tk, hq, hkv, dv):
    grp = hq // hkv
    tq = q_ref.shape[1]
    rows = grp * tq
    n_keys = k_ref.shape[1]
    nk = n_keys // tk
    nt = (((1,), (1,)), ((), ()))
    outs = [None] * hq
    for g in range(hkv):
        ks = slice(g * V7X_LANES, (g + 1) * V7X_LANES)
        q = jnp.concatenate([q_ref[0, :, (g * grp + r) * V7X_LANES:(g * grp + r + 1) * V7X_LANES] for r in range(grp)],
                            axis=0)
        mx = None
        for j in range(nk):
            s = lax.dot_general(q, k_ref[0, j * tk:(j + 1) * tk, ks], nt, preferred_element_type=F32)
            s_scr[:, j * tk:(j + 1) * tk] = s
            for t in range(tk // V7X_LANES):
                slab = s[:, t * V7X_LANES:(t + 1) * V7X_LANES]
                mx = slab if mx is None else jnp.maximum(mx, slab)
        mb = jnp.broadcast_to(jnp.max(mx, axis=-1, keepdims=True), (rows, V7X_LANES))
        acc = None
        for j in range(nk):
            slabs = [jnp.exp(s_scr[:, j * tk + t * V7X_LANES:j * tk + (t + 1) * V7X_LANES] - mb)
                     for t in range(tk // V7X_LANES)]
            p = jnp.concatenate(slabs, axis=-1).astype(BF16)
            part = jnp.dot(p, v_ref[0, j * tk:(j + 1) * tk, ks], preferred_element_type=F32)
            acc = part if acc is None else acc + part
        o = acc[:, :dv] / acc[:, dv:dv + 1]
        for r in range(grp):
            outs[g * grp + r] = o[r * tq:(r + 1) * tq]
    o_ref[0] = jnp.concatenate(outs, axis=-1).astype(o_ref.dtype)


def _attention(q, k, v1, hq, hkv, dv, tk=ATTN_TK):
    b, nq, _ = q.shape
    m = k.shape[1]
    grp = hq // hkv
    tq = min(ATTN_ROWS // grp, nq)
    tk = min(tk, m)
    return pl.pallas_call(
        functools.partial(_attn_kernel, tk=tk, hq=hq, hkv=hkv, dv=dv),
        grid=(b, nq // tq),
        in_specs=[pl.BlockSpec((1, tq, hq * V7X_LANES), lambda i, j: (i, j, 0)),
                  pl.BlockSpec((1, m, hkv * V7X_LANES), lambda i, j: (i, 0, 0)),
                  pl.BlockSpec((1, m, hkv * V7X_LANES), lambda i, j: (i, 0, 0))],
        out_specs=pl.BlockSpec((1, tq, hq * dv), lambda i, j: (i, j, 0)),
        out_shape=jax.ShapeDtypeStruct((b, nq, hq * dv), BF16),
        scratch_shapes=[pltpu.VMEM((grp * tq, m), F32)],
        compiler_params=_cparams(("parallel", "parallel")),
        name="attention",
    )(q, k, v1)


def _na_kernel(q_ref, k_ref, v_ref, kc_ref, vc_ref, bias_ref, o_ref, s_scr, *, rows):
    g = pl.program_id(1)
    w0 = jnp.clip(g * NA_GROUP - NA_WIN_R // 2, 0, rows - NA_KEY_ROWS)
    off = pl.multiple_of(w0 * GRID_W, GRID_W)
    nq = NA_GROUP * GRID_W
    nloc = NA_KEY_ROWS * GRID_W
    l = kc_ref.shape[1]
    tk = min(ATTN_TK, l)
    nt = (((1,), (1,)), ((), ()))
    outs = []
    for h in range(NA_HEADS):
        hs = slice(h * V7X_LANES, (h + 1) * V7X_LANES)
        qh = q_ref[0, :, hs]
        mx = None
        for j in range((nloc + l) // tk):
            if j * tk < nloc:
                kj = k_ref[0, pl.ds(off + j * tk, tk), hs]
                s = lax.dot_general(qh, kj, nt, preferred_element_type=F32) + bias_ref[0, h, :, j * tk:(j + 1) * tk]
            else:
                kj = kc_ref[0, j * tk - nloc:(j + 1) * tk - nloc, hs]
                s = lax.dot_general(qh, kj, nt, preferred_element_type=F32)
            s_scr[:, j * tk:(j + 1) * tk] = s
            for t in range(tk // V7X_LANES):
                slab = s[:, t * V7X_LANES:(t + 1) * V7X_LANES]
                mx = slab if mx is None else jnp.maximum(mx, slab)
        mb = jnp.broadcast_to(jnp.max(mx, axis=-1, keepdims=True), (nq, V7X_LANES))
        acc = None
        for j in range((nloc + l) // tk):
            slabs = [jnp.exp(s_scr[:, j * tk + t * V7X_LANES:j * tk + (t + 1) * V7X_LANES] - mb)
                     for t in range(tk // V7X_LANES)]
            p = jnp.concatenate(slabs, axis=-1).astype(BF16)
            if j * tk < nloc:
                vj = v_ref[0, pl.ds(off + j * tk, tk), hs]
            else:
                vj = vc_ref[0, j * tk - nloc:(j + 1) * tk - nloc, hs]
            part = jnp.dot(p, vj, preferred_element_type=F32)
            acc = part if acc is None else acc + part
        outs.append(acc[:, :HEAD_DIM] / acc[:, HEAD_DIM:HEAD_DIM + 1])
    o_ref[0] = jnp.concatenate(outs, axis=-1).astype(o_ref.dtype)


def _na_bias_table(rel_bias, rows):
    kr = NA_WIN_R
    cpos = np.arange(GRID_W)
    c0 = np.clip(cpos - NA_WIN_C // 2, 0, GRID_W - NA_WIN_C)
    col_ok = (cpos[None, :] >= c0[:, None]) & (cpos[None, :] < c0[:, None] + NA_WIN_C)
    dc = np.clip(cpos[None, :] - cpos[:, None], -(NA_WIN_C - 1), NA_WIN_C - 1) + (NA_WIN_C - 1)
    n_dr, n_dc = rel_bias.shape[1:]
    select = np.zeros((n_dc, GRID_W * GRID_W), np.float32)
    select[dc.ravel(), np.arange(GRID_W * GRID_W)] = 1.0
    base = jnp.dot(rel_bias.reshape(NA_HEADS * n_dr, n_dc), jnp.asarray(select), precision=HIGHEST)
    base = jnp.where(col_ok[None, None], base.reshape(NA_HEADS, n_dr, GRID_W, GRID_W), NEG_INF)
    ngroups = rows // NA_GROUP
    pick = np.zeros((3, NA_GROUP, NA_KEY_ROWS, n_dr + 1), np.float32)
    for var, g in enumerate((0, min(1, ngroups - 1), ngroups - 1)):
        w0 = int(np.clip(g * NA_GROUP - kr // 2, 0, rows - NA_KEY_ROWS))
        for qr in range(NA_GROUP):
            r = g * NA_GROUP + qr
            r0 = int(np.clip(r - kr // 2, 0, rows - kr))
            for kl in range(NA_KEY_ROWS):
                key_row = w0 + kl
                pick[var, qr, kl, key_row - r + kr - 1 if r0 <= key_row < r0 + kr else n_dr] = 1.0
    base_ext = jnp.concatenate([jnp.transpose(base, (1, 0, 2, 3)).reshape(n_dr, -1),
                                jnp.full((1, NA_HEADS * GRID_W * GRID_W), NEG_INF, F32)], axis=0)
    tab = jnp.dot(jnp.asarray(pick.reshape(-1, n_dr + 1)), base_ext, precision=HIGHEST)
    tab = tab.reshape(3, NA_GROUP, NA_KEY_ROWS, NA_HEADS, GRID_W, GRID_W)
    return jnp.transpose(tab, (0, 3, 1, 4, 2, 5)).reshape(3, NA_HEADS, NA_GROUP * GRID_W, NA_KEY_ROWS * GRID_W)


def _na_attention(q, k, v, kc, vc, bias_tab):
    b, n, w = q.shape
    l = kc.shape[1]
    rows = n // GRID_W
    assert rows % NA_GROUP == 0 and rows >= NA_KEY_ROWS
    ngroups = rows // NA_GROUP
    nq = NA_GROUP * GRID_W

    def bias_map(i, g):
        return (jnp.where(g == 0, 0, jnp.where(g == ngroups - 1, 2, 1)), 0, 0, 0)

    return pl.pallas_call(
        functools.partial(_na_kernel, rows=rows),
        grid=(b, ngroups),
        in_specs=[pl.BlockSpec((1, nq, w), lambda i, g: (i, g, 0)),
                  pl.BlockSpec((1, n, w), lambda i, g: (i, 0, 0)),
                  pl.BlockSpec((1, n, w), lambda i, g: (i, 0, 0)),
                  pl.BlockSpec((1, l, w), lambda i, g: (i, 0, 0)),
                  pl.BlockSpec((1, l, w), lambda i, g: (i, 0, 0)),
                  pl.BlockSpec((1, NA_HEADS, nq, NA_KEY_ROWS * GRID_W), bias_map)],
        out_specs=pl.BlockSpec((1, nq, NA_W), lambda i, g: (i, g, 0)),
        out_shape=jax.ShapeDtypeStruct((b, n, NA_W), BF16),
        scratch_shapes=[pltpu.VMEM((nq, NA_KEY_ROWS * GRID_W + l), F32)],
        compiler_params=_cparams(("parallel", "arbitrary")),
        name="na_attention",
    )(q, k, v, kc, vc, bias_tab)


DN_LEVELS = int(math.log2(DN_CHUNK))


def _dn_constants():
    c, w = DN_CHUNK, DN_W
    ii = np.arange(c)[:, None]
    jj = (np.arange(w) % c)[None, :]
    hh = (np.arange(w) // HEAD_DIM)[None, :]
    tri, expand, neg, strict, lvl = [], [], [], [], []
    for d in range(2):
        rev = d == 1
        incl = (ii <= jj) if rev else (ii >= jj)
        neg.append(np.where(incl, 0.0, NEG_INF))
        strict.append((ii < jj) if rev else (ii > jj))
        t = np.arange(c)
        tri.append((t[:, None] <= t[None, :]) if rev else (t[:, None] >= t[None, :]))
        expand.append(np.arange(V7X_LANES)[:, None] == hh + d * DN_HEADS)
        for lv in range(DN_LEVELS):
            same = (ii >> (lv + 1)) == (jj >> (lv + 1))
            hi_i = ((ii >> lv) & 1) == 1
            hi_j = ((jj >> lv) & 1) == 1
            lvl.append((same & ~hi_i & hi_j) if rev else (same & hi_i & ~hi_j))
    r2 = np.arange(w)
    block = (r2[:, None] // HEAD_DIM) == (r2[None, :] // HEAD_DIM)
    f = lambda a, dt: jnp.asarray(np.stack(a).astype(np.float32), dtype=dt)
    return dict(tri=f(tri, F32), expand=f(expand, F32), neg=f(neg, F32), strict=f(strict, F32),
                lvl=f(lvl, BF16).reshape(2, DN_LEVELS, c, w), eye=jnp.asarray((ii == jj).astype(np.float32)),
                block16=jnp.asarray(block.astype(np.float32), dtype=BF16), block32=jnp.asarray(block.astype(np.float32)))


def _dn_block_diag(y16, block16):
    return jnp.concatenate([y16] * DN_HEADS, axis=0) * block16


def _dn_prep_kernel(q_ref, k_ref, v_ref, g_ref, b_ref, tri_ref, exp_ref, neg_ref, strict_ref, lvl_ref, eye_ref,
                    blk_ref, *out_refs, chunks):
    c = DN_CHUNK
    nt = (((1,), (1,)), ((), ()))

    def bdmm(x, y):
        return jnp.dot(x.astype(BF16), _dn_block_diag(y.astype(BF16), blk_ref[...]), preferred_element_type=F32)

    def chunk_group(gi, carry):
        probs = []
        for sub in range(DN_PREP_INTERLEAVE):
            ci = gi * DN_PREP_INTERLEAVE + sub
            rows = pl.ds(pl.multiple_of(ci * c, c), c)
            q = q_ref[0, rows, :] * (HEAD_DIM ** -0.5)
            k = k_ref[0, rows, :]
            v = v_ref[0, rows, :]
            gcol = g_ref[0, rows, :]
            bcol = b_ref[0, rows, :]
            kbd = _dn_block_diag(k.astype(BF16), blk_ref[...])
            kk = lax.dot_general(k.astype(BF16), kbd, nt, preferred_element_type=F32)
            qk = lax.dot_general(q.astype(BF16), kbd, nt, preferred_element_type=F32)
            for d in range(2):
                gc = jnp.dot(tri_ref[d], gcol, preferred_element_type=F32, precision=HIGHEST)
                gcx = jnp.dot(gc, exp_ref[d], preferred_element_type=F32, precision=HIGHEST)
                bx = jnp.dot(bcol, exp_ref[d], preferred_element_type=F32, precision=HIGHEST)
                gr = jnp.sum(gcx * eye_ref[...], axis=0, keepdims=True)
                decay = jnp.exp(gcx - gr + neg_ref[d])
                a16 = (kk * bx * decay * strict_ref[d]).astype(BF16)
                probs.append(dict(d=d, ci=ci, rows=rows, q=q, k=k, v=v, qk=qk, gcx=gcx, bx=bx, decay=decay, a16=a16,
                                  t=eye_ref[...]))
        for lv in range(DN_LEVELS):
            for p in probs:
                p["x"] = jnp.dot(p["t"].astype(BF16), _dn_block_diag(p["a16"] * lvl_ref[p["d"], lv], blk_ref[...]),
                                 preferred_element_type=F32)
            for p in probs:
                p["t"] = p["t"] - bdmm(p["x"], p["t"])
        for p in probs:
            d, rows, gcx = p["d"], p["rows"], p["gcx"]
            u_ref, w_ref, a_ref, qg_ref, kd_ref, eg_ref = out_refs[6 * d:6 * d + 6]
            last = 0 if d == 1 else c - 1
            e_gc = jnp.exp(gcx)
            g_last = gcx[last:last + 1, :]
            u_ref[0, rows, :] = bdmm(p["t"], p["v"] * p["bx"]).astype(u_ref.dtype)
            w_ref[0, rows, :] = bdmm(p["t"], p["k"] * p["bx"] * e_gc).astype(w_ref.dtype)
            a_ref[0, rows, :] = (p["qk"] * p["decay"]).astype(a_ref.dtype)
            qg_ref[0, rows, :] = (p["q"] * e_gc).astype(qg_ref.dtype)
            kd_ref[0, rows, :] = (p["k"] * jnp.exp(g_last - gcx)).astype(kd_ref.dtype)
            eg_ref[0, p["ci"]] = jnp.exp(g_last)
        return carry

    lax.fori_loop(0, chunks // DN_PREP_INTERLEAVE, chunk_group, 0)


def _dn_scan_kernel(*refs, chunks):
    ins = (refs[0:6], refs[6:12])
    s0_refs = refs[12:14]
    b16_ref, b32_ref = refs[14:16]
    o_refs = refs[16:18]
    sf_refs = refs[18:20]
    s_scr = refs[20]
    c = DN_CHUNK
    blk = pl.program_id(1)

    @pl.when(blk == 0)
    def _():
        s_scr[0] = s0_refs[0][0]
        s_scr[1] = s0_refs[1][0]

    def chunk(ci, carry):
        st = []
        for d in range(2):
            idx = (chunks - 1 - ci) if d == 1 else ci
            st.append(dict(idx=idx, rows=pl.ds(pl.multiple_of(idx * c, c), c), s=s_scr[d]))
        for d, p in enumerate(st):
            u_ref, w_ref, a_ref, qg_ref, kd_ref, eg_ref = ins[d]
            wq = jnp.concatenate([w_ref[0, p["rows"], :], qg_ref[0, p["rows"], :]], axis=0)
            p["r"] = jnp.dot(wq, p["s"].astype(BF16), preferred_element_type=F32)
        for d, p in enumerate(st):
            u_ref = ins[d][0]
            p["vn"] = (u_ref[0, p["rows"], :].astype(F32) - p["r"][:c]).astype(BF16)
        for d, p in enumerate(st):
            kd_ref, eg_ref = ins[d][4], ins[d][5]
            kv = lax.dot_general(kd_ref[0, p["rows"], :], p["vn"], (((0,), (0,)), ((), ())), preferred_element_type=F32)
            s_scr[d] = p["s"] * eg_ref[0, p["idx"]] + kv * b32_ref[...]
        for d, p in enumerate(st):
            a_ref = ins[d][2]
            o_refs[d][0, p["rows"], :] = p["r"][c:] + jnp.dot(a_ref[0, p["rows"], :],
                                                              _dn_block_diag(p["vn"], b16_ref[...]),
                                                              preferred_element_type=F32)
        return carry

    lax.fori_loop(0, chunks, chunk, 0)

    @pl.when(blk == pl.num_programs(1) - 1)
    def _():
        sf_refs[0][0] = s_scr[0]
        sf_refs[1][0] = s_scr[1]


def _delta_bidir(q, k, v, g, beta, s0_f, s0_b):
    b, n, w = q.shape
    nchunks = n // DN_CHUNK
    chunks = min(DN_BLOCK_CHUNKS, nchunks)
    nblk = nchunks // chunks
    bt = chunks * DN_CHUNK

    tok = pl.BlockSpec((1, bt, w), lambda i, j: (i, j, 0))
    small = pl.BlockSpec((1, bt, V7X_LANES), lambda i, j: (i, j, 0))
    egl = pl.BlockSpec((1, chunks, 1, w), lambda i, j: (i, j, 0, 0))
    per_dir_shapes = [jax.ShapeDtypeStruct((b, n, w), BF16)] * 5 + [jax.ShapeDtypeStruct((b, nchunks, 1, w), F32)]
    cst = _dn_constants()

    def const_spec(a):
        nd = a.ndim
        return pl.BlockSpec(a.shape, lambda i, j: (0,) * nd)

    prep_consts = [cst[name] for name in ("tri", "expand", "neg", "strict", "lvl", "eye", "block16")]
    prep = pl.pallas_call(
        functools.partial(_dn_prep_kernel, chunks=chunks),
        grid=(b, nblk),
        in_specs=[tok, tok, tok, small, small] + [const_spec(a) for a in prep_consts],
        out_specs=([tok] * 5 + [egl]) * 2,
        out_shape=per_dir_shapes * 2,
        compiler_params=_cparams(("parallel", "parallel")),
        name="delta_prep",
    )(q, k, v, g, beta, *prep_consts)

    tok_r = pl.BlockSpec((1, bt, w), lambda i, j: (i, nblk - 1 - j, 0))
    egl_r = pl.BlockSpec((1, chunks, 1, w), lambda i, j: (i, nblk - 1 - j, 0, 0))
    state = pl.BlockSpec((1, w, w), lambda i, j: (i, 0, 0))
    scan_consts = [cst["block16"], cst["block32"]]
    o_f, o_b, s_f, s_b = pl.pallas_call(
        functools.partial(_dn_scan_kernel, chunks=chunks),
        grid=(b, nblk),
        in_specs=[tok] * 5 + [egl] + [tok_r] * 5 + [egl_r] + [state, state] + [const_spec(a) for a in scan_consts],
        out_specs=[tok, tok_r, state, state],
        out_shape=[jax.ShapeDtypeStruct((b, n, w), F32)] * 2 + [jax.ShapeDtypeStruct((b, w, w), F32)] * 2,
        scratch_shapes=[pltpu.VMEM((2, w, w), F32)],
        compiler_params=_cparams(("parallel", "arbitrary")),
        name="delta_scan",
    )(*prep, s0_f, s0_b, *scan_consts)
    return o_f, o_b, s_f, s_b


def _out_proj_kernel(x_ref, ya_ref, yb_ref, of_ref, ob_ref, dgate_ref, yd_ref, hm_ref, dg_ref, w_ref, gate_ref, g2_ref,
                     sh_ref, sc_ref, rw_ref, rb_ref, xo_ref, h2_ref, idx_ref, gt_ref):
    od = of_ref[0] + ob_ref[0]
    ms = jnp.dot(od * od, hm_ref[...], preferred_element_type=F32, precision=HIGHEST)
    yc = (od * lax.rsqrt(ms + EPS) * dg_ref[...] * jax.nn.silu(dgate_ref[0])).astype(BF16)
    acc = None
    for i, y in enumerate((ya_ref[0], yb_ref[0], yc, yd_ref[0])):
        wd = y.shape[-1]
        part = jnp.dot(y, w_ref[i * wd:(i + 1) * wd, :], preferred_element_type=F32)
        acc = part if acc is None else acc + part
    xn = x_ref[0] + gate_ref[0] * acc
    xo_ref[0] = xn
    h2 = _rms(xn, g2_ref[...]) * (1.0 + sc_ref[0]) + sh_ref[0]
    h2_ref[0] = h2
    logits = jnp.dot(h2, rw_ref[...], preferred_element_type=F32, precision=HIGHEST) + rb_ref[...]
    lane = lax.broadcasted_iota(jnp.int32, logits.shape, 1).astype(F32)
    vals, idxs = [], []
    cur = logits
    for _ in range(TOP_K):
        mx = jnp.max(cur, axis=-1, keepdims=True)
        ik = jnp.min(jnp.where(cur == mx, lane, float(V7X_LANES)), axis=-1, keepdims=True)
        vals.append(mx)
        idxs.append(ik)
        cur = jnp.where(lane == ik, -jnp.inf, cur)
    es = [jnp.exp(vv - vals[0]) for vv in vals]
    den = es[0] + es[1] + es[2] + es[3]
    idx_out = jnp.zeros(logits.shape, F32)
    gate_out = jnp.zeros(logits.shape, F32)
    for kk in range(TOP_K):
        idx_out = jnp.where(lane == float(kk), idxs[kk], idx_out)
        gate_out = jnp.where(lane == float(kk), es[kk] / den, gate_out)
    idx_ref[0] = idx_out.astype(jnp.int32)
    gt_ref[0] = gate_out


def _out_proj(x, ya, yb, o_f, o_b, dn_gate, dn_out_g, yd, w_out, gate, g2, shift, scale, rw_pad, rb_pad, tm):
    b, n, d = x.shape
    yw = ya.shape[-1]
    vec = pl.BlockSpec((1, 1, d), lambda i, j: (i, 0, 0))
    tok = pl.BlockSpec((1, tm, d), lambda i, j: (i, j, 0))
    ytok = pl.BlockSpec((1, tm, yw), lambda i, j: (i, j, 0))
    ltok = pl.BlockSpec((1, tm, V7X_LANES), lambda i, j: (i, j, 0))
    head = np.arange(yw) // HEAD_DIM
    head_mean = jnp.asarray((head[:, None] == head[None, :]).astype(np.float32) / HEAD_DIM)
    return pl.pallas_call(
        _out_proj_kernel,
        grid=(b, n // tm),
        in_specs=[tok, ytok, ytok, ytok, ytok, ytok, ytok,
                  pl.BlockSpec((yw, yw), lambda i, j: (0, 0)),
                  pl.BlockSpec((1, yw), lambda i, j: (0, 0)),
                  pl.BlockSpec((4 * yw, d), lambda i, j: (0, 0)),
                  vec,
                  pl.BlockSpec((1, d), lambda i, j: (0, 0)),
                  vec, vec,
                  pl.BlockSpec((d, V7X_LANES), lambda i, j: (0, 0)),
                  pl.BlockSpec((1, V7X_LANES), lambda i, j: (0, 0))],
        out_specs=[tok, tok, ltok, ltok],
        out_shape=[jax.ShapeDtypeStruct((b, n, d), F32), jax.ShapeDtypeStruct((b, n, d), F32),
                   jax.ShapeDtypeStruct((b, n, V7X_LANES), jnp.int32), jax.ShapeDtypeStruct((b, n, V7X_LANES), F32)],
        compiler_params=_cparams(("parallel", "parallel")),
        name="out_proj",
    )(x, ya, yb, o_f, o_b, dn_gate, yd, head_mean, jnp.tile(dn_out_g, DN_HEADS).reshape(1, yw), w_out,
      gate.reshape(b, 1, d), g2.reshape(1, d), shift.reshape(b, 1, d), scale.reshape(b, 1, d), rw_pad, rb_pad)


def _experts_kernel(te_ref, tf_ref, tv_ref, src0_ref, srcn_ref, h_hbm, w1_ref, b1_ref, w2_ref, b2_ref, o_ref,
                    xbuf, w1b, w2b, sem, *, fc):
    i = pl.program_id(0)
    nt = pl.num_programs(0)
    tm = o_ref.shape[0]
    f = w2_ref.shape[1]
    slot = i % 2

    def row_copy(src_ref, r, dst_slot):
        return pltpu.make_async_copy(h_hbm.at[pl.ds(src_ref[0, 0, r], 1)], xbuf.at[dst_slot, pl.ds(r, 1)],
                                     sem.at[dst_slot])

    def issue(src_ref, dst_slot):
        def body(r8, carry):
            for u in range(MOE_ISSUE_UNROLL):
                row_copy(src_ref, r8 * MOE_ISSUE_UNROLL + u, dst_slot).start()
            return carry
        lax.fori_loop(0, tm // MOE_ISSUE_UNROLL, body, 0)

    def wait_tile(s):
        pltpu.make_async_copy(h_hbm.at[pl.ds(0, tm)], xbuf.at[s], sem.at[s]).wait()

    @pl.when(jnp.logical_and(i == 0, tv_ref[0] != 0))
    def _():
        issue(src0_ref, 0)

    @pl.when(tv_ref[i] == 0)
    def _():
        @pl.when(jnp.logical_and(i > 0, tv_ref[jnp.maximum(i - 1, 0)] != 0))
        def _():
            wait_tile(slot)
        o_ref[...] = jnp.zeros(o_ref.shape, o_ref.dtype)

    @pl.when(tv_ref[i] != 0)
    def _():
        issue(srcn_ref, 1 - slot)

        @pl.when(tf_ref[i] != 0)
        def _():
            w1b[...] = w1_ref[0].astype(BF16)
            w2b[...] = w2_ref[0].astype(BF16)

        wait_tile(slot)
        xb = xbuf[slot].astype(BF16)
        acc = None
        for j in range(f // fc):
            glu = jnp.dot(xb, w1b[:, j * fc:(j + 1) * fc], preferred_element_type=F32) + b1_ref[0, :, j * fc:(j + 1) * fc]
            lin = (jnp.dot(xb, w1b[:, f + j * fc:f + (j + 1) * fc], preferred_element_type=F32)
                   + b1_ref[0, :, f + j * fc:f + (j + 1) * fc])
            glu = jnp.minimum(glu, SWIGLU_LIMIT)
            lin = jnp.clip(lin, -SWIGLU_LIMIT, SWIGLU_LIMIT)
            act = glu * jax.nn.sigmoid(SWIGLU_ALPHA * glu) * (lin + 1.0)
            part = jnp.dot(act.astype(BF16), w2b[j * fc:(j + 1) * fc, :], preferred_element_type=F32)
            acc = part if acc is None else acc + part
        o_ref[...] = acc + b2_ref[0]

        @pl.when(i == nt - 1)
        def _():
            wait_tile(1 - slot)


def _expert_tiles(h, src, tile_e, tile_first, tile_valid, w1, b1, w2, b2, tm):
    t, d = h.shape
    r = src.shape[0]
    e, _, f2 = w1.shape
    f = f2 // 2
    nt = r // tm
    grid_spec = pltpu.PrefetchScalarGridSpec(
        num_scalar_prefetch=3,
        grid=(nt,),
        in_specs=[pl.BlockSpec((1, 1, tm), lambda i, te, tf, tv: (0, 0, 0), memory_space=pltpu.SMEM),
                  pl.BlockSpec((1, 1, tm), lambda i, te, tf, tv: (i + 1, 0, 0), memory_space=pltpu.SMEM),
                  pl.BlockSpec(memory_space=pl.ANY),
                  pl.BlockSpec((1, d, f2), lambda i, te, tf, tv: (te[i], 0, 0)),
                  pl.BlockSpec((1, 1, f2), lambda i, te, tf, tv: (te[i], 0, 0)),
                  pl.BlockSpec((1, f, d), lambda i, te, tf, tv: (te[i], 0, 0)),
                  pl.BlockSpec((1, 1, d), lambda i, te, tf, tv: (te[i], 0, 0))],
        out_specs=pl.BlockSpec((tm, d), lambda i, te, tf, tv: (i, 0)),
        scratch_shapes=[pltpu.VMEM((2, tm, d), F32), pltpu.VMEM((d, f2), BF16), pltpu.VMEM((f, d), BF16),
                        pltpu.SemaphoreType.DMA((2,))],
    )
    src3 = jnp.concatenate([src, jnp.zeros((tm,), src.dtype)]).reshape(nt + 1, 1, tm)
    return pl.pallas_call(
        functools.partial(_experts_kernel, fc=MOE_FC),
        grid_spec=grid_spec,
        out_shape=jax.ShapeDtypeStruct((r, d), F32),
        compiler_params=_cparams(("arbitrary",)),
        name="moe_experts",
    )(tile_e, tile_first, tile_valid, src3, src3, h, w1, b1.reshape(e, 1, f2), w2, b2.reshape(e, 1, d))


def _combine_kernel(pos_ref, ys_hbm, gate_ref, x_ref, mod_ref, o_ref, buf, sem):
    ct = o_ref.shape[0]

    def issue(r2, carry):
        for u in range(COMBINE_ISSUE_UNROLL):
            r = r2 * COMBINE_ISSUE_UNROLL + u
            for kk in range(TOP_K):
                p = pos_ref[0, 0, r * TOP_K + kk]
                pltpu.make_async_copy(ys_hbm.at[pl.ds(p, 1)], buf.at[kk, pl.ds(r, 1)], sem).start()
        return carry

    lax.fori_loop(0, ct // COMBINE_ISSUE_UNROLL, issue, 0)
    for kk in range(TOP_K):
        pltpu.make_async_copy(ys_hbm.at[pl.ds(0, ct)], buf.at[kk], sem).wait()
    g = gate_ref[...]
    mix = (g[:, 0:1] * buf[0] + g[:, 1:2] * buf[1]) + (g[:, 2:3] * buf[2] + g[:, 3:4] * buf[3])
    o_ref[...] = x_ref[...] + mod_ref[0] * mix


def _combine_rows(ys, pos, gates, x, mod, ct):
    b, n, d = x.shape
    t = b * n
    ct = min(ct, n)
    nsteps = t // ct
    per_batch = n // ct
    out = pl.pallas_call(
        _combine_kernel,
        grid=(nsteps,),
        in_specs=[pl.BlockSpec((1, 1, ct * TOP_K), lambda i: (i, 0, 0), memory_space=pltpu.SMEM),
                  pl.BlockSpec(memory_space=pl.ANY),
                  pl.BlockSpec((ct, V7X_LANES), lambda i: (i, 0)),
                  pl.BlockSpec((ct, d), lambda i: (i, 0)),
                  pl.BlockSpec((1, 1, d), lambda i: (i // per_batch, 0, 0))],
        out_specs=pl.BlockSpec((ct, d), lambda i: (i, 0)),
        out_shape=jax.ShapeDtypeStruct((t, d), F32),
        scratch_shapes=[pltpu.VMEM((TOP_K, ct, d), F32), pltpu.SemaphoreType.DMA],
        compiler_params=_cparams(("arbitrary",)),
        name="moe_combine",
    )(pos.reshape(nsteps, 1, ct * TOP_K), ys, gates, x.reshape(t, d), mod.reshape(b, 1, d))
    return out.reshape(b, n, d)


def _route_plan(idx4, tm):
    t = idx4.shape[0]
    e = N_EXPERTS
    r_max = t * TOP_K + e * tm
    nt = r_max // tm
    onehot = (idx4[:, :, None] == jnp.arange(e, dtype=jnp.int32)[None, None, :]).astype(jnp.int32)
    member = jnp.sum(onehot, axis=1)
    csum = jnp.cumsum(member, axis=0)
    cnt = csum[-1]
    excl = csum - member
    cnt_pad = ((cnt + tm - 1) // tm) * tm
    ends = jnp.cumsum(cnt_pad)
    base = ends - cnt_pad
    pos = jnp.sum(onehot * (excl + base[None, :])[:, None, :], axis=-1)
    flat = pos.reshape(-1)
    tok = jnp.repeat(jnp.arange(t, dtype=jnp.int32), TOP_K)
    src = jnp.zeros((r_max,), jnp.int32).at[flat].set(tok, unique_indices=True)
    tile_start = jnp.arange(nt, dtype=jnp.int32) * tm
    tile_e = jnp.minimum(jnp.sum((ends[None, :] <= tile_start[:, None]).astype(jnp.int32), axis=1), e - 1)
    tile_valid = (tile_start < ends[-1]).astype(jnp.int32)
    tile_first = jnp.concatenate([jnp.ones((1,), jnp.int32), (tile_e[1:] != tile_e[:-1]).astype(jnp.int32)])
    return pos.astype(jnp.int32), src, tile_e, tile_first, tile_valid


def _moe_residual(h2, idx4, gates, w1, b1, w2, b2, streams):
    pos, src, tile_e, tile_first, tile_valid = _route_plan(idx4, MOE_TM)
    ys = _expert_tiles(h2, src, tile_e, tile_first, tile_valid, w1, b1, w2, b2, MOE_TM)
    outs, start = [], 0
    for x, mod in streams:
        cnt = x.shape[0] * x.shape[1]
        outs.append(_combine_rows(ys, pos[start:start + cnt], gates[start:start + cnt], x, mod, COMBINE_ROWS))
        start += cnt
    return outs


def _axial_rope_tables(n_tok, rot_dim):
    t = jnp.arange(n_tok, dtype=jnp.int32)
    row = (t // GRID_W).astype(F32)
    col = (t % GRID_W).astype(F32)
    n_freq = rot_dim // 4
    freqs = ROPE_THETA ** (-jnp.arange(n_freq, dtype=F32) / n_freq)
    ang = jnp.concatenate([row[:, None] * freqs, col[:, None] * freqs], axis=-1)
    return jnp.cos(ang), jnp.sin(ang)


def _rope_lane_tables(cos, sin, start):
    n, nf = cos.shape
    c2 = jnp.repeat(cos, 2, axis=1)
    s2 = jnp.stack([-sin, sin], axis=-1).reshape(n, 2 * nf)
    tail = V7X_LANES - start - 2 * nf
    c = jnp.concatenate([jnp.ones((n, start), F32), c2, jnp.ones((n, tail), F32)], axis=1)
    s = jnp.concatenate([jnp.zeros((n, start), F32), s2, jnp.zeros((n, tail), F32)], axis=1)
    return c, s


def _heads(t, n):
    return t.reshape(t.shape[0], t.shape[1], n, t.shape[2] // n)


def _permute_w_in(w_in):
    offs = (0,) + IN_OFFSETS

    def seg(i):
        return np.arange(offs[i], offs[i] + IN_SIZES[i])

    def head_groups(i, nheads):
        cols = seg(i).reshape(nheads, HEAD_DIM)
        return np.concatenate([cols, -np.ones((nheads, V7X_LANES - HEAD_DIM), np.int64)], axis=1).reshape(-1)

    cols = np.concatenate([head_groups(0, NA_HEADS), head_groups(1, NA_HEADS), head_groups(2, NA_HEADS),
                           head_groups(3, GQA_Q_HEADS), head_groups(4, GQA_KV_HEADS), head_groups(5, GQA_KV_HEADS),
                           seg(6), seg(7), seg(8), seg(9), seg(12), seg(13), seg(14), seg(10), seg(11)])
    cols = np.concatenate([cols, -np.ones(sum(o[0] for o in PROJ_OUTPUTS) - cols.size, np.int64)])
    w = jnp.take(w_in, jnp.asarray(np.maximum(cols, 0)), axis=1)
    return jnp.where(jnp.asarray(cols >= 0)[None, :], w, 0.0).astype(BF16)


def _dn_prep(qkv, small, conv_w, a_log, dt_bias):
    b, n, _ = qkv.shape
    pad = DN_CONV // 2
    xp = jnp.pad(qkv, ((0, 0), (pad, pad), (0, 0)))
    conv = sum(xp[:, i:i + n, :] * conv_w[i][None, None, :] for i in range(DN_CONV))
    act = jax.nn.silu(conv)
    q, k, v = jnp.split(act, 3, axis=-1)

    def l2(t):
        th = _heads(t, DN_HEADS)
        return (th * lax.rsqrt(jnp.sum(th * th, axis=-1, keepdims=True) + EPS)).reshape(b, n, DN_W)

    beta_raw = small[..., MLA_ROPE:MLA_ROPE + 2 * DN_HEADS]
    a_raw = small[..., MLA_ROPE + 2 * DN_HEADS:MLA_ROPE + 4 * DN_HEADS]
    beta = jax.nn.sigmoid(beta_raw)
    g = -jnp.exp(a_log.reshape(-1)) * jax.nn.softplus(a_raw + dt_bias.reshape(-1))
    lane_pad = ((0, 0), (0, 0), (0, V7X_LANES - 2 * DN_HEADS))
    return l2(q), l2(k), v, jnp.pad(g, lane_pad), jnp.pad(beta, lane_pad)


def _mla_weights(w_uq, w_ukv):
    wq = jnp.pad(w_uq.reshape(MLA_Q_RANK, MLA_HEADS, MLA_QK), ((0, 0), (0, 0), (0, V7X_LANES - MLA_QK)))
    wkv = w_ukv.reshape(MLA_KV_RANK, MLA_HEADS, MLA_NOPE + MLA_V)
    wk = jnp.pad(wkv[..., :MLA_NOPE], ((0, 0), (0, 0), (0, V7X_LANES - MLA_NOPE)))
    wv = jnp.pad(wkv[..., MLA_NOPE:], ((0, 0), (0, 0), (0, V7X_LANES - MLA_V)))
    width = MLA_HEADS * V7X_LANES
    w_kv = jnp.concatenate([wk.reshape(MLA_KV_RANK, width), wv.reshape(MLA_KV_RANK, width)], axis=1)
    return wq.reshape(MLA_Q_RANK, width).astype(BF16), w_kv.astype(BF16)


def kernel(x, c, ctx, c_ctx, ada_w, ada_b, norm1_g, norm2_g, w_in, w_out, na_qn_g, na_kn_g, na_rel_bias, gqa_qn_g, gqa_kn_g, dn_conv_w, dn_a_log, dn_dt_bias, dn_out_g, mla_cq_g, mla_ckv_g, mla_w_uq, mla_w_ukv, mla_qn_g, mla_kn_g, router_w, router_b, exp_w1, exp_b1, exp_w2, exp_b2):
    b, n, d = x.shape
    l = ctx.shape[1]
    depth = ada_w.shape[0]
    rope_g = _rope_lane_tables(*_axial_rope_tables(n, HEAD_DIM), 0)
    rope_m = _rope_lane_tables(*_axial_rope_tables(n, MLA_ROPE), MLA_NOPE)
    cond = jnp.concatenate([jax.nn.silu(c), jax.nn.silu(c_ctx)[None], jnp.zeros((16 - b - 1, d), F32)], axis=0)
    s0 = jnp.zeros((b, DN_W, DN_W), F32)

    for ly in range(depth):
        with_ctx = ly < depth - 1
        mod_all = _mm_bias(cond, ada_w[ly], ada_b[ly], 1024)
        mod = jnp.split(mod_all[:b], N_ADA, axis=-1)
        mod_c = [jnp.broadcast_to(m_, (b, d)) for m_ in jnp.split(mod_all[b:b + 1], N_ADA, axis=-1)]
        w_perm = _permute_w_in(w_in[ly])
        gains = jnp.concatenate([_lane_gain(g_) for g_ in (na_qn_g[ly], na_kn_g[ly], gqa_qn_g[ly], gqa_kn_g[ly])]
                                + [jnp.zeros((4, V7X_LANES), F32)], axis=0)
        p = _in_proj(x, norm1_g[ly], mod[0], mod[1], w_perm, gains, rope_g, PROJ_TM)
        pc = _in_proj(ctx, norm1_g[ly], mod_c[0], mod_c[1], w_perm, gains, None, l)
        bias_tab = _na_bias_table(na_rel_bias[ly], n // GRID_W)

        ya = _na_attention(p[0], p[1], p[2], pc[1], pc[2], bias_tab)

        yb = _attention(p[3], jnp.concatenate([p[4], pc[4]], axis=1), jnp.concatenate([p[5], pc[5]], axis=1),
                        GQA_Q_HEADS, GQA_KV_HEADS, HEAD_DIM)

        qdc, kdc, vdc, gdc, bdc = _dn_prep(pc[6], pc[10], dn_conv_w[ly], dn_a_log[ly], dn_dt_bias[ly])
        odc_f, odc_b, s_f, s_b = _delta_bidir(qdc, kdc, vdc, gdc, bdc, s0, s0)
        qd, kd, vd, gd, bd = _dn_prep(p[6], p[10], dn_conv_w[ly], dn_a_log[ly], dn_dt_bias[ly])
        od_f, od_b, _, _ = _delta_bidir(qd, kd, vd, gd, bd, s_f, s_b)

        w_q, w_kv = _mla_weights(mla_w_uq[ly], mla_w_ukv[ly])
        qm = _mla_q(p[8], mla_cq_g[ly], w_q, mla_qn_g[ly], rope_m)
        km, vm = _mla_kv(p[9], p[10], mla_ckv_g[ly], w_kv, mla_kn_g[ly], rope_m)
        kmc, vmc = _mla_kv(pc[9], pc[10], mla_ckv_g[ly], w_kv, mla_kn_g[ly], None)
        yd = _attention(qm, jnp.concatenate([km, kmc], axis=1), jnp.concatenate([vm, vmc], axis=1),
                        MLA_HEADS, MLA_HEADS, MLA_V)

        rw_pad = jnp.pad(router_w[ly], ((0, 0), (0, V7X_LANES - N_EXPERTS)))
        rb_pad = jnp.pad(router_b[ly], (0, V7X_LANES - N_EXPERTS), constant_values=NEG_INF).reshape(1, V7X_LANES)
        w_out_b = w_out[ly].astype(BF16)
        x, h2, idx, gts = _out_proj(x, ya, yb, od_f, od_b, p[7], dn_out_g[ly], yd, w_out_b, mod[2], norm2_g[ly],
                                    mod[3], mod[4], rw_pad, rb_pad, PROJ_TM)
        tok = h2.reshape(b * n, d)
        idx4 = idx.reshape(b * n, V7X_LANES)[:, :TOP_K]
        gates = gts.reshape(b * n, V7X_LANES)

        if with_ctx:
            yac = _attention(pc[0], pc[1], pc[2], NA_HEADS, NA_HEADS, HEAD_DIM)
            ybc = _attention(pc[3], pc[4], pc[5], GQA_Q_HEADS, GQA_KV_HEADS, HEAD_DIM)
            qmc = _mla_q(pc[8], mla_cq_g[ly], w_q, mla_qn_g[ly], None)
            ydc = _attention(qmc, kmc, vmc, MLA_HEADS, MLA_HEADS, MLA_V)
            ctx, h2c, idxc, gtsc = _out_proj(ctx, yac, ybc, odc_f, odc_b, pc[7], dn_out_g[ly], ydc, w_out_b, mod_c[2],
                                             norm2_g[ly], mod_c[3], mod_c[4], rw_pad, rb_pad, l)
            tok = jnp.concatenate([tok, h2c.reshape(b * l, d)], axis=0)
            idx4 = jnp.concatenate([idx4, idxc.reshape(b * l, V7X_LANES)[:, :TOP_K]], axis=0)
            gates = jnp.concatenate([gates, gtsc.reshape(b * l, V7X_LANES)], axis=0)

        streams = [(x, mod[5])] + ([(ctx, mod_c[5])] if with_ctx else [])
        outs = _moe_residual(tok, idx4, gates, exp_w1[ly], exp_b1[ly], exp_w2[ly], exp_b2[ly], streams)
        x = outs[0]
        if with_ctx:
            ctx = outs[1]
    return x
```

```python
import functools
import math

import jax
import jax.numpy as jnp
import numpy as np
from jax import lax
from jax.experimental import pallas as pl
from jax.experimental.pallas import tpu as pltpu

F32 = jnp.float32
BF16 = jnp.bfloat16
HIGHEST = lax.Precision.HIGHEST

GRID_W = 64
HEAD_DIM = 64
NA_HEADS = 4
NA_WIN_R = 8
NA_WIN_C = 16
GQA_Q_HEADS = 4
GQA_KV_HEADS = 2
DN_HEADS = 4
DN_CONV = 5
DN_CHUNK = 64
MLA_HEADS = 4
MLA_Q_RANK = 256
MLA_KV_RANK = 128
MLA_NOPE = 64
MLA_ROPE = 32
MLA_V = 64
MLA_QK = MLA_NOPE + MLA_ROPE
N_EXPERTS = 32
TOP_K = 4
SWIGLU_LIMIT = 7.0
SWIGLU_ALPHA = 1.702
ROPE_THETA = 10000.0
EPS = 1e-6
NEG_INF = -1e30
N_ADA = 6

NA_W = NA_HEADS * HEAD_DIM
GQA_QW = GQA_Q_HEADS * HEAD_DIM
GQA_KVW = GQA_KV_HEADS * HEAD_DIM
DN_W = DN_HEADS * HEAD_DIM
MLA_W = MLA_HEADS * MLA_V
IN_SIZES = (NA_W, NA_W, NA_W, GQA_QW, GQA_KVW, GQA_KVW, DN_W, DN_W, DN_W, DN_W, 2 * DN_HEADS, 2 * DN_HEADS,
            MLA_Q_RANK, MLA_KV_RANK, MLA_ROPE)
IN_OFFSETS = tuple(int(o) for o in np.cumsum(IN_SIZES)[:-1])

V7X_VMEM_BYTES = 64 * 1024 * 1024
V7X_LANES = 128
VMEM_LIMIT = V7X_VMEM_BYTES - 8 * 1024 * 1024

PROJ_TM = 512
ATTN_ROWS = 256
ATTN_TK = 256
NA_GROUP = 8
NA_KEY_ROWS = 16
DN_BLOCK_CHUNKS = 8
DN_PREP_INTERLEAVE = 2
MOE_TM = 512
MOE_FC = 512
MOE_ISSUE_UNROLL = 8
COMBINE_ROWS = 256
COMBINE_ISSUE_UNROLL = 2


def _cparams(semantics):
    return pltpu.CompilerParams(dimension_semantics=semantics, vmem_limit_bytes=VMEM_LIMIT)


def _rms(x, g):
    return x * lax.rsqrt(jnp.mean(x * x, axis=-1, keepdims=True) + EPS) * g


def _mm_bias_kernel(x_ref, w_ref, b_ref, o_ref):
    o_ref[...] = jnp.dot(x_ref[...].astype(BF16), w_ref[...].astype(BF16), preferred_element_type=F32) + b_ref[...]


def _mm_bias(x, w, b, tn):
    m, k = x.shape
    n = w.shape[1]
    return pl.pallas_call(
        _mm_bias_kernel,
        grid=(n // tn,),
        in_specs=[pl.BlockSpec((m, k), lambda j: (0, 0)),
                  pl.BlockSpec((k, tn), lambda j: (0, j)),
                  pl.BlockSpec((1, tn), lambda j: (0, j))],
        out_specs=pl.BlockSpec((m, tn), lambda j: (0, j)),
        out_shape=jax.ShapeDtypeStruct((m, n), F32),
        compiler_params=_cparams(("parallel",)),
        name="ada_mm",
    )(x, w, b.reshape(1, n))


def _head_epilogue(x, gain, width, rope, scale, ones_lane):
    lane = lax.broadcasted_iota(jnp.int32, (1, V7X_LANES), 1)
    if gain is not None:
        ms = jnp.sum(x * x, axis=-1, keepdims=True) * (1.0 / width)
        x = x * lax.rsqrt(ms + EPS) * gain
    if rope is not None:
        swapped = jnp.where((lane % 2) == 0, pltpu.roll(x, V7X_LANES - 1, 1), pltpu.roll(x, 1, 1))
        x = x * rope[0] + swapped * rope[1]
    if scale != 1.0:
        x = x * scale
    if ones_lane is not None:
        x = x + (lane == ones_lane).astype(F32)
    return x


def _mla_q_kernel(x_ref, g_ref, w_ref, qn_ref, rc_ref, rs_ref, o_ref, *, use_rope):
    y = _rms(x_ref[...], g_ref[...])
    p = jnp.dot(y.astype(BF16), w_ref[...], preferred_element_type=F32)
    rope = (rc_ref[...], rs_ref[...]) if use_rope else None
    for h in range(MLA_HEADS):
        sl = slice(h * V7X_LANES, (h + 1) * V7X_LANES)
        o_ref[:, sl] = _head_epilogue(p[:, sl], qn_ref[...], MLA_QK, rope, MLA_QK ** -0.5, None).astype(o_ref.dtype)


def _mla_kv_kernel(x_ref, g_ref, w_ref, small_ref, kn_ref, rc_ref, rs_ref, k_ref, v_ref, *, use_rope):
    y = _rms(x_ref[...], g_ref[...])
    p = jnp.dot(y.astype(BF16), w_ref[...], preferred_element_type=F32)
    lane = lax.broadcasted_iota(jnp.int32, (1, V7X_LANES), 1)
    kpe = pltpu.roll(jnp.where(lane < MLA_ROPE, small_ref[...], 0.0), MLA_NOPE, 1)
    rope = (rc_ref[...], rs_ref[...]) if use_rope else None
    width = MLA_HEADS * V7X_LANES
    for h in range(MLA_HEADS):
        sl = slice(h * V7X_LANES, (h + 1) * V7X_LANES)
        k_ref[:, sl] = _head_epilogue(p[:, sl] + kpe, kn_ref[...], MLA_QK, rope, 1.0, None).astype(k_ref.dtype)
        vs = slice(width + h * V7X_LANES, width + (h + 1) * V7X_LANES)
        v_ref[:, sl] = _head_epilogue(p[:, vs], None, MLA_V, None, 1.0, MLA_V).astype(v_ref.dtype)


def _lane_gain(g):
    return jnp.pad(g, (0, V7X_LANES - g.shape[0])).reshape(1, V7X_LANES)


def _mla_q(cq, cq_g, w_q, qn_g, rope, tm=1024):
    b, n, k = cq.shape
    tm = min(tm, n)
    width = MLA_HEADS * V7X_LANES
    per_batch = n // tm
    use_rope = rope is not None
    tabs = rope if use_rope else (jnp.zeros((tm, V7X_LANES), F32),) * 2
    tab_spec = pl.BlockSpec((tm, V7X_LANES), (lambda i: (i % per_batch, 0)) if use_rope else (lambda i: (0, 0)))
    out = pl.pallas_call(
        functools.partial(_mla_q_kernel, use_rope=use_rope),
        grid=(b * n // tm,),
        in_specs=[pl.BlockSpec((tm, k), lambda i: (i, 0)),
                  pl.BlockSpec((1, k), lambda i: (0, 0)),
                  pl.BlockSpec((k, width), lambda i: (0, 0)),
                  pl.BlockSpec((1, V7X_LANES), lambda i: (0, 0)),
                  tab_spec, tab_spec],
        out_specs=pl.BlockSpec((tm, width), lambda i: (i, 0)),
        out_shape=jax.ShapeDtypeStruct((b * n, width), BF16),
        compiler_params=_cparams(("parallel",)),
        name="mla_q",
    )(cq.reshape(b * n, k), cq_g.reshape(1, k), w_q, _lane_gain(qn_g), *tabs)
    return out.reshape(b, n, width)


def _mla_kv(ckv, small, ckv_g, w_kv, kn_g, rope, tm=1024):
    b, n, k = ckv.shape
    tm = min(tm, n)
    width = MLA_HEADS * V7X_LANES
    per_batch = n // tm
    use_rope = rope is not None
    tabs = rope if use_rope else (jnp.zeros((tm, V7X_LANES), F32),) * 2
    tab_spec = pl.BlockSpec((tm, V7X_LANES), (lambda i: (i % per_batch, 0)) if use_rope else (lambda i: (0, 0)))
    tok = pl.BlockSpec((tm, width), lambda i: (i, 0))
    kk, vv = pl.pallas_call(
        functools.partial(_mla_kv_kernel, use_rope=use_rope),
        grid=(b * n // tm,),
        in_specs=[pl.BlockSpec((tm, k), lambda i: (i, 0)),
                  pl.BlockSpec((1, k), lambda i: (0, 0)),
                  pl.BlockSpec((k, 2 * width), lambda i: (0, 0)),
                  pl.BlockSpec((tm, V7X_LANES), lambda i: (i, 0)),
                  pl.BlockSpec((1, V7X_LANES), lambda i: (0, 0)),
                  tab_spec, tab_spec],
        out_specs=[tok, tok],
        out_shape=[jax.ShapeDtypeStruct((b * n, width), BF16)] * 2,
        compiler_params=_cparams(("parallel",)),
        name="mla_kv",
    )(ckv.reshape(b * n, k), ckv_g.reshape(1, k), w_kv, small.reshape(b * n, V7X_LANES), _lane_gain(kn_g), *tabs)
    return kk.reshape(b, n, width), vv.reshape(b, n, width)


PROJ_OUTPUTS = (
    (NA_HEADS * V7X_LANES, NA_HEADS, 0, False, HEAD_DIM ** -0.5, None),
    (NA_HEADS * V7X_LANES, NA_HEADS, 1, False, 1.0, None),
    (NA_HEADS * V7X_LANES, NA_HEADS, None, False, 1.0, HEAD_DIM),
    (GQA_Q_HEADS * V7X_LANES, GQA_Q_HEADS, 2, True, HEAD_DIM ** -0.5, None),
    (GQA_KV_HEADS * V7X_LANES, GQA_KV_HEADS, 3, True, 1.0, None),
    (GQA_KV_HEADS * V7X_LANES, GQA_KV_HEADS, None, False, 1.0, HEAD_DIM),
    (3 * DN_W, 0, None, False, 1.0, None),
    (DN_W, 0, None, False, 1.0, None),
    (MLA_Q_RANK, 0, None, False, 1.0, None),
    (MLA_KV_RANK, 0, None, False, 1.0, None),
    (V7X_LANES, 0, None, False, 1.0, None),
)


def _in_proj_kernel(x_ref, g_ref, sh_ref, sc_ref, w_ref, gains_ref, rc_ref, rs_ref, *out_refs, use_rope):
    h = _rms(x_ref[0], g_ref[...]) * (1.0 + sc_ref[0]) + sh_ref[0]
    hb = h.astype(BF16)
    off = 0
    for o_ref, (wd, nheads, gain_row, rotary, scale, ones_lane) in zip(out_refs, PROJ_OUTPUTS):
        p = jnp.dot(hb, w_ref[:, off:off + wd], preferred_element_type=F32)
        off += wd
        if nheads == 0:
            o_ref[0] = p
            continue
        gain = None if gain_row is None else gains_ref[gain_row:gain_row + 1, :]
        rope = (rc_ref[...], rs_ref[...]) if (rotary and use_rope) else None
        for hh in range(nheads):
            sl = slice(hh * V7X_LANES, (hh + 1) * V7X_LANES)
            o_ref[0, :, sl] = _head_epilogue(p[:, sl], gain, HEAD_DIM, rope, scale, ones_lane).astype(o_ref.dtype)


def _in_proj(x, g, shift, scale, w_perm, gains, rope, tm):
    b, n, d = x.shape
    wtot = w_perm.shape[1]
    vec = pl.BlockSpec((1, 1, d), lambda i, j: (i, 0, 0))
    use_rope = rope is not None
    tabs = rope if use_rope else (jnp.zeros((tm, V7X_LANES), F32),) * 2
    tab_spec = pl.BlockSpec((tm, V7X_LANES), (lambda i, j: (j, 0)) if use_rope else (lambda i, j: (0, 0)))
    return pl.pallas_call(
        functools.partial(_in_proj_kernel, use_rope=use_rope),
        grid=(b, n // tm),
        in_specs=[pl.BlockSpec((1, tm, d), lambda i, j: (i, j, 0)),
                  pl.BlockSpec((1, d), lambda i, j: (0, 0)),
                  vec, vec,
                  pl.BlockSpec((d, wtot), lambda i, j: (0, 0)),
                  pl.BlockSpec(gains.shape, lambda i, j: (0, 0)),
                  tab_spec, tab_spec],
        out_specs=[pl.BlockSpec((1, tm, o[0]), lambda i, j: (i, j, 0)) for o in PROJ_OUTPUTS],
        out_shape=[jax.ShapeDtypeStruct((b, n, o[0]), F32 if o[1] == 0 else BF16) for o in PROJ_OUTPUTS],
        compiler_params=_cparams(("parallel", "parallel")),
        name="in_proj",
    )(x, g.reshape(1, d), shift.reshape(b, 1, d), scale.reshape(b, 1, d), w_perm, gains, *tabs)


def _attn_kernel(q_ref, k_ref, v_ref, o_ref, s_scr, *, tk, hq, hkv, dv):
    grp = hq // hkv
    tq = q_ref.shape[1]
    rows = grp * tq
    n_keys = k_ref.shape[1]
    nk = n_keys // tk
    nt = (((1,), (1,)), ((), ()))
    outs = [None] * hq
    for g in range(hkv):
        ks = slice(g * V7X_LANES, (g + 1) * V7X_LANES)
        q = jnp.concatenate([q_ref[0, :, (g * grp + r) * V7X_LANES:(g * grp + r + 1) * V7X_LANES] for r in range(grp)],
                            axis=0)
        mx = None
        for j in range(nk):
            s = lax.dot_general(q, k_ref[0, j * tk:(j + 1) * tk, ks], nt, preferred_element_type=F32)
            s_scr[:, j * tk:(j + 1) * tk] = s
            for t in range(tk // V7X_LANES):
                slab = s[:, t * V7X_LANES:(t + 1) * V7X_LANES]
                mx = slab if mx is None else jnp.maximum(mx, slab)
        mb = jnp.broadcast_to(jnp.max(mx, axis=-1, keepdims=True), (rows, V7X_LANES))
        acc = None
        for j in range(nk):
            slabs = [jnp.exp(s_scr[:, j * tk + t * V7X_LANES:j * tk + (t + 1) * V7X_LANES] - mb)
                     for t in range(tk // V7X_LANES)]
            p = jnp.concatenate(slabs, axis=-1).astype(BF16)
            part = jnp.dot(p, v_ref[0, j * tk:(j + 1) * tk, ks], preferred_element_type=F32)
            acc = part if acc is None else acc + part
        o = acc[:, :dv] / acc[:, dv:dv + 1]
        for r in range(grp):
            outs[g * grp + r] = o[r * tq:(r + 1) * tq]
    o_ref[0] = jnp.concatenate(outs, axis=-1).astype(o_ref.dtype)


def _attention(q, k, v1, hq, hkv, dv, tk=ATTN_TK):
    b, nq, _ = q.shape
    m = k.shape[1]
    grp = hq // hkv
    tq = min(ATTN_ROWS // grp, nq)
    tk = min(tk, m)
    return pl.pallas_call(
        functools.partial(_attn_kernel, tk=tk, hq=hq, hkv=hkv, dv=dv),
        grid=(b, nq // tq),
        in_specs=[pl.BlockSpec((1, tq, hq * V7X_LANES), lambda i, j: (i, j, 0)),
                  pl.BlockSpec((1, m, hkv * V7X_LANES), lambda i, j: (i, 0, 0)),
                  pl.BlockSpec((1, m, hkv * V7X_LANES), lambda i, j: (i, 0, 0))],
        out_specs=pl.BlockSpec((1, tq, hq * dv), lambda i, j: (i, j, 0)),
        out_shape=jax.ShapeDtypeStruct((b, nq, hq * dv), BF16),
        scratch_shapes=[pltpu.VMEM((grp * tq, m), F32)],
        compiler_params=_cparams(("parallel", "parallel")),
        name="attention",
    )(q, k, v1)


def _na_kernel(q_ref, k_ref, v_ref, kc_ref, vc_ref, bias_ref, o_ref, s_scr, *, rows):
    g = pl.program_id(1)
    w0 = jnp.clip(g * NA_GROUP - NA_WIN_R // 2, 0, rows - NA_KEY_ROWS)
    off = pl.multiple_of(w0 * GRID_W, GRID_W)
    nq = NA_GROUP * GRID_W
    nloc = NA_KEY_ROWS * GRID_W
    l = kc_ref.shape[1]
    tk = min(ATTN_TK, l)
    nt = (((1,), (1,)), ((), ()))
    outs = []
    for h in range(NA_HEADS):
        hs = slice(h * V7X_LANES, (h + 1) * V7X_LANES)
        qh = q_ref[0, :, hs]
        mx = None
        for j in range((nloc + l) // tk):
            if j * tk < nloc:
                kj = k_ref[0, pl.ds(off + j * tk, tk), hs]
                s = lax.dot_general(qh, kj, nt, preferred_element_type=F32) + bias_ref[0, h, :, j * tk:(j + 1) * tk]
            else:
                kj = kc_ref[0, j * tk - nloc:(j + 1) * tk - nloc, hs]
                s = lax.dot_general(qh, kj, nt, preferred_element_type=F32)
            s_scr[:, j * tk:(j + 1) * tk] = s
            for t in range(tk // V7X_LANES):
                slab = s[:, t * V7X_LANES:(t + 1) * V7X_LANES]
                mx = slab if mx is None else jnp.maximum(mx, slab)
        mb = jnp.broadcast_to(jnp.max(mx, axis=-1, keepdims=True), (nq, V7X_LANES))
        acc = None
        for j in range((nloc + l) // tk):
            slabs = [jnp.exp(s_scr[:, j * tk + t * V7X_LANES:j * tk + (t + 1) * V7X_LANES] - mb)
                     for t in range(tk // V7X_LANES)]
            p = jnp.concatenate(slabs, axis=-1).astype(BF16)
            if j * tk < nloc:
                vj = v_ref[0, pl.ds(off + j * tk, tk), hs]
            else:
                vj = vc_ref[0, j * tk - nloc:(j + 1) * tk - nloc, hs]
            part = jnp.dot(p, vj, preferred_element_type=F32)
            acc = part if acc is None else acc + part
        outs.append(acc[:, :HEAD_DIM] / acc[:, HEAD_DIM:HEAD_DIM + 1])
    o_ref[0] = jnp.concatenate(outs, axis=-1).astype(o_ref.dtype)


def _na_bias_table(rel_bias, rows):
    kr = NA_WIN_R
    cpos = np.arange(GRID_W)
    c0 = np.clip(cpos - NA_WIN_C // 2, 0, GRID_W - NA_WIN_C)
    col_ok = (cpos[None, :] >= c0[:, None]) & (cpos[None, :] < c0[:, None] + NA_WIN_C)
    dc = np.clip(cpos[None, :] - cpos[:, None], -(NA_WIN_C - 1), NA_WIN_C - 1) + (NA_WIN_C - 1)
    n_dr, n_dc = rel_bias.shape[1:]
    select = np.zeros((n_dc, GRID_W * GRID_W), np.float32)
    select[dc.ravel(), np.arange(GRID_W * GRID_W)] = 1.0
    base = jnp.dot(rel_bias.reshape(NA_HEADS * n_dr, n_dc), jnp.asarray(select), precision=HIGHEST)
    base = jnp.where(col_ok[None, None], base.reshape(NA_HEADS, n_dr, GRID_W, GRID_W), NEG_INF)
    ngroups = rows // NA_GROUP
    pick = np.zeros((3, NA_GROUP, NA_KEY_ROWS, n_dr + 1), np.float32)
    for var, g in enumerate((0, min(1, ngroups - 1), ngroups - 1)):
        w0 = int(np.clip(g * NA_GROUP - kr // 2, 0, rows - NA_KEY_ROWS))
        for qr in range(NA_GROUP):
            r = g * NA_GROUP + qr
            r0 = int(np.clip(r - kr // 2, 0, rows - kr))
            for kl in range(NA_KEY_ROWS):
                key_row = w0 + kl
                pick[var, qr, kl, key_row - r + kr - 1 if r0 <= key_row < r0 + kr else n_dr] = 1.0
    base_ext = jnp.concatenate([jnp.transpose(base, (1, 0, 2, 3)).reshape(n_dr, -1),
                                jnp.full((1, NA_HEADS * GRID_W * GRID_W), NEG_INF, F32)], axis=0)
    tab = jnp.dot(jnp.asarray(pick.reshape(-1, n_dr + 1)), base_ext, precision=HIGHEST)
    tab = tab.reshape(3, NA_GROUP, NA_KEY_ROWS, NA_HEADS, GRID_W, GRID_W)
    return jnp.transpose(tab, (0, 3, 1, 4, 2, 5)).reshape(3, NA_HEADS, NA_GROUP * GRID_W, NA_KEY_ROWS * GRID_W)


def _na_attention(q, k, v, kc, vc, bias_tab):
    b, n, w = q.shape
    l = kc.shape[1]
    rows = n // GRID_W
    assert rows % NA_GROUP == 0 and rows >= NA_KEY_ROWS
    ngroups = rows // NA_GROUP
    nq = NA_GROUP * GRID_W

    def bias_map(i, g):
        return (jnp.where(g == 0, 0, jnp.where(g == ngroups - 1, 2, 1)), 0, 0, 0)

    return pl.pallas_call(
        functools.partial(_na_kernel, rows=rows),
        grid=(b, ngroups),
        in_specs=[pl.BlockSpec((1, nq, w), lambda i, g: (i, g, 0)),
                  pl.BlockSpec((1, n, w), lambda i, g: (i, 0, 0)),
                  pl.BlockSpec((1, n, w), lambda i, g: (i, 0, 0)),
                  pl.BlockSpec((1, l, w), lambda i, g: (i, 0, 0)),
                  pl.BlockSpec((1, l, w), lambda i, g: (i, 0, 0)),
                  pl.BlockSpec((1, NA_HEADS, nq, NA_KEY_ROWS * GRID_W), bias_map)],
        out_specs=pl.BlockSpec((1, nq, NA_W), lambda i, g: (i, g, 0)),
        out_shape=jax.ShapeDtypeStruct((b, n, NA_W), BF16),
        scratch_shapes=[pltpu.VMEM((nq, NA_KEY_ROWS * GRID_W + l), F32)],
        compiler_params=_cparams(("parallel", "arbitrary")),
        name="na_attention",
    )(q, k, v, kc, vc, bias_tab)


DN_LEVELS = int(math.log2(DN_CHUNK))


def _dn_constants():
    c, w = DN_CHUNK, DN_W
    ii = np.arange(c)[:, None]
    jj = (np.arange(w) % c)[None, :]
    hh = (np.arange(w) // HEAD_DIM)[None, :]
    tri, expand, neg, strict, lvl = [], [], [], [], []
    for d in range(2):
        rev = d == 1
        incl = (ii <= jj) if rev else (ii >= jj)
        neg.append(np.where(incl, 0.0, NEG_INF))
        strict.append((ii < jj) if rev else (ii > jj))
        t = np.arange(c)
        tri.append((t[:, None] <= t[None, :]) if rev else (t[:, None] >= t[None, :]))
        expand.append(np.arange(V7X_LANES)[:, None] == hh + d * DN_HEADS)
        for lv in range(DN_LEVELS):
            same = (ii >> (lv + 1)) == (jj >> (lv + 1))
            hi_i = ((ii >> lv) & 1) == 1
            hi_j = ((jj >> lv) & 1) == 1
            lvl.append((same & ~hi_i & hi_j) if rev else (same & hi_i & ~hi_j))
    r2 = np.arange(w)
    block = (r2[:, None] // HEAD_DIM) == (r2[None, :] // HEAD_DIM)
    f = lambda a, dt: jnp.asarray(np.stack(a).astype(np.float32), dtype=dt)
    return dict(tri=f(tri, F32), expand=f(expand, F32), neg=f(neg, F32), strict=f(strict, F32),
                lvl=f(lvl, BF16).reshape(2, DN_LEVELS, c, w), eye=jnp.asarray((ii == jj).astype(np.float32)),
                block16=jnp.asarray(block.astype(np.float32), dtype=BF16), block32=jnp.asarray(block.astype(np.float32)))


def _dn_block_diag(y16, block16):
    return jnp.concatenate([y16] * DN_HEADS, axis=0) * block16


def _dn_prep_kernel(q_ref, k_ref, v_ref, g_ref, b_ref, tri_ref, exp_ref, neg_ref, strict_ref, lvl_ref, eye_ref,
                    blk_ref, *out_refs, chunks):
    c = DN_CHUNK
    nt = (((1,), (1,)), ((), ()))

    def bdmm(x, y):
        return jnp.dot(x.astype(BF16), _dn_block_diag(y.astype(BF16), blk_ref[...]), preferred_element_type=F32)

    def chunk_group(gi, carry):
        probs = []
        for sub in range(DN_PREP_INTERLEAVE):
            ci = gi * DN_PREP_INTERLEAVE + sub
            rows = pl.ds(pl.multiple_of(ci * c, c), c)
            q = q_ref[0, rows, :] * (HEAD_DIM ** -0.5)
            k = k_ref[0, rows, :]
            v = v_ref[0, rows, :]
            gcol = g_ref[0, rows, :]
            bcol = b_ref[0, rows, :]
            kbd = _dn_block_diag(k.astype(BF16), blk_ref[...])
            kk = lax.dot_general(k.astype(BF16), kbd, nt, preferred_element_type=F32)
            qk = lax.dot_general(q.astype(BF16), kbd, nt, preferred_element_type=F32)
            for d in range(2):
                gc = jnp.dot(tri_ref[d], gcol, preferred_element_type=F32, precision=HIGHEST)
                gcx = jnp.dot(gc, exp_ref[d], preferred_element_type=F32, precision=HIGHEST)
                bx = jnp.dot(bcol, exp_ref[d], preferred_element_type=F32, precision=HIGHEST)
                gr = jnp.sum(gcx * eye_ref[...], axis=0, keepdims=True)
                decay = jnp.exp(gcx - gr + neg_ref[d])
                a16 = (kk * bx * decay * strict_ref[d]).astype(BF16)
                probs.append(dict(d=d, ci=ci, rows=rows, q=q, k=k, v=v, qk=qk, gcx=gcx, bx=bx, decay=decay, a16=a16,
                                  t=eye_ref[...]))
        for lv in range(DN_LEVELS):
            for p in probs:
                p["x"] = jnp.dot(p["t"].astype(BF16), _dn_block_diag(p["a16"] * lvl_ref[p["d"], lv], blk_ref[...]),
                                 preferred_element_type=F32)
            for p in probs:
                p["t"] = p["t"] - bdmm(p["x"], p["t"])
        for p in probs:
            d, rows, gcx = p["d"], p["rows"], p["gcx"]
            u_ref, w_ref, a_ref, qg_ref, kd_ref, eg_ref = out_refs[6 * d:6 * d + 6]
            last = 0 if d == 1 else c - 1
            e_gc = jnp.exp(gcx)
            g_last = gcx[last:last + 1, :]
            u_ref[0, rows, :] = bdmm(p["t"], p["v"] * p["bx"]).astype(u_ref.dtype)
            w_ref[0, rows, :] = bdmm(p["t"], p["k"] * p["bx"] * e_gc).astype(w_ref.dtype)
            a_ref[0, rows, :] = (p["qk"] * p["decay"]).astype(a_ref.dtype)
            qg_ref[0, rows, :] = (p["q"] * e_gc).astype(qg_ref.dtype)
            kd_ref[0, rows, :] = (p["k"] * jnp.exp(g_last - gcx)).astype(kd_ref.dtype)
            eg_ref[0, p["ci"]] = jnp.exp(g_last)
        return carry

    lax.fori_loop(0, chunks // DN_PREP_INTERLEAVE, chunk_group, 0)


def _dn_scan_kernel(*refs, chunks):
    ins = (refs[0:6], refs[6:12])
    s0_refs = refs[12:14]
    b16_ref, b32_ref = refs[14:16]
    o_refs = refs[16:18]
    sf_refs = refs[18:20]
    s_scr = refs[20]
    c = DN_CHUNK
    blk = pl.program_id(1)

    @pl.when(blk == 0)
    def _():
        s_scr[0] = s0_refs[0][0]
        s_scr[1] = s0_refs[1][0]

    def chunk(ci, carry):
        st = []
        for d in range(2):
            idx = (chunks - 1 - ci) if d == 1 else ci
            st.append(dict(idx=idx, rows=pl.ds(pl.multiple_of(idx * c, c), c), s=s_scr[d]))
        for d, p in enumerate(st):
            u_ref, w_ref, a_ref, qg_ref, kd_ref, eg_ref = ins[d]
            wq = jnp.concatenate([w_ref[0, p["rows"], :], qg_ref[0, p["rows"], :]], axis=0)
            p["r"] = jnp.dot(wq, p["s"].astype(BF16), preferred_element_type=F32)
        for d, p in enumerate(st):
            u_ref = ins[d][0]
            p["vn"] = (u_ref[0, p["rows"], :].astype(F32) - p["r"][:c]).astype(BF16)
        for d, p in enumerate(st):
            kd_ref, eg_ref = ins[d][4], ins[d][5]
            kv = lax.dot_general(kd_ref[0, p["rows"], :], p["vn"], (((0,), (0,)), ((), ())), preferred_element_type=F32)
            s_scr[d] = p["s"] * eg_ref[0, p["idx"]] + kv * b32_ref[...]
        for d, p in enumerate(st):
            a_ref = ins[d][2]
            o_refs[d][0, p["rows"], :] = p["r"][c:] + jnp.dot(a_ref[0, p["rows"], :],
                                                              _dn_block_diag(p["vn"], b16_ref[...]),
                                                              preferred_element_type=F32)
        return carry

    lax.fori_loop(0, chunks, chunk, 0)

    @pl.when(blk == pl.num_programs(1) - 1)
    def _():
        sf_refs[0][0] = s_scr[0]
        sf_refs[1][0] = s_scr[1]


def _delta_bidir(q, k, v, g, beta, s0_f, s0_b):
    b, n, w = q.shape
    nchunks = n // DN_CHUNK
    chunks = min(DN_BLOCK_CHUNKS, nchunks)
    nblk = nchunks // chunks
    bt = chunks * DN_CHUNK

    tok = pl.BlockSpec((1, bt, w), lambda i, j: (i, j, 0))
    small = pl.BlockSpec((1, bt, V7X_LANES), lambda i, j: (i, j, 0))
    egl = pl.BlockSpec((1, chunks, 1, w), lambda i, j: (i, j, 0, 0))
    per_dir_shapes = [jax.ShapeDtypeStruct((b, n, w), BF16)] * 5 + [jax.ShapeDtypeStruct((b, nchunks, 1, w), F32)]
    cst = _dn_constants()

    def const_spec(a):
        nd = a.ndim
        return pl.BlockSpec(a.shape, lambda i, j: (0,) * nd)

    prep_consts = [cst[name] for name in ("tri", "expand", "neg", "strict", "lvl", "eye", "block16")]
    prep = pl.pallas_call(
        functools.partial(_dn_prep_kernel, chunks=chunks),
        grid=(b, nblk),
        in_specs=[tok, tok, tok, small, small] + [const_spec(a) for a in prep_consts],
        out_specs=([tok] * 5 + [egl]) * 2,
        out_shape=per_dir_shapes * 2,
        compiler_params=_cparams(("parallel", "parallel")),
        name="delta_prep",
    )(q, k, v, g, beta, *prep_consts)

    tok_r = pl.BlockSpec((1, bt, w), lambda i, j: (i, nblk - 1 - j, 0))
    egl_r = pl.BlockSpec((1, chunks, 1, w), lambda i, j: (i, nblk - 1 - j, 0, 0))
    state = pl.BlockSpec((1, w, w), lambda i, j: (i, 0, 0))
    scan_consts = [cst["block16"], cst["block32"]]
    o_f, o_b, s_f, s_b = pl.pallas_call(
        functools.partial(_dn_scan_kernel, chunks=chunks),
        grid=(b, nblk),
        in_specs=[tok] * 5 + [egl] + [tok_r] * 5 + [egl_r] + [state, state] + [const_spec(a) for a in scan_consts],
        out_specs=[tok, tok_r, state, state],
        out_shape=[jax.ShapeDtypeStruct((b, n, w), F32)] * 2 + [jax.ShapeDtypeStruct((b, w, w), F32)] * 2,
        scratch_shapes=[pltpu.VMEM((2, w, w), F32)],
        compiler_params=_cparams(("parallel", "arbitrary")),
        name="delta_scan",
    )(*prep, s0_f, s0_b, *scan_consts)
    return o_f, o_b, s_f, s_b


def _out_proj_kernel(x_ref, ya_ref, yb_ref, of_ref, ob_ref, dgate_ref, yd_ref, hm_ref, dg_ref, w_ref, gate_ref, g2_ref,
                     sh_ref, sc_ref, rw_ref, rb_ref, xo_ref, h2_ref, idx_ref, gt_ref):
    od = of_ref[0] + ob_ref[0]
    ms = jnp.dot(od * od, hm_ref[...], preferred_element_type=F32, precision=HIGHEST)
    yc = (od * lax.rsqrt(ms + EPS) * dg_ref[...] * jax.nn.silu(dgate_ref[0])).astype(BF16)
    acc = None
    for i, y in enumerate((ya_ref[0], yb_ref[0], yc, yd_ref[0])):
        wd = y.shape[-1]
        part = jnp.dot(y, w_ref[i * wd:(i + 1) * wd, :], preferred_element_type=F32)
        acc = part if acc is None else acc + part
    xn = x_ref[0] + gate_ref[0] * acc
    xo_ref[0] = xn
    h2 = _rms(xn, g2_ref[...]) * (1.0 + sc_ref[0]) + sh_ref[0]
    h2_ref[0] = h2
    logits = jnp.dot(h2, rw_ref[...], preferred_element_type=F32, precision=HIGHEST) + rb_ref[...]
    lane = lax.broadcasted_iota(jnp.int32, logits.shape, 1).astype(F32)
    vals, idxs = [], []
    cur = logits
    for _ in range(TOP_K):
        mx = jnp.max(cur, axis=-1, keepdims=True)
        ik = jnp.min(jnp.where(cur == mx, lane, float(V7X_LANES)), axis=-1, keepdims=True)
        vals.append(mx)
        idxs.append(ik)
        cur = jnp.where(lane == ik, -jnp.inf, cur)
    es = [jnp.exp(vv - vals[0]) for vv in vals]
    den = es[0] + es[1] + es[2] + es[3]
    idx_out = jnp.zeros(logits.shape, F32)
    gate_out = jnp.zeros(logits.shape, F32)
    for kk in range(TOP_K):
        idx_out = jnp.where(lane == float(kk), idxs[kk], idx_out)
        gate_out = jnp.where(lane == float(kk), es[kk] / den, gate_out)
    idx_ref[0] = idx_out.astype(jnp.int32)
    gt_ref[0] = gate_out


def _out_proj(x, ya, yb, o_f, o_b, dn_gate, dn_out_g, yd, w_out, gate, g2, shift, scale, rw_pad, rb_pad, tm):
    b, n, d = x.shape
    yw = ya.shape[-1]
    vec = pl.BlockSpec((1, 1, d), lambda i, j: (i, 0, 0))
    tok = pl.BlockSpec((1, tm, d), lambda i, j: (i, j, 0))
    ytok = pl.BlockSpec((1, tm, yw), lambda i, j: (i, j, 0))
    ltok = pl.BlockSpec((1, tm, V7X_LANES), lambda i, j: (i, j, 0))
    head = np.arange(yw) // HEAD_DIM
    head_mean = jnp.asarray((head[:, None] == head[None, :]).astype(np.float32) / HEAD_DIM)
    return pl.pallas_call(
        _out_proj_kernel,
        grid=(b, n // tm),
        in_specs=[tok, ytok, ytok, ytok, ytok, ytok, ytok,
                  pl.BlockSpec((yw, yw), lambda i, j: (0, 0)),
                  pl.BlockSpec((1, yw), lambda i, j: (0, 0)),
                  pl.BlockSpec((4 * yw, d), lambda i, j: (0, 0)),
                  vec,
                  pl.BlockSpec((1, d), lambda i, j: (0, 0)),
                  vec, vec,
                  pl.BlockSpec((d, V7X_LANES), lambda i, j: (0, 0)),
                  pl.BlockSpec((1, V7X_LANES), lambda i, j: (0, 0))],
        out_specs=[tok, tok, ltok, ltok],
        out_shape=[jax.ShapeDtypeStruct((b, n, d), F32), jax.ShapeDtypeStruct((b, n, d), F32),
                   jax.ShapeDtypeStruct((b, n, V7X_LANES), jnp.int32), jax.ShapeDtypeStruct((b, n, V7X_LANES), F32)],
        compiler_params=_cparams(("parallel", "parallel")),
        name="out_proj",
    )(x, ya, yb, o_f, o_b, dn_gate, yd, head_mean, jnp.tile(dn_out_g, DN_HEADS).reshape(1, yw), w_out,
      gate.reshape(b, 1, d), g2.reshape(1, d), shift.reshape(b, 1, d), scale.reshape(b, 1, d), rw_pad, rb_pad)


def _experts_kernel(te_ref, tf_ref, tv_ref, src0_ref, srcn_ref, h_hbm, w1_ref, b1_ref, w2_ref, b2_ref, o_ref,
                    xbuf, w1b, w2b, sem, *, fc):
    i = pl.program_id(0)
    nt = pl.num_programs(0)
    tm = o_ref.shape[0]
    f = w2_ref.shape[1]
    slot = i % 2

    def row_copy(src_ref, r, dst_slot):
        return pltpu.make_async_copy(h_hbm.at[pl.ds(src_ref[0, 0, r], 1)], xbuf.at[dst_slot, pl.ds(r, 1)],
                                     sem.at[dst_slot])

    def issue(src_ref, dst_slot):
        def body(r8, carry):
            for u in range(MOE_ISSUE_UNROLL):
                row_copy(src_ref, r8 * MOE_ISSUE_UNROLL + u, dst_slot).start()
            return carry
        lax.fori_loop(0, tm // MOE_ISSUE_UNROLL, body, 0)

    def wait_tile(s):
        pltpu.make_async_copy(h_hbm.at[pl.ds(0, tm)], xbuf.at[s], sem.at[s]).wait()

    @pl.when(jnp.logical_and(i == 0, tv_ref[0] != 0))
    def _():
        issue(src0_ref, 0)

    @pl.when(tv_ref[i] == 0)
    def _():
        @pl.when(jnp.logical_and(i > 0, tv_ref[jnp.maximum(i - 1, 0)] != 0))
        def _():
            wait_tile(slot)
        o_ref[...] = jnp.zeros(o_ref.shape, o_ref.dtype)

    @pl.when(tv_ref[i] != 0)
    def _():
        issue(srcn_ref, 1 - slot)

        @pl.when(tf_ref[i] != 0)
        def _():
            w1b[...] = w1_ref[0].astype(BF16)
            w2b[...] = w2_ref[0].astype(BF16)

        wait_tile(slot)
        xb = xbuf[slot].astype(BF16)
        acc = None
        for j in range(f // fc):
            glu = jnp.dot(xb, w1b[:, j * fc:(j + 1) * fc], preferred_element_type=F32) + b1_ref[0, :, j * fc:(j + 1) * fc]
            lin = (jnp.dot(xb, w1b[:, f + j * fc:f + (j + 1) * fc], preferred_element_type=F32)
                   + b1_ref[0, :, f + j * fc:f + (j + 1) * fc])
            glu = jnp.minimum(glu, SWIGLU_LIMIT)
            lin = jnp.clip(lin, -SWIGLU_LIMIT, SWIGLU_LIMIT)
            act = glu * jax.nn.sigmoid(SWIGLU_ALPHA * glu) * (lin + 1.0)
            part = jnp.dot(act.astype(BF16), w2b[j * fc:(j + 1) * fc, :], preferred_element_type=F32)
            acc = part if acc is None else acc + part
        o_ref[...] = acc + b2_ref[0]

        @pl.when(i == nt - 1)
        def _():
            wait_tile(1 - slot)


def _expert_tiles(h, src, tile_e, tile_first, tile_valid, w1, b1, w2, b2, tm):
    t, d = h.shape
    r = src.shape[0]
    e, _, f2 = w1.shape
    f = f2 // 2
    nt = r // tm
    grid_spec = pltpu.PrefetchScalarGridSpec(
        num_scalar_prefetch=3,
        grid=(nt,),
        in_specs=[pl.BlockSpec((1, 1, tm), lambda i, te, tf, tv: (0, 0, 0), memory_space=pltpu.SMEM),
                  pl.BlockSpec((1, 1, tm), lambda i, te, tf, tv: (i + 1, 0, 0), memory_space=pltpu.SMEM),
                  pl.BlockSpec(memory_space=pl.ANY),
                  pl.BlockSpec((1, d, f2), lambda i, te, tf, tv: (te[i], 0, 0)),
                  pl.BlockSpec((1, 1, f2), lambda i, te, tf, tv: (te[i], 0, 0)),
                  pl.BlockSpec((1, f, d), lambda i, te, tf, tv: (te[i], 0, 0)),
                  pl.BlockSpec((1, 1, d), lambda i, te, tf, tv: (te[i], 0, 0))],
        out_specs=pl.BlockSpec((tm, d), lambda i, te, tf, tv: (i, 0)),
        scratch_shapes=[pltpu.VMEM((2, tm, d), F32), pltpu.VMEM((d, f2), BF16), pltpu.VMEM((f, d), BF16),
                        pltpu.SemaphoreType.DMA((2,))],
    )
    src3 = jnp.concatenate([src, jnp.zeros((tm,), src.dtype)]).reshape(nt + 1, 1, tm)
    return pl.pallas_call(
        functools.partial(_experts_kernel, fc=MOE_FC),
        grid_spec=grid_spec,
        out_shape=jax.ShapeDtypeStruct((r, d), F32),
        compiler_params=_cparams(("arbitrary",)),
        name="moe_experts",
    )(tile_e, tile_first, tile_valid, src3, src3, h, w1, b1.reshape(e, 1, f2), w2, b2.reshape(e, 1, d))


def _combine_kernel(pos_ref, ys_hbm, gate_ref, x_ref, mod_ref, o_ref, buf, sem):
    ct = o_ref.shape[0]

    def issue(r2, carry):
        for u in range(COMBINE_ISSUE_UNROLL):
            r = r2 * COMBINE_ISSUE_UNROLL + u
            for kk in range(TOP_K):
                p = pos_ref[0, 0, r * TOP_K + kk]
                pltpu.make_async_copy(ys_hbm.at[pl.ds(p, 1)], buf.at[kk, pl.ds(r, 1)], sem).start()
        return carry

    lax.fori_loop(0, ct // COMBINE_ISSUE_UNROLL, issue, 0)
    for kk in range(TOP_K):
        pltpu.make_async_copy(ys_hbm.at[pl.ds(0, ct)], buf.at[kk], sem).wait()
    g = gate_ref[...]
    mix = (g[:, 0:1] * buf[0] + g[:, 1:2] * buf[1]) + (g[:, 2:3] * buf[2] + g[:, 3:4] * buf[3])
    o_ref[...] = x_ref[...] + mod_ref[0] * mix


def _combine_rows(ys, pos, gates, x, mod, ct):
    b, n, d = x.shape
    t = b * n
    ct = min(ct, n)
    nsteps = t // ct
    per_batch = n // ct
    out = pl.pallas_call(
        _combine_kernel,
        grid=(nsteps,),
        in_specs=[pl.BlockSpec((1, 1, ct * TOP_K), lambda i: (i, 0, 0), memory_space=pltpu.SMEM),
                  pl.BlockSpec(memory_space=pl.ANY),
                  pl.BlockSpec((ct, V7X_LANES), lambda i: (i, 0)),
                  pl.BlockSpec((ct, d), lambda i: (i, 0)),
                  pl.BlockSpec((1, 1, d), lambda i: (i // per_batch, 0, 0))],
        out_specs=pl.BlockSpec((ct, d), lambda i: (i, 0)),
        out_shape=jax.ShapeDtypeStruct((t, d), F32),
        scratch_shapes=[pltpu.VMEM((TOP_K, ct, d), F32), pltpu.SemaphoreType.DMA],
        compiler_params=_cparams(("arbitrary",)),
        name="moe_combine",
    )(pos.reshape(nsteps, 1, ct * TOP_K), ys, gates, x.reshape(t, d), mod.reshape(b, 1, d))
    return out.reshape(b, n, d)


def _route_plan(idx4, tm):
    t = idx4.shape[0]
    e = N_EXPERTS
    r_max = t * TOP_K + e * tm
    nt = r_max // tm
    onehot = (idx4[:, :, None] == jnp.arange(e, dtype=jnp.int32)[None, None, :]).astype(jnp.int32)
    member = jnp.sum(onehot, axis=1)
    csum = jnp.cumsum(member, axis=0)
    cnt = csum[-1]
    excl = csum - member
    cnt_pad = ((cnt + tm - 1) // tm) * tm
    ends = jnp.cumsum(cnt_pad)
    base = ends - cnt_pad
    pos = jnp.sum(onehot * (excl + base[None, :])[:, None, :], axis=-1)
    flat = pos.reshape(-1)
    tok = jnp.repeat(jnp.arange(t, dtype=jnp.int32), TOP_K)
    src = jnp.zeros((r_max,), jnp.int32).at[flat].set(tok, unique_indices=True)
    tile_start = jnp.arange(nt, dtype=jnp.int32) * tm
    tile_e = jnp.minimum(jnp.sum((ends[None, :] <= tile_start[:, None]).astype(jnp.int32), axis=1), e - 1)
    tile_valid = (tile_start < ends[-1]).astype(jnp.int32)
    tile_first = jnp.concatenate([jnp.ones((1,), jnp.int32), (tile_e[1:] != tile_e[:-1]).astype(jnp.int32)])
    return pos.astype(jnp.int32), src, tile_e, tile_first, tile_valid


def _moe_residual(h2, idx4, gates, w1, b1, w2, b2, streams):
    pos, src, tile_e, tile_first, tile_valid = _route_plan(idx4, MOE_TM)
    ys = _expert_tiles(h2, src, tile_e, tile_first, tile_valid, w1, b1, w2, b2, MOE_TM)
    outs, start = [], 0
    for x, mod in streams:
        cnt = x.shape[0] * x.shape[1]
        outs.append(_combine_rows(ys, pos[start:start + cnt], gates[start:start + cnt], x, mod, COMBINE_ROWS))
        start += cnt
    return outs


def _axial_rope_tables(n_tok, rot_dim):
    t = jnp.arange(n_tok, dtype=jnp.int32)
    row = (t // GRID_W).astype(F32)
    col = (t % GRID_W).astype(F32)
    n_freq = rot_dim // 4
    freqs = ROPE_THETA ** (-jnp.arange(n_freq, dtype=F32) / n_freq)
    ang = jnp.concatenate([row[:, None] * freqs, col[:, None] * freqs], axis=-1)
    return jnp.cos(ang), jnp.sin(ang)


def _rope_lane_tables(cos, sin, start):
    n, nf = cos.shape
    c2 = jnp.repeat(cos, 2, axis=1)
    s2 = jnp.stack([-sin, sin], axis=-1).reshape(n, 2 * nf)
    tail = V7X_LANES - start - 2 * nf
    c = jnp.concatenate([jnp.ones((n, start), F32), c2, jnp.ones((n, tail), F32)], axis=1)
    s = jnp.concatenate([jnp.zeros((n, start), F32), s2, jnp.zeros((n, tail), F32)], axis=1)
    return c, s


def _permute_w_in(w_in):
    offs = (0,) + IN_OFFSETS

    def seg(i):
        return np.arange(offs[i], offs[i] + IN_SIZES[i])

    def head_groups(i, nheads):
        cols = seg(i).reshape(nheads, HEAD_DIM)
        return np.concatenate([cols, -np.ones((nheads, V7X_LANES - HEAD_DIM), np.int64)], axis=1).reshape(-1)

    cols = np.concatenate([head_groups(0, NA_HEADS), head_groups(1, NA_HEADS), head_groups(2, NA_HEADS),
                           head_groups(3, GQA_Q_HEADS), head_groups(4, GQA_KV_HEADS), head_groups(5, GQA_KV_HEADS),
                           seg(6), seg(7), seg(8), seg(9), seg(12), seg(13), seg(14), seg(10), seg(11)])
    cols = np.concatenate([cols, -np.ones(sum(o[0] for o in PROJ_OUTPUTS) - cols.size, np.int64)])
    w = jnp.take(w_in, jnp.asarray(np.maximum(cols, 0)), axis=1)
    return jnp.where(jnp.asarray(cols >= 0)[None, :], w, 0.0).astype(BF16)


def _dn_conv_kernel(prev_ref, cur_ref, next_ref, w_ref, hs_ref, q_ref, k_ref, v_ref):
    j = pl.program_id(1)
    tb = cur_ref.shape[1]
    halo = prev_ref.shape[1]
    prev = jnp.where(j > 0, prev_ref[0], 0.0)
    nxt = jnp.where(j < pl.num_programs(1) - 1, next_ref[0], 0.0)
    win = jnp.concatenate([prev, cur_ref[0], nxt], axis=0)
    acc = None
    for i in range(DN_CONV):
        start = halo + i - DN_CONV // 2
        term = win[start:start + tb, :] * w_ref[i:i + 1, :]
        acc = term if acc is None else acc + term
    act = jax.nn.silu(acc)
    for ref, lo, normed in ((q_ref, 0, True), (k_ref, DN_W, True), (v_ref, 2 * DN_W, False)):
        t = act[:, lo:lo + DN_W]
        if normed:
            ss = jnp.dot(t * t, hs_ref[...], preferred_element_type=F32, precision=HIGHEST)
            t = t * lax.rsqrt(ss + EPS)
        ref[0] = t


def _dn_conv(qkv, conv_w, tb=512):
    b, n, w3 = qkv.shape
    tb = min(tb, n)
    halo = 8
    per_blk = tb // halo
    last = n // halo - 1
    head = np.arange(DN_W) // HEAD_DIM
    head_sum = jnp.asarray((head[:, None] == head[None, :]).astype(np.float32))
    out = pl.BlockSpec((1, tb, DN_W), lambda i, j: (i, j, 0))
    return pl.pallas_call(
        _dn_conv_kernel,
        grid=(b, n // tb),
        in_specs=[pl.BlockSpec((1, halo, w3), lambda i, j: (i, jnp.maximum(j * per_blk - 1, 0), 0)),
                  pl.BlockSpec((1, tb, w3), lambda i, j: (i, j, 0)),
                  pl.BlockSpec((1, halo, w3), lambda i, j: (i, jnp.minimum((j + 1) * per_blk, last), 0)),
                  pl.BlockSpec((8, w3), lambda i, j: (0, 0)),
                  pl.BlockSpec((DN_W, DN_W), lambda i, j: (0, 0))],
        out_specs=[out, out, out],
        out_shape=[jax.ShapeDtypeStruct((b, n, DN_W), F32)] * 3,
        compiler_params=_cparams(("parallel", "parallel")),
        name="delta_conv",
    )(qkv, qkv, qkv, jnp.pad(conv_w, ((0, 8 - DN_CONV), (0, 0))), head_sum)


def _dn_prep(qkv, small, conv_w, a_log, dt_bias):
    q, k, v = _dn_conv(qkv, conv_w)
    beta_raw = small[..., MLA_ROPE:MLA_ROPE + 2 * DN_HEADS]
    a_raw = small[..., MLA_ROPE + 2 * DN_HEADS:MLA_ROPE + 4 * DN_HEADS]
    beta = jax.nn.sigmoid(beta_raw)
    g = -jnp.exp(a_log.reshape(-1)) * jax.nn.softplus(a_raw + dt_bias.reshape(-1))
    lane_pad = ((0, 0), (0, 0), (0, V7X_LANES - 2 * DN_HEADS))
    return q, k, v, jnp.pad(g, lane_pad), jnp.pad(beta, lane_pad)


def _mla_weights(w_uq, w_ukv):
    wq = jnp.pad(w_uq.reshape(MLA_Q_RANK, MLA_HEADS, MLA_QK), ((0, 0), (0, 0), (0, V7X_LANES - MLA_QK)))
    wkv = w_ukv.reshape(MLA_KV_RANK, MLA_HEADS, MLA_NOPE + MLA_V)
    wk = jnp.pad(wkv[..., :MLA_NOPE], ((0, 0), (0, 0), (0, V7X_LANES - MLA_NOPE)))
    wv = jnp.pad(wkv[..., MLA_NOPE:], ((0, 0), (0, 0), (0, V7X_LANES - MLA_V)))
    width = MLA_HEADS * V7X_LANES
    w_kv = jnp.concatenate([wk.reshape(MLA_KV_RANK, width), wv.reshape(MLA_KV_RANK, width)], axis=1)
    return wq.reshape(MLA_Q_RANK, width).astype(BF16), w_kv.astype(BF16)


def kernel(x, c, ctx, c_ctx, ada_w, ada_b, norm1_g, norm2_g, w_in, w_out, na_qn_g, na_kn_g, na_rel_bias, gqa_qn_g, gqa_kn_g, dn_conv_w, dn_a_log, dn_dt_bias, dn_out_g, mla_cq_g, mla_ckv_g, mla_w_uq, mla_w_ukv, mla_qn_g, mla_kn_g, router_w, router_b, exp_w1, exp_b1, exp_w2, exp_b2):
    b, n, d = x.shape
    l = ctx.shape[1]
    depth = ada_w.shape[0]
    rope_g = _rope_lane_tables(*_axial_rope_tables(n, HEAD_DIM), 0)
    rope_m = _rope_lane_tables(*_axial_rope_tables(n, MLA_ROPE), MLA_NOPE)
    cond = jnp.concatenate([jax.nn.silu(c), jax.nn.silu(c_ctx)[None], jnp.zeros((16 - b - 1, d), F32)], axis=0)
    s0 = jnp.zeros((b, DN_W, DN_W), F32)

    for ly in range(depth):
        with_ctx = ly < depth - 1
        mod_all = _mm_bias(cond, ada_w[ly], ada_b[ly], 1024)
        mod = jnp.split(mod_all[:b], N_ADA, axis=-1)
        mod_c = [jnp.broadcast_to(m_, (b, d)) for m_ in jnp.split(mod_all[b:b + 1], N_ADA, axis=-1)]
        w_perm = _permute_w_in(w_in[ly])
        gains = jnp.concatenate([_lane_gain(g_) for g_ in (na_qn_g[ly], na_kn_g[ly], gqa_qn_g[ly], gqa_kn_g[ly])]
                                + [jnp.zeros((4, V7X_LANES), F32)], axis=0)
        p = _in_proj(x, norm1_g[ly], mod[0], mod[1], w_perm, gains, rope_g, PROJ_TM)
        pc = _in_proj(ctx, norm1_g[ly], mod_c[0], mod_c[1], w_perm, gains, None, l)
        bias_tab = _na_bias_table(na_rel_bias[ly], n // GRID_W)

        ya = _na_attention(p[0], p[1], p[2], pc[1], pc[2], bias_tab)

        yb = _attention(p[3], jnp.concatenate([p[4], pc[4]], axis=1), jnp.concatenate([p[5], pc[5]], axis=1),
                        GQA_Q_HEADS, GQA_KV_HEADS, HEAD_DIM)

        qdc, kdc, vdc, gdc, bdc = _dn_prep(pc[6], pc[10], dn_conv_w[ly], dn_a_log[ly], dn_dt_bias[ly])
        odc_f, odc_b, s_f, s_b = _delta_bidir(qdc, kdc, vdc, gdc, bdc, s0, s0)
        qd, kd, vd, gd, bd = _dn_prep(p[6], p[10], dn_conv_w[ly], dn_a_log[ly], dn_dt_bias[ly])
        od_f, od_b, _, _ = _delta_bidir(qd, kd, vd, gd, bd, s_f, s_b)

        w_q, w_kv = _mla_weights(mla_w_uq[ly], mla_w_ukv[ly])
        qm = _mla_q(p[8], mla_cq_g[ly], w_q, mla_qn_g[ly], rope_m)
        km, vm = _mla_kv(p[9], p[10], mla_ckv_g[ly], w_kv, mla_kn_g[ly], rope_m)
        kmc, vmc = _mla_kv(pc[9], pc[10], mla_ckv_g[ly], w_kv, mla_kn_g[ly], None)
        yd = _attention(qm, jnp.concatenate([km, kmc], axis=1), jnp.concatenate([vm, vmc], axis=1),
                        MLA_HEADS, MLA_HEADS, MLA_V)

        rw_pad = jnp.pad(router_w[ly], ((0, 0), (0, V7X_LANES - N_EXPERTS)))
        rb_pad = jnp.pad(router_b[ly], (0, V7X_LANES - N_EXPERTS), constant_values=NEG_INF).reshape(1, V7X_LANES)
        w_out_b = w_out[ly].astype(BF16)
        x, h2, idx, gts = _out_proj(x, ya, yb, od_f, od_b, p[7], dn_out_g[ly], yd, w_out_b, mod[2], norm2_g[ly],
                                    mod[3], mod[4], rw_pad, rb_pad, PROJ_TM)
        tok = h2.reshape(b * n, d)
        idx4 = idx.reshape(b * n, V7X_LANES)[:, :TOP_K]
        gates = gts.reshape(b * n, V7X_LANES)

        if with_ctx:
            yac = _attention(pc[0], pc[1], pc[2], NA_HEADS, NA_HEADS, HEAD_DIM)
            ybc = _attention(pc[3], pc[4], pc[5], GQA_Q_HEADS, GQA_KV_HEADS, HEAD_DIM)
            qmc = _mla_q(pc[8], mla_cq_g[ly], w_q, mla_qn_g[ly], None)
            ydc = _attention(qmc, kmc, vmc, MLA_HEADS, MLA_HEADS, MLA_V)
            ctx, h2c, idxc, gtsc = _out_proj(ctx, yac, ybc, odc_f, odc_b, pc[7], dn_out_g[ly], ydc, w_out_b, mod_c[2],
                                             norm2_g[ly], mod_c[3], mod_c[4], rw_pad, rb_pad, l)
            tok = jnp.concatenate([tok, h2c.reshape(b * l, d)], axis=0)
            idx4 = jnp.concatenate([idx4, idxc.reshape(b * l, V7X_LANES)[:, :TOP_K]], axis=0)
            gates = jnp.concatenate([gates, gtsc.reshape(b * l, V7X_LANES)], axis=0)

        streams = [(x, mod[5])] + ([(ctx, mod_c[5])] if with_ctx else [])
        outs = _moe_residual(tok, idx4, gates, exp_w1[ly], exp_b1[ly], exp_w2[ly], exp_b2[ly], streams)
        x = outs[0]
        if with_ctx:
            ctx = outs[1]
    return x
```

```python
import functools
import math

import jax
import jax.numpy as jnp
import numpy as np
from jax import lax
from jax.experimental import pallas as pl
from jax.experimental.pallas import tpu as pltpu

F32 = jnp.float32
BF16 = jnp.bfloat16
HIGHEST = lax.Precision.HIGHEST

GRID_W = 64
HEAD_DIM = 64
NA_HEADS = 4
NA_WIN_R = 8
NA_WIN_C = 16
GQA_Q_HEADS = 4
GQA_KV_HEADS = 2
DN_HEADS = 4
DN_CONV = 5
DN_CHUNK = 64
MLA_HEADS = 4
MLA_Q_RANK = 256
MLA_KV_RANK = 128
MLA_NOPE = 64
MLA_ROPE = 32
MLA_V = 64
MLA_QK = MLA_NOPE + MLA_ROPE
N_EXPERTS = 32
TOP_K = 4
SWIGLU_LIMIT = 7.0
SWIGLU_ALPHA = 1.702
ROPE_THETA = 10000.0
EPS = 1e-6
NEG_INF = -1e30
N_ADA = 6

NA_W = NA_HEADS * HEAD_DIM
GQA_QW = GQA_Q_HEADS * HEAD_DIM
GQA_KVW = GQA_KV_HEADS * HEAD_DIM
DN_W = DN_HEADS * HEAD_DIM
MLA_W = MLA_HEADS * MLA_V
IN_SIZES = (NA_W, NA_W, NA_W, GQA_QW, GQA_KVW, GQA_KVW, DN_W, DN_W, DN_W, DN_W, 2 * DN_HEADS, 2 * DN_HEADS,
            MLA_Q_RANK, MLA_KV_RANK, MLA_ROPE)
IN_OFFSETS = tuple(int(o) for o in np.cumsum(IN_SIZES)[:-1])

V7X_VMEM_BYTES = 64 * 1024 * 1024
V7X_LANES = 128
VMEM_LIMIT = V7X_VMEM_BYTES - 8 * 1024 * 1024

PROJ_TM = 512
ATTN_ROWS = 256
ATTN_TK = 256
NA_GROUP = 8
NA_KEY_ROWS = 16
DN_BLOCK_CHUNKS = 8
DN_PREP_INTERLEAVE = 2
MOE_TM = 512
MOE_FC = 512
MOE_ISSUE_UNROLL = 8
COMBINE_ROWS = 256
COMBINE_ISSUE_UNROLL = 2


def _cparams(semantics):
    return pltpu.CompilerParams(dimension_semantics=semantics, vmem_limit_bytes=VMEM_LIMIT)


def _rms(x, g):
    return x * lax.rsqrt(jnp.mean(x * x, axis=-1, keepdims=True) + EPS) * g


def _mm_bias_kernel(x_ref, w_ref, b_ref, o_ref):
    o_ref[...] = jnp.dot(x_ref[...].astype(BF16), w_ref[...].astype(BF16), preferred_element_type=F32) + b_ref[...]


def _mm_bias(x, w, b, tn):
    m, k = x.shape
    n = w.shape[1]
    return pl.pallas_call(
        _mm_bias_kernel,
        grid=(n // tn,),
        in_specs=[pl.BlockSpec((m, k), lambda j: (0, 0)),
                  pl.BlockSpec((k, tn), lambda j: (0, j)),
                  pl.BlockSpec((1, tn), lambda j: (0, j))],
        out_specs=pl.BlockSpec((m, tn), lambda j: (0, j)),
        out_shape=jax.ShapeDtypeStruct((m, n), F32),
        compiler_params=_cparams(("parallel",)),
        name="ada_mm",
    )(x, w, b.reshape(1, n))


def _head_epilogue(x, gain, width, rope, scale, ones_lane):
    lane = lax.broadcasted_iota(jnp.int32, (1, V7X_LANES), 1)
    if gain is not None:
        ms = jnp.sum(x * x, axis=-1, keepdims=True) * (1.0 / width)
        x = x * lax.rsqrt(ms + EPS) * gain
    if rope is not None:
        swapped = jnp.where((lane % 2) == 0, pltpu.roll(x, V7X_LANES - 1, 1), pltpu.roll(x, 1, 1))
        x = x * rope[0] + swapped * rope[1]
    if scale != 1.0:
        x = x * scale
    if ones_lane is not None:
        x = x + (lane == ones_lane).astype(F32)
    return x


def _mla_q_kernel(x_ref, g_ref, w_ref, qn_ref, rc_ref, rs_ref, o_ref, *, use_rope):
    y = _rms(x_ref[...], g_ref[...])
    p = jnp.dot(y.astype(BF16), w_ref[...], preferred_element_type=F32)
    rope = (rc_ref[...], rs_ref[...]) if use_rope else None
    for h in range(MLA_HEADS):
        sl = slice(h * V7X_LANES, (h + 1) * V7X_LANES)
        o_ref[:, sl] = _head_epilogue(p[:, sl], qn_ref[...], MLA_QK, rope, MLA_QK ** -0.5, None).astype(o_ref.dtype)


def _mla_kv_kernel(x_ref, g_ref, w_ref, small_ref, kn_ref, rc_ref, rs_ref, k_ref, v_ref, *, use_rope):
    y = _rms(x_ref[...], g_ref[...])
    p = jnp.dot(y.astype(BF16), w_ref[...], preferred_element_type=F32)
    lane = lax.broadcasted_iota(jnp.int32, (1, V7X_LANES), 1)
    kpe = pltpu.roll(jnp.where(lane < MLA_ROPE, small_ref[...], 0.0), MLA_NOPE, 1)
    rope = (rc_ref[...], rs_ref[...]) if use_rope else None
    width = MLA_HEADS * V7X_LANES
    for h in range(MLA_HEADS):
        sl = slice(h * V7X_LANES, (h + 1) * V7X_LANES)
        k_ref[:, sl] = _head_epilogue(p[:, sl] + kpe, kn_ref[...], MLA_QK, rope, 1.0, None).astype(k_ref.dtype)
        vs = slice(width + h * V7X_LANES, width + (h + 1) * V7X_LANES)
        v_ref[:, sl] = _head_epilogue(p[:, vs], None, MLA_V, None, 1.0, MLA_V).astype(v_ref.dtype)


def _lane_gain(g):
    return jnp.pad(g, (0, V7X_LANES - g.shape[0])).reshape(1, V7X_LANES)


def _mla_q(cq, cq_g, w_q, qn_g, rope, tm=1024):
    b, n, k = cq.shape
    tm = min(tm, n)
    width = MLA_HEADS * V7X_LANES
    per_batch = n // tm
    use_rope = rope is not None
    tabs = rope if use_rope else (jnp.zeros((tm, V7X_LANES), F32),) * 2
    tab_spec = pl.BlockSpec((tm, V7X_LANES), (lambda i: (i % per_batch, 0)) if use_rope else (lambda i: (0, 0)))
    out = pl.pallas_call(
        functools.partial(_mla_q_kernel, use_rope=use_rope),
        grid=(b * n // tm,),
        in_specs=[pl.BlockSpec((tm, k), lambda i: (i, 0)),
                  pl.BlockSpec((1, k), lambda i: (0, 0)),
                  pl.BlockSpec((k, width), lambda i: (0, 0)),
                  pl.BlockSpec((1, V7X_LANES), lambda i: (0, 0)),
                  tab_spec, tab_spec],
        out_specs=pl.BlockSpec((tm, width), lambda i: (i, 0)),
        out_shape=jax.ShapeDtypeStruct((b * n, width), BF16),
        compiler_params=_cparams(("parallel",)),
        name="mla_q",
    )(cq.reshape(b * n, k), cq_g.reshape(1, k), w_q, _lane_gain(qn_g), *tabs)
    return out.reshape(b, n, width)


def _mla_kv(ckv, small, ckv_g, w_kv, kn_g, rope, tm=1024):
    b, n, k = ckv.shape
    tm = min(tm, n)
    width = MLA_HEADS * V7X_LANES
    per_batch = n // tm
    use_rope = rope is not None
    tabs = rope if use_rope else (jnp.zeros((tm, V7X_LANES), F32),) * 2
    tab_spec = pl.BlockSpec((tm, V7X_LANES), (lambda i: (i % per_batch, 0)) if use_rope else (lambda i: (0, 0)))
    tok = pl.BlockSpec((tm, width), lambda i: (i, 0))
    kk, vv = pl.pallas_call(
        functools.partial(_mla_kv_kernel, use_rope=use_rope),
        grid=(b * n // tm,),
        in_specs=[pl.BlockSpec((tm, k), lambda i: (i, 0)),
                  pl.BlockSpec((1, k), lambda i: (0, 0)),
                  pl.BlockSpec((k, 2 * width), lambda i: (0, 0)),
                  pl.BlockSpec((tm, V7X_LANES), lambda i: (i, 0)),
                  pl.BlockSpec((1, V7X_LANES), lambda i: (0, 0)),
                  tab_spec, tab_spec],
        out_specs=[tok, tok],
        out_shape=[jax.ShapeDtypeStruct((b * n, width), BF16)] * 2,
        compiler_params=_cparams(("parallel",)),
        name="mla_kv",
    )(ckv.reshape(b * n, k), ckv_g.reshape(1, k), w_kv, small.reshape(b * n, V7X_LANES), _lane_gain(kn_g), *tabs)
    return kk.reshape(b, n, width), vv.reshape(b, n, width)


PROJ_OUTPUTS = (
    (NA_HEADS * V7X_LANES, NA_HEADS, 0, False, HEAD_DIM ** -0.5, None),
    (NA_HEADS * V7X_LANES, NA_HEADS, 1, False, 1.0, None),
    (NA_HEADS * V7X_LANES, NA_HEADS, None, False, 1.0, HEAD_DIM),
    (GQA_Q_HEADS * V7X_LANES, GQA_Q_HEADS, 2, True, HEAD_DIM ** -0.5, None),
    (GQA_KV_HEADS * V7X_LANES, GQA_KV_HEADS, 3, True, 1.0, None),
    (GQA_KV_HEADS * V7X_LANES, GQA_KV_HEADS, None, False, 1.0, HEAD_DIM),
    (3 * DN_W, 0, None, False, 1.0, None),
    (DN_W, 0, None, False, 1.0, None),
    (MLA_Q_RANK, 0, None, False, 1.0, None),
    (MLA_KV_RANK, 0, None, False, 1.0, None),
    (V7X_LANES, 0, None, False, 1.0, None),
)


def _in_proj_kernel(x_ref, g_ref, sh_ref, sc_ref, w_ref, gains_ref, rc_ref, rs_ref, *out_refs, use_rope):
    h = _rms(x_ref[0], g_ref[...]) * (1.0 + sc_ref[0]) + sh_ref[0]
    hb = h.astype(BF16)
    off = 0
    for o_ref, (wd, nheads, gain_row, rotary, scale, ones_lane) in zip(out_refs, PROJ_OUTPUTS):
        p = jnp.dot(hb, w_ref[:, off:off + wd], preferred_element_type=F32)
        off += wd
        if nheads == 0:
            o_ref[0] = p
            continue
        gain = None if gain_row is None else gains_ref[gain_row:gain_row + 1, :]
        rope = (rc_ref[...], rs_ref[...]) if (rotary and use_rope) else None
        for hh in range(nheads):
            sl = slice(hh * V7X_LANES, (hh + 1) * V7X_LANES)
            o_ref[0, :, sl] = _head_epilogue(p[:, sl], gain, HEAD_DIM, rope, scale, ones_lane).astype(o_ref.dtype)


def _in_proj(x, g, shift, scale, w_perm, gains, rope, tm):
    b, n, d = x.shape
    wtot = w_perm.shape[1]
    vec = pl.BlockSpec((1, 1, d), lambda i, j: (i, 0, 0))
    use_rope = rope is not None
    tabs = rope if use_rope else (jnp.zeros((tm, V7X_LANES), F32),) * 2
    tab_spec = pl.BlockSpec((tm, V7X_LANES), (lambda i, j: (j, 0)) if use_rope else (lambda i, j: (0, 0)))
    return pl.pallas_call(
        functools.partial(_in_proj_kernel, use_rope=use_rope),
        grid=(b, n // tm),
        in_specs=[pl.BlockSpec((1, tm, d), lambda i, j: (i, j, 0)),
                  pl.BlockSpec((1, d), lambda i, j: (0, 0)),
                  vec, vec,
                  pl.BlockSpec((d, wtot), lambda i, j: (0, 0)),
                  pl.BlockSpec(gains.shape, lambda i, j: (0, 0)),
                  tab_spec, tab_spec],
        out_specs=[pl.BlockSpec((1, tm, o[0]), lambda i, j: (i, j, 0)) for o in PROJ_OUTPUTS],
        out_shape=[jax.ShapeDtypeStruct((b, n, o[0]), F32 if o[1] == 0 else BF16) for o in PROJ_OUTPUTS],
        compiler_params=_cparams(("parallel", "parallel")),
        name="in_proj",
    )(x, g.reshape(1, d), shift.reshape(b, 1, d), scale.reshape(b, 1, d), w_perm, gains, *tabs)


def _attn_kernel(q_ref, k_ref, v_ref, o_ref, s_scr, *, tk, hq, hkv, dv):
    grp = hq // hkv
    tq = q_ref.shape[1]
    rows = grp * tq
    n_keys = k_ref.shape[1]
    nk = n_keys // tk
    nt = (((1,), (1,)), ((), ()))
    outs = [None] * hq
    for g in range(hkv):
        ks = slice(g * V7X_LANES, (g + 1) * V7X_LANES)
        q = jnp.concatenate([q_ref[0, :, (g * grp + r) * V7X_LANES:(g * grp + r + 1) * V7X_LANES] for r in range(grp)],
                            axis=0)
        mx = None
        for j in range(nk):
            s = lax.dot_general(q, k_ref[0, j * tk:(j + 1) * tk, ks], nt, preferred_element_type=F32)
            s_scr[:, j * tk:(j + 1) * tk] = s
            for t in range(tk // V7X_LANES):
                slab = s[:, t * V7X_LANES:(t + 1) * V7X_LANES]
                mx = slab if mx is None else jnp.maximum(mx, slab)
        mb = jnp.broadcast_to(jnp.max(mx, axis=-1, keepdims=True), (rows, V7X_LANES))
        acc = None
        for j in range(nk):
            slabs = [jnp.exp(s_scr[:, j * tk + t * V7X_LANES:j * tk + (t + 1) * V7X_LANES] - mb)
                     for t in range(tk // V7X_LANES)]
            p = jnp.concatenate(slabs, axis=-1).astype(BF16)
            part = jnp.dot(p, v_ref[0, j * tk:(j + 1) * tk, ks], preferred_element_type=F32)
            acc = part if acc is None else acc + part
        o = acc[:, :dv] / acc[:, dv:dv + 1]
        for r in range(grp):
            outs[g * grp + r] = o[r * tq:(r + 1) * tq]
    o_ref[0] = jnp.concatenate(outs, axis=-1).astype(o_ref.dtype)


def _attention(q, k, v1, hq, hkv, dv, tk=ATTN_TK):
    b, nq, _ = q.shape
    m = k.shape[1]
    grp = hq // hkv
    tq = min(ATTN_ROWS // grp, nq)
    tk = min(tk, m)
    return pl.pallas_call(
        functools.partial(_attn_kernel, tk=tk, hq=hq, hkv=hkv, dv=dv),
        grid=(b, nq // tq),
        in_specs=[pl.BlockSpec((1, tq, hq * V7X_LANES), lambda i, j: (i, j, 0)),
                  pl.BlockSpec((1, m, hkv * V7X_LANES), lambda i, j: (i, 0, 0)),
                  pl.BlockSpec((1, m, hkv * V7X_LANES), lambda i, j: (i, 0, 0))],
        out_specs=pl.BlockSpec((1, tq, hq * dv), lambda i, j: (i, j, 0)),
        out_shape=jax.ShapeDtypeStruct((b, nq, hq * dv), BF16),
        scratch_shapes=[pltpu.VMEM((grp * tq, m), F32)],
        compiler_params=_cparams(("parallel", "parallel")),
        name="attention",
    )(q, k, v1)


def _na_kernel(q_ref, k_ref, v_ref, kc_ref, vc_ref, bias_ref, o_ref, s_scr, *, rows):
    g = pl.program_id(1)
    w0 = jnp.clip(g * NA_GROUP - NA_WIN_R // 2, 0, rows - NA_KEY_ROWS)
    off = pl.multiple_of(w0 * GRID_W, GRID_W)
    nq = NA_GROUP * GRID_W
    nloc = NA_KEY_ROWS * GRID_W
    l = kc_ref.shape[1]
    tk = min(ATTN_TK, l)
    nt = (((1,), (1,)), ((), ()))
    outs = []
    for h in range(NA_HEADS):
        hs = slice(h * V7X_LANES, (h + 1) * V7X_LANES)
        qh = q_ref[0, :, hs]
        mx = None
        for j in range((nloc + l) // tk):
            if j * tk < nloc:
                kj = k_ref[0, pl.ds(off + j * tk, tk), hs]
                s = lax.dot_general(qh, kj, nt, preferred_element_type=F32) + bias_ref[0, h, :, j * tk:(j + 1) * tk]
            else:
                kj = kc_ref[0, j * tk - nloc:(j + 1) * tk - nloc, hs]
                s = lax.dot_general(qh, kj, nt, preferred_element_type=F32)
            s_scr[:, j * tk:(j + 1) * tk] = s
            for t in range(tk // V7X_LANES):
                slab = s[:, t * V7X_LANES:(t + 1) * V7X_LANES]
                mx = slab if mx is None else jnp.maximum(mx, slab)
        mb = jnp.broadcast_to(jnp.max(mx, axis=-1, keepdims=True), (nq, V7X_LANES))
        acc = None
        for j in range((nloc + l) // tk):
            slabs = [jnp.exp(s_scr[:, j * tk + t * V7X_LANES:j * tk + (t + 1) * V7X_LANES] - mb)
                     for t in range(tk // V7X_LANES)]
            p = jnp.concatenate(slabs, axis=-1).astype(BF16)
            if j * tk < nloc:
                vj = v_ref[0, pl.ds(off + j * tk, tk), hs]
            else:
                vj = vc_ref[0, j * tk - nloc:(j + 1) * tk - nloc, hs]
            part = jnp.dot(p, vj, preferred_element_type=F32)
            acc = part if acc is None else acc + part
        outs.append(acc[:, :HEAD_DIM] / acc[:, HEAD_DIM:HEAD_DIM + 1])
    o_ref[0] = jnp.concatenate(outs, axis=-1).astype(o_ref.dtype)


def _na_bias_table(rel_bias, rows):
    kr = NA_WIN_R
    cpos = np.arange(GRID_W)
    c0 = np.clip(cpos - NA_WIN_C // 2, 0, GRID_W - NA_WIN_C)
    col_ok = (cpos[None, :] >= c0[:, None]) & (cpos[None, :] < c0[:, None] + NA_WIN_C)
    dc = np.clip(cpos[None, :] - cpos[:, None], -(NA_WIN_C - 1), NA_WIN_C - 1) + (NA_WIN_C - 1)
    n_dr, n_dc = rel_bias.shape[1:]
    select = np.zeros((n_dc, GRID_W * GRID_W), np.float32)
    select[dc.ravel(), np.arange(GRID_W * GRID_W)] = 1.0
    base = jnp.dot(rel_bias.reshape(NA_HEADS * n_dr, n_dc), jnp.asarray(select), precision=HIGHEST)
    base = jnp.where(col_ok[None, None], base.reshape(NA_HEADS, n_dr, GRID_W, GRID_W), NEG_INF)
    ngroups = rows // NA_GROUP
    pick = np.zeros((3, NA_GROUP, NA_KEY_ROWS, n_dr + 1), np.float32)
    for var, g in enumerate((0, min(1, ngroups - 1), ngroups - 1)):
        w0 = int(np.clip(g * NA_GROUP - kr // 2, 0, rows - NA_KEY_ROWS))
        for qr in range(NA_GROUP):
            r = g * NA_GROUP + qr
            r0 = int(np.clip(r - kr // 2, 0, rows - kr))
            for kl in range(NA_KEY_ROWS):
                key_row = w0 + kl
                pick[var, qr, kl, key_row - r + kr - 1 if r0 <= key_row < r0 + kr else n_dr] = 1.0
    base_ext = jnp.concatenate([jnp.transpose(base, (1, 0, 2, 3)).reshape(n_dr, -1),
                                jnp.full((1, NA_HEADS * GRID_W * GRID_W), NEG_INF, F32)], axis=0)
    tab = jnp.dot(jnp.asarray(pick.reshape(-1, n_dr + 1)), base_ext, precision=HIGHEST)
    tab = tab.reshape(3, NA_GROUP, NA_KEY_ROWS, NA_HEADS, GRID_W, GRID_W)
    return jnp.transpose(tab, (0, 3, 1, 4, 2, 5)).reshape(3, NA_HEADS, NA_GROUP * GRID_W, NA_KEY_ROWS * GRID_W)


def _na_attention(q, k, v, kc, vc, bias_tab):
    b, n, w = q.shape
    l = kc.shape[1]
    rows = n // GRID_W
    assert rows % NA_GROUP == 0 and rows >= NA_KEY_ROWS
    ngroups = rows // NA_GROUP
    nq = NA_GROUP * GRID_W

    def bias_map(i, g):
        return (jnp.where(g == 0, 0, jnp.where(g == ngroups - 1, 2, 1)), 0, 0, 0)

    return pl.pallas_call(
        functools.partial(_na_kernel, rows=rows),
        grid=(b, ngroups),
        in_specs=[pl.BlockSpec((1, nq, w), lambda i, g: (i, g, 0)),
                  pl.BlockSpec((1, n, w), lambda i, g: (i, 0, 0)),
                  pl.BlockSpec((1, n, w), lambda i, g: (i, 0, 0)),
                  pl.BlockSpec((1, l, w), lambda i, g: (i, 0, 0)),
                  pl.BlockSpec((1, l, w), lambda i, g: (i, 0, 0)),
                  pl.BlockSpec((1, NA_HEADS, nq, NA_KEY_ROWS * GRID_W), bias_map)],
        out_specs=pl.BlockSpec((1, nq, NA_W), lambda i, g: (i, g, 0)),
        out_shape=jax.ShapeDtypeStruct((b, n, NA_W), BF16),
        scratch_shapes=[pltpu.VMEM((nq, NA_KEY_ROWS * GRID_W + l), F32)],
        compiler_params=_cparams(("parallel", "arbitrary")),
        name="na_attention",
    )(q, k, v, kc, vc, bias_tab)


DN_LEVELS = int(math.log2(DN_CHUNK))


def _dn_constants():
    c, w = DN_CHUNK, DN_W
    ii = np.arange(c)[:, None]
    jj = (np.arange(w) % c)[None, :]
    hh = (np.arange(w) // HEAD_DIM)[None, :]
    tri, expand, neg, strict, lvl = [], [], [], [], []
    for d in range(2):
        rev = d == 1
        incl = (ii <= jj) if rev else (ii >= jj)
        neg.append(np.where(incl, 0.0, NEG_INF))
        strict.append((ii < jj) if rev else (ii > jj))
        t = np.arange(c)
        tri.append((t[:, None] <= t[None, :]) if rev else (t[:, None] >= t[None, :]))
        expand.append(np.arange(V7X_LANES)[:, None] == hh + d * DN_HEADS)
        for lv in range(DN_LEVELS):
            same = (ii >> (lv + 1)) == (jj >> (lv + 1))
            hi_i = ((ii >> lv) & 1) == 1
            hi_j = ((jj >> lv) & 1) == 1
            lvl.append((same & ~hi_i & hi_j) if rev else (same & hi_i & ~hi_j))
    r2 = np.arange(w)
    block = (r2[:, None] // HEAD_DIM) == (r2[None, :] // HEAD_DIM)
    f = lambda a, dt: jnp.asarray(np.stack(a).astype(np.float32), dtype=dt)
    return dict(tri=f(tri, F32), expand=f(expand, F32), neg=f(neg, F32), strict=f(strict, F32),
                lvl=f(lvl, BF16).reshape(2, DN_LEVELS, c, w), eye=jnp.asarray((ii == jj).astype(np.float32)),
                block16=jnp.asarray(block.astype(np.float32), dtype=BF16), block32=jnp.asarray(block.astype(np.float32)))


def _dn_block_diag(y16, block16):
    return jnp.concatenate([y16] * DN_HEADS, axis=0) * block16


def _dn_prep_kernel(q_ref, k_ref, v_ref, g_ref, b_ref, tri_ref, exp_ref, neg_ref, strict_ref, lvl_ref, eye_ref,
                    blk_ref, *out_refs, chunks):
    c = DN_CHUNK
    nt = (((1,), (1,)), ((), ()))

    def bdmm(x, y):
        return jnp.dot(x.astype(BF16), _dn_block_diag(y.astype(BF16), blk_ref[...]), preferred_element_type=F32)

    def chunk_group(gi, carry):
        probs = []
        for sub in range(DN_PREP_INTERLEAVE):
            ci = gi * DN_PREP_INTERLEAVE + sub
            rows = pl.ds(pl.multiple_of(ci * c, c), c)
            q = q_ref[0, rows, :] * (HEAD_DIM ** -0.5)
            k = k_ref[0, rows, :]
            v = v_ref[0, rows, :]
            gcol = g_ref[0, rows, :]
            bcol = b_ref[0, rows, :]
            kbd = _dn_block_diag(k.astype(BF16), blk_ref[...])
            kk = lax.dot_general(k.astype(BF16), kbd, nt, preferred_element_type=F32)
            qk = lax.dot_general(q.astype(BF16), kbd, nt, preferred_element_type=F32)
            for d in range(2):
                gc = jnp.dot(tri_ref[d], gcol, preferred_element_type=F32, precision=HIGHEST)
                gcx = jnp.dot(gc, exp_ref[d], preferred_element_type=F32, precision=HIGHEST)
                bx = jnp.dot(bcol, exp_ref[d], preferred_element_type=F32, precision=HIGHEST)
                gr = jnp.sum(gcx * eye_ref[...], axis=0, keepdims=True)
                decay = jnp.exp(gcx - gr + neg_ref[d])
                a16 = (kk * bx * decay * strict_ref[d]).astype(BF16)
                probs.append(dict(d=d, ci=ci, rows=rows, q=q, k=k, v=v, qk=qk, gcx=gcx, bx=bx, decay=decay, a16=a16,
                                  t=eye_ref[...]))
        for lv in range(DN_LEVELS):
            for p in probs:
                p["x"] = jnp.dot(p["t"].astype(BF16), _dn_block_diag(p["a16"] * lvl_ref[p["d"], lv], blk_ref[...]),
                                 preferred_element_type=F32)
            for p in probs:
                p["t"] = p["t"] - bdmm(p["x"], p["t"])
        for p in probs:
            d, rows, gcx = p["d"], p["rows"], p["gcx"]
            u_ref, w_ref, a_ref, qg_ref, kd_ref, eg_ref = out_refs[6 * d:6 * d + 6]
            last = 0 if d == 1 else c - 1
            e_gc = jnp.exp(gcx)
            g_last = gcx[last:last + 1, :]
            u_ref[0, rows, :] = bdmm(p["t"], p["v"] * p["bx"]).astype(u_ref.dtype)
            w_ref[0, rows, :] = bdmm(p["t"], p["k"] * p["bx"] * e_gc).astype(w_ref.dtype)
            a_ref[0, rows, :] = (p["qk"] * p["decay"]).astype(a_ref.dtype)
            qg_ref[0, rows, :] = (p["q"] * e_gc).astype(qg_ref.dtype)
            kd_ref[0, rows, :] = (p["k"] * jnp.exp(g_last - gcx)).astype(kd_ref.dtype)
            eg_ref[0, p["ci"]] = jnp.exp(g_last)
        return carry

    lax.fori_loop(0, chunks // DN_PREP_INTERLEAVE, chunk_group, 0)


def _dn_scan_kernel(*refs, chunks):
    ins = (refs[0:6], refs[6:12])
    s0_refs = refs[12:14]
    b16_ref, b32_ref = refs[14:16]
    o_refs = refs[16:18]
    sf_refs = refs[18:20]
    s_scr = refs[20]
    c = DN_CHUNK
    blk = pl.program_id(1)

    @pl.when(blk == 0)
    def _():
        s_scr[0] = s0_refs[0][0]
        s_scr[1] = s0_refs[1][0]

    def chunk(ci, carry):
        st = []
        for d in range(2):
            idx = (chunks - 1 - ci) if d == 1 else ci
            st.append(dict(idx=idx, rows=pl.ds(pl.multiple_of(idx * c, c), c), s=s_scr[d]))
        for d, p in enumerate(st):
            u_ref, w_ref, a_ref, qg_ref, kd_ref, eg_ref = ins[d]
            wq = jnp.concatenate([w_ref[0, p["rows"], :], qg_ref[0, p["rows"], :]], axis=0)
            p["r"] = jnp.dot(wq, p["s"].astype(BF16), preferred_element_type=F32)
        for d, p in enumerate(st):
            u_ref = ins[d][0]
            p["vn"] = (u_ref[0, p["rows"], :].astype(F32) - p["r"][:c]).astype(BF16)
        for d, p in enumerate(st):
            kd_ref, eg_ref = ins[d][4], ins[d][5]
            kv = lax.dot_general(kd_ref[0, p["rows"], :], p["vn"], (((0,), (0,)), ((), ())), preferred_element_type=F32)
            s_scr[d] = p["s"] * eg_ref[0, p["idx"]] + kv * b32_ref[...]
        for d, p in enumerate(st):
            a_ref = ins[d][2]
            o_refs[d][0, p["rows"], :] = p["r"][c:] + jnp.dot(a_ref[0, p["rows"], :],
                                                              _dn_block_diag(p["vn"], b16_ref[...]),
                                                              preferred_element_type=F32)
        return carry

    lax.fori_loop(0, chunks, chunk, 0)

    @pl.when(blk == pl.num_programs(1) - 1)
    def _():
        sf_refs[0][0] = s_scr[0]
        sf_refs[1][0] = s_scr[1]


def _delta_bidir(q, k, v, g, beta, s0_f, s0_b):
    b, n, w = q.shape
    nchunks = n // DN_CHUNK
    chunks = min(DN_BLOCK_CHUNKS, nchunks)
    nblk = nchunks // chunks
    bt = chunks * DN_CHUNK

    tok = pl.BlockSpec((1, bt, w), lambda i, j: (i, j, 0))
    small = pl.BlockSpec((1, bt, V7X_LANES), lambda i, j: (i, j, 0))
    egl = pl.BlockSpec((1, chunks, 1, w), lambda i, j: (i, j, 0, 0))
    per_dir_shapes = [jax.ShapeDtypeStruct((b, n, w), BF16)] * 5 + [jax.ShapeDtypeStruct((b, nchunks, 1, w), F32)]
    cst = _dn_constants()

    def const_spec(a):
        nd = a.ndim
        return pl.BlockSpec(a.shape, lambda i, j: (0,) * nd)

    prep_consts = [cst[name] for name in ("tri", "expand", "neg", "strict", "lvl", "eye", "block16")]
    prep = pl.pallas_call(
        functools.partial(_dn_prep_kernel, chunks=chunks),
        grid=(b, nblk),
        in_specs=[tok, tok, tok, small, small] + [const_spec(a) for a in prep_consts],
        out_specs=([tok] * 5 + [egl]) * 2,
        out_shape=per_dir_shapes * 2,
        compiler_params=_cparams(("parallel", "parallel")),
        name="delta_prep",
    )(q, k, v, g, beta, *prep_consts)

    tok_r = pl.BlockSpec((1, bt, w), lambda i, j: (i, nblk - 1 - j, 0))
    egl_r = pl.BlockSpec((1, chunks, 1, w), lambda i, j: (i, nblk - 1 - j, 0, 0))
    state = pl.BlockSpec((1, w, w), lambda i, j: (i, 0, 0))
    scan_consts = [cst["block16"], cst["block32"]]
    o_f, o_b, s_f, s_b = pl.pallas_call(
        functools.partial(_dn_scan_kernel, chunks=chunks),
        grid=(b, nblk),
        in_specs=[tok] * 5 + [egl] + [tok_r] * 5 + [egl_r] + [state, state] + [const_spec(a) for a in scan_consts],
        out_specs=[tok, tok_r, state, state],
        out_shape=[jax.ShapeDtypeStruct((b, n, w), F32)] * 2 + [jax.ShapeDtypeStruct((b, w, w), F32)] * 2,
        scratch_shapes=[pltpu.VMEM((2, w, w), F32)],
        compiler_params=_cparams(("parallel", "arbitrary")),
        name="delta_scan",
    )(*prep, s0_f, s0_b, *scan_consts)
    return o_f, o_b, s_f, s_b


def _out_proj_kernel(x_ref, ya_ref, yb_ref, of_ref, ob_ref, dgate_ref, yd_ref, hm_ref, dg_ref, w_ref, gate_ref, g2_ref,
                     sh_ref, sc_ref, rw_ref, rb_ref, xo_ref, h2_ref, idx_ref, gt_ref):
    od = of_ref[0] + ob_ref[0]
    ms = jnp.dot(od * od, hm_ref[...], preferred_element_type=F32, precision=HIGHEST)
    yc = (od * lax.rsqrt(ms + EPS) * dg_ref[...] * jax.nn.silu(dgate_ref[0])).astype(BF16)
    acc = None
    for i, y in enumerate((ya_ref[0], yb_ref[0], yc, yd_ref[0])):
        wd = y.shape[-1]
        part = jnp.dot(y, w_ref[i * wd:(i + 1) * wd, :], preferred_element_type=F32)
        acc = part if acc is None else acc + part
    xn = x_ref[0] + gate_ref[0] * acc
    xo_ref[0] = xn
    h2 = _rms(xn, g2_ref[...]) * (1.0 + sc_ref[0]) + sh_ref[0]
    h2_ref[0] = h2
    logits = jnp.dot(h2, rw_ref[...], preferred_element_type=F32, precision=HIGHEST) + rb_ref[...]
    lane = lax.broadcasted_iota(jnp.int32, logits.shape, 1).astype(F32)
    vals, idxs = [], []
    cur = logits
    for _ in range(TOP_K):
        mx = jnp.max(cur, axis=-1, keepdims=True)
        ik = jnp.min(jnp.where(cur == mx, lane, float(V7X_LANES)), axis=-1, keepdims=True)
        vals.append(mx)
        idxs.append(ik)
        cur = jnp.where(lane == ik, -jnp.inf, cur)
    es = [jnp.exp(vv - vals[0]) for vv in vals]
    den = es[0] + es[1] + es[2] + es[3]
    idx_out = jnp.zeros(logits.shape, F32)
    gate_out = jnp.zeros(logits.shape, F32)
    for kk in range(TOP_K):
        idx_out = jnp.where(lane == float(kk), idxs[kk], idx_out)
        gate_out = jnp.where(lane == float(kk), es[kk] / den, gate_out)
    idx_ref[0] = idx_out.astype(jnp.int32)
    gt_ref[0] = gate_out


def _out_proj(x, ya, yb, o_f, o_b, dn_gate, dn_out_g, yd, w_out, gate, g2, shift, scale, rw_pad, rb_pad, tm):
    b, n, d = x.shape
    yw = ya.shape[-1]
    vec = pl.BlockSpec((1, 1, d), lambda i, j: (i, 0, 0))
    tok = pl.BlockSpec((1, tm, d), lambda i, j: (i, j, 0))
    ytok = pl.BlockSpec((1, tm, yw), lambda i, j: (i, j, 0))
    ltok = pl.BlockSpec((1, tm, V7X_LANES), lambda i, j: (i, j, 0))
    head = np.arange(yw) // HEAD_DIM
    head_mean = jnp.asarray((head[:, None] == head[None, :]).astype(np.float32) / HEAD_DIM)
    return pl.pallas_call(
        _out_proj_kernel,
        grid=(b, n // tm),
        in_specs=[tok, ytok, ytok, ytok, ytok, ytok, ytok,
                  pl.BlockSpec((yw, yw), lambda i, j: (0, 0)),
                  pl.BlockSpec((1, yw), lambda i, j: (0, 0)),
                  pl.BlockSpec((4 * yw, d), lambda i, j: (0, 0)),
                  vec,
                  pl.BlockSpec((1, d), lambda i, j: (0, 0)),
                  vec, vec,
                  pl.BlockSpec((d, V7X_LANES), lambda i, j: (0, 0)),
                  pl.BlockSpec((1, V7X_LANES), lambda i, j: (0, 0))],
        out_specs=[tok, tok, ltok, ltok],
        out_shape=[jax.ShapeDtypeStruct((b, n, d), F32), jax.ShapeDtypeStruct((b, n, d), F32),
                   jax.ShapeDtypeStruct((b, n, V7X_LANES), jnp.int32), jax.ShapeDtypeStruct((b, n, V7X_LANES), F32)],
        compiler_params=_cparams(("parallel", "parallel")),
        name="out_proj",
    )(x, ya, yb, o_f, o_b, dn_gate, yd, head_mean, jnp.tile(dn_out_g, DN_HEADS).reshape(1, yw), w_out,
      gate.reshape(b, 1, d), g2.reshape(1, d), shift.reshape(b, 1, d), scale.reshape(b, 1, d), rw_pad, rb_pad)


def _experts_kernel(te_ref, tf_ref, tv_ref, src0_ref, srcn_ref, h_hbm, w1_ref, b1_ref, w2_ref, b2_ref, o_ref,
                    xbuf, w1b, w2b, sem, *, fc):
    i = pl.program_id(0)
    nt = pl.num_programs(0)
    tm = o_ref.shape[0]
    f = w2_ref.shape[1]
    slot = i % 2

    def row_copy(src_ref, r, dst_slot):
        return pltpu.make_async_copy(h_hbm.at[pl.ds(src_ref[0, 0, r], 1)], xbuf.at[dst_slot, pl.ds(r, 1)],
                                     sem.at[dst_slot])

    def issue(src_ref, dst_slot):
        def body(r8, carry):
            for u in range(MOE_ISSUE_UNROLL):
                row_copy(src_ref, r8 * MOE_ISSUE_UNROLL + u, dst_slot).start()
            return carry
        lax.fori_loop(0, tm // MOE_ISSUE_UNROLL, body, 0)

    def wait_tile(s):
        pltpu.make_async_copy(h_hbm.at[pl.ds(0, tm)], xbuf.at[s], sem.at[s]).wait()

    @pl.when(jnp.logical_and(i == 0, tv_ref[0] != 0))
    def _():
        issue(src0_ref, 0)

    @pl.when(tv_ref[i] == 0)
    def _():
        @pl.when(jnp.logical_and(i > 0, tv_ref[jnp.maximum(i - 1, 0)] != 0))
        def _():
            wait_tile(slot)
        o_ref[...] = jnp.zeros(o_ref.shape, o_ref.dtype)

    @pl.when(tv_ref[i] != 0)
    def _():
        issue(srcn_ref, 1 - slot)

        @pl.when(tf_ref[i] != 0)
        def _():
            w1b[...] = w1_ref[0].astype(BF16)
            w2b[...] = w2_ref[0].astype(BF16)

        wait_tile(slot)
        xb = xbuf[slot].astype(BF16)
        acc = None
        for j in range(f // fc):
            glu = jnp.dot(xb, w1b[:, j * fc:(j + 1) * fc], preferred_element_type=F32) + b1_ref[0, :, j * fc:(j + 1) * fc]
            lin = (jnp.dot(xb, w1b[:, f + j * fc:f + (j + 1) * fc], preferred_element_type=F32)
                   + b1_ref[0, :, f + j * fc:f + (j + 1) * fc])
            glu = jnp.minimum(glu, SWIGLU_LIMIT)
            lin = jnp.clip(lin, -SWIGLU_LIMIT, SWIGLU_LIMIT)
            act = glu * jax.nn.sigmoid(SWIGLU_ALPHA * glu) * (lin + 1.0)
            part = jnp.dot(act.astype(BF16), w2b[j * fc:(j + 1) * fc, :], preferred_element_type=F32)
            acc = part if acc is None else acc + part
        o_ref[...] = acc + b2_ref[0]

        @pl.when(i == nt - 1)
        def _():
            wait_tile(1 - slot)


def _expert_tiles(h, src, tile_e, tile_first, tile_valid, w1_all, b1_all, w2_all, b2_all, layer, tm):
    t, d = h.shape
    r = src.shape[0]
    depth, n_exp, _, f2 = w1_all.shape
    e = depth * n_exp
    f = f2 // 2
    nt = r // tm
    w1 = w1_all.reshape(e, d, f2)
    b1 = b1_all.reshape(e, f2)
    w2 = w2_all.reshape(e, f, d)
    b2 = b2_all.reshape(e, d)
    tile_e = tile_e + layer * n_exp
    grid_spec = pltpu.PrefetchScalarGridSpec(
        num_scalar_prefetch=3,
        grid=(nt,),
        in_specs=[pl.BlockSpec((1, 1, tm), lambda i, te, tf, tv: (0, 0, 0), memory_space=pltpu.SMEM),
                  pl.BlockSpec((1, 1, tm), lambda i, te, tf, tv: (i + 1, 0, 0), memory_space=pltpu.SMEM),
                  pl.BlockSpec(memory_space=pl.ANY),
                  pl.BlockSpec((1, d, f2), lambda i, te, tf, tv: (te[i], 0, 0)),
                  pl.BlockSpec((1, 1, f2), lambda i, te, tf, tv: (te[i], 0, 0)),
                  pl.BlockSpec((1, f, d), lambda i, te, tf, tv: (te[i], 0, 0)),
                  pl.BlockSpec((1, 1, d), lambda i, te, tf, tv: (te[i], 0, 0))],
        out_specs=pl.BlockSpec((tm, d), lambda i, te, tf, tv: (i, 0)),
        scratch_shapes=[pltpu.VMEM((2, tm, d), F32), pltpu.VMEM((d, f2), BF16), pltpu.VMEM((f, d), BF16),
                        pltpu.SemaphoreType.DMA((2,))],
    )
    src3 = jnp.concatenate([src, jnp.zeros((tm,), src.dtype)]).reshape(nt + 1, 1, tm)
    return pl.pallas_call(
        functools.partial(_experts_kernel, fc=MOE_FC),
        grid_spec=grid_spec,
        out_shape=jax.ShapeDtypeStruct((r, d), F32),
        compiler_params=_cparams(("arbitrary",)),
        name="moe_experts",
    )(tile_e, tile_first, tile_valid, src3, src3, h, w1, b1.reshape(e, 1, f2), w2, b2.reshape(e, 1, d))


def _combine_kernel(pos_ref, ys_hbm, gate_ref, x_ref, mod_ref, o_ref, buf, sem):
    ct = o_ref.shape[0]

    def issue(r2, carry):
        for u in range(COMBINE_ISSUE_UNROLL):
            r = r2 * COMBINE_ISSUE_UNROLL + u
            for kk in range(TOP_K):
                p = pos_ref[0, 0, r * TOP_K + kk]
                pltpu.make_async_copy(ys_hbm.at[pl.ds(p, 1)], buf.at[kk, pl.ds(r, 1)], sem).start()
        return carry

    lax.fori_loop(0, ct // COMBINE_ISSUE_UNROLL, issue, 0)
    for kk in range(TOP_K):
        pltpu.make_async_copy(ys_hbm.at[pl.ds(0, ct)], buf.at[kk], sem).wait()
    g = gate_ref[...]
    mix = (g[:, 0:1] * buf[0] + g[:, 1:2] * buf[1]) + (g[:, 2:3] * buf[2] + g[:, 3:4] * buf[3])
    o_ref[...] = x_ref[...] + mod_ref[0] * mix


def _combine_rows(ys, pos, gates, x, mod, ct):
    b, n, d = x.shape
    t = b * n
    ct = min(ct, n)
    nsteps = t // ct
    per_batch = n // ct
    out = pl.pallas_call(
        _combine_kernel,
        grid=(nsteps,),
        in_specs=[pl.BlockSpec((1, 1, ct * TOP_K), lambda i: (i, 0, 0), memory_space=pltpu.SMEM),
                  pl.BlockSpec(memory_space=pl.ANY),
                  pl.BlockSpec((ct, V7X_LANES), lambda i: (i, 0)),
                  pl.BlockSpec((ct, d), lambda i: (i, 0)),
                  pl.BlockSpec((1, 1, d), lambda i: (i // per_batch, 0, 0))],
        out_specs=pl.BlockSpec((ct, d), lambda i: (i, 0)),
        out_shape=jax.ShapeDtypeStruct((t, d), F32),
        scratch_shapes=[pltpu.VMEM((TOP_K, ct, d), F32), pltpu.SemaphoreType.DMA],
        compiler_params=_cparams(("arbitrary",)),
        name="moe_combine",
    )(pos.reshape(nsteps, 1, ct * TOP_K), ys, gates, x.reshape(t, d), mod.reshape(b, 1, d))
    return out.reshape(b, n, d)


def _route_plan(idx4, tm):
    t = idx4.shape[0]
    e = N_EXPERTS
    r_max = t * TOP_K + e * tm
    nt = r_max // tm
    onehot = (idx4[:, :, None] == jnp.arange(e, dtype=jnp.int32)[None, None, :]).astype(jnp.int32)
    member = jnp.sum(onehot, axis=1)
    csum = jnp.cumsum(member, axis=0)
    cnt = csum[-1]
    excl = csum - member
    cnt_pad = ((cnt + tm - 1) // tm) * tm
    ends = jnp.cumsum(cnt_pad)
    base = ends - cnt_pad
    pos = jnp.sum(onehot * (excl + base[None, :])[:, None, :], axis=-1)
    flat = pos.reshape(-1)
    tok = jnp.repeat(jnp.arange(t, dtype=jnp.int32), TOP_K)
    src = jnp.zeros((r_max,), jnp.int32).at[flat].set(tok, unique_indices=True)
    tile_start = jnp.arange(nt, dtype=jnp.int32) * tm
    tile_e = jnp.minimum(jnp.sum((ends[None, :] <= tile_start[:, None]).astype(jnp.int32), axis=1), e - 1)
    tile_valid = (tile_start < ends[-1]).astype(jnp.int32)
    tile_first = jnp.concatenate([jnp.ones((1,), jnp.int32), (tile_e[1:] != tile_e[:-1]).astype(jnp.int32)])
    return pos.astype(jnp.int32), src, tile_e, tile_first, tile_valid


def _moe_residual(h2, idx4, gates, w1, b1, w2, b2, layer, streams):
    pos, src, tile_e, tile_first, tile_valid = _route_plan(idx4, MOE_TM)
    ys = _expert_tiles(h2, src, tile_e, tile_first, tile_valid, w1, b1, w2, b2, layer, MOE_TM)
    outs, start = [], 0
    for x, mod in streams:
        cnt = x.shape[0] * x.shape[1]
        outs.append(_combine_rows(ys, pos[start:start + cnt], gates[start:start + cnt], x, mod, COMBINE_ROWS))
        start += cnt
    return outs


def _axial_rope_tables(n_tok, rot_dim):
    t = jnp.arange(n_tok, dtype=jnp.int32)
    row = (t // GRID_W).astype(F32)
    col = (t % GRID_W).astype(F32)
    n_freq = rot_dim // 4
    freqs = ROPE_THETA ** (-jnp.arange(n_freq, dtype=F32) / n_freq)
    ang = jnp.concatenate([row[:, None] * freqs, col[:, None] * freqs], axis=-1)
    return jnp.cos(ang), jnp.sin(ang)


def _rope_lane_tables(cos, sin, start):
    n, nf = cos.shape
    c2 = jnp.repeat(cos, 2, axis=1)
    s2 = jnp.stack([-sin, sin], axis=-1).reshape(n, 2 * nf)
    tail = V7X_LANES - start - 2 * nf
    c = jnp.concatenate([jnp.ones((n, start), F32), c2, jnp.ones((n, tail), F32)], axis=1)
    s = jnp.concatenate([jnp.zeros((n, start), F32), s2, jnp.zeros((n, tail), F32)], axis=1)
    return c, s


def _permute_w_in(w_in):
    offs = (0,) + IN_OFFSETS

    def seg(i):
        return np.arange(offs[i], offs[i] + IN_SIZES[i])

    def head_groups(i, nheads):
        cols = seg(i).reshape(nheads, HEAD_DIM)
        return np.concatenate([cols, -np.ones((nheads, V7X_LANES - HEAD_DIM), np.int64)], axis=1).reshape(-1)

    cols = np.concatenate([head_groups(0, NA_HEADS), head_groups(1, NA_HEADS), head_groups(2, NA_HEADS),
                           head_groups(3, GQA_Q_HEADS), head_groups(4, GQA_KV_HEADS), head_groups(5, GQA_KV_HEADS),
                           seg(6), seg(7), seg(8), seg(9), seg(12), seg(13), seg(14), seg(10), seg(11)])
    cols = np.concatenate([cols, -np.ones(sum(o[0] for o in PROJ_OUTPUTS) - cols.size, np.int64)])
    w = jnp.take(w_in, jnp.asarray(np.maximum(cols, 0)), axis=1)
    return jnp.where(jnp.asarray(cols >= 0)[None, :], w, 0.0).astype(BF16)


def _dn_conv_kernel(prev_ref, cur_ref, next_ref, w_ref, hs_ref, q_ref, k_ref, v_ref):
    j = pl.program_id(1)
    tb = cur_ref.shape[1]
    halo = prev_ref.shape[1]
    prev = jnp.where(j > 0, prev_ref[0], 0.0)
    nxt = jnp.where(j < pl.num_programs(1) - 1, next_ref[0], 0.0)
    win = jnp.concatenate([prev, cur_ref[0], nxt], axis=0)
    acc = None
    for i in range(DN_CONV):
        start = halo + i - DN_CONV // 2
        term = win[start:start + tb, :] * w_ref[i:i + 1, :]
        acc = term if acc is None else acc + term
    act = jax.nn.silu(acc)
    for ref, lo, normed in ((q_ref, 0, True), (k_ref, DN_W, True), (v_ref, 2 * DN_W, False)):
        t = act[:, lo:lo + DN_W]
        if normed:
            ss = jnp.dot(t * t, hs_ref[...], preferred_element_type=F32, precision=HIGHEST)
            t = t * lax.rsqrt(ss + EPS)
        ref[0] = t


def _dn_conv(qkv, conv_w, tb=512):
    b, n, w3 = qkv.shape
    tb = min(tb, n)
    halo = 8
    per_blk = tb // halo
    last = n // halo - 1
    head = np.arange(DN_W) // HEAD_DIM
    head_sum = jnp.asarray((head[:, None] == head[None, :]).astype(np.float32))
    out = pl.BlockSpec((1, tb, DN_W), lambda i, j: (i, j, 0))
    return pl.pallas_call(
        _dn_conv_kernel,
        grid=(b, n // tb),
        in_specs=[pl.BlockSpec((1, halo, w3), lambda i, j: (i, jnp.maximum(j * per_blk - 1, 0), 0)),
                  pl.BlockSpec((1, tb, w3), lambda i, j: (i, j, 0)),
                  pl.BlockSpec((1, halo, w3), lambda i, j: (i, jnp.minimum((j + 1) * per_blk, last), 0)),
                  pl.BlockSpec((8, w3), lambda i, j: (0, 0)),
                  pl.BlockSpec((DN_W, DN_W), lambda i, j: (0, 0))],
        out_specs=[out, out, out],
        out_shape=[jax.ShapeDtypeStruct((b, n, DN_W), F32)] * 3,
        compiler_params=_cparams(("parallel", "parallel")),
        name="delta_conv",
    )(qkv, qkv, qkv, jnp.pad(conv_w, ((0, 8 - DN_CONV), (0, 0))), head_sum)


def _dn_prep(qkv, small, conv_w, a_log, dt_bias):
    q, k, v = _dn_conv(qkv, conv_w)
    beta_raw = small[..., MLA_ROPE:MLA_ROPE + 2 * DN_HEADS]
    a_raw = small[..., MLA_ROPE + 2 * DN_HEADS:MLA_ROPE + 4 * DN_HEADS]
    beta = jax.nn.sigmoid(beta_raw)
    g = -jnp.exp(a_log.reshape(-1)) * jax.nn.softplus(a_raw + dt_bias.reshape(-1))
    lane_pad = ((0, 0), (0, 0), (0, V7X_LANES - 2 * DN_HEADS))
    return q, k, v, jnp.pad(g, lane_pad), jnp.pad(beta, lane_pad)


def _mla_weights(w_uq, w_ukv):
    wq = jnp.pad(w_uq.reshape(MLA_Q_RANK, MLA_HEADS, MLA_QK), ((0, 0), (0, 0), (0, V7X_LANES - MLA_QK)))
    wkv = w_ukv.reshape(MLA_KV_RANK, MLA_HEADS, MLA_NOPE + MLA_V)
    wk = jnp.pad(wkv[..., :MLA_NOPE], ((0, 0), (0, 0), (0, V7X_LANES - MLA_NOPE)))
    wv = jnp.pad(wkv[..., MLA_NOPE:], ((0, 0), (0, 0), (0, V7X_LANES - MLA_V)))
    width = MLA_HEADS * V7X_LANES
    w_kv = jnp.concatenate([wk.reshape(MLA_KV_RANK, width), wv.reshape(MLA_KV_RANK, width)], axis=1)
    return wq.reshape(MLA_Q_RANK, width).astype(BF16), w_kv.astype(BF16)


def kernel(x, c, ctx, c_ctx, ada_w, ada_b, norm1_g, norm2_g, w_in, w_out, na_qn_g, na_kn_g, na_rel_bias, gqa_qn_g, gqa_kn_g, dn_conv_w, dn_a_log, dn_dt_bias, dn_out_g, mla_cq_g, mla_ckv_g, mla_w_uq, mla_w_ukv, mla_qn_g, mla_kn_g, router_w, router_b, exp_w1, exp_b1, exp_w2, exp_b2):
    b, n, d = x.shape
    l = ctx.shape[1]
    depth = ada_w.shape[0]
    rope_g = _rope_lane_tables(*_axial_rope_tables(n, HEAD_DIM), 0)
    rope_m = _rope_lane_tables(*_axial_rope_tables(n, MLA_ROPE), MLA_NOPE)
    cond = jnp.concatenate([jax.nn.silu(c), jax.nn.silu(c_ctx)[None], jnp.zeros((16 - b - 1, d), F32)], axis=0)
    s0 = jnp.zeros((b, DN_W, DN_W), F32)

    for ly in range(depth):
        with_ctx = ly < depth - 1
        mod_all = _mm_bias(cond, ada_w[ly], ada_b[ly], 1024)
        mod = jnp.split(mod_all[:b], N_ADA, axis=-1)
        mod_c = [jnp.broadcast_to(m_, (b, d)) for m_ in jnp.split(mod_all[b:b + 1], N_ADA, axis=-1)]
        w_perm = _permute_w_in(w_in[ly])
        gains = jnp.concatenate([_lane_gain(g_) for g_ in (na_qn_g[ly], na_kn_g[ly], gqa_qn_g[ly], gqa_kn_g[ly])]
                                + [jnp.zeros((4, V7X_LANES), F32)], axis=0)
        p = _in_proj(x, norm1_g[ly], mod[0], mod[1], w_perm, gains, rope_g, PROJ_TM)
        pc = _in_proj(ctx, norm1_g[ly], mod_c[0], mod_c[1], w_perm, gains, None, l)
        bias_tab = _na_bias_table(na_rel_bias[ly], n // GRID_W)

        ya = _na_attention(p[0], p[1], p[2], pc[1], pc[2], bias_tab)

        yb = _attention(p[3], jnp.concatenate([p[4], pc[4]], axis=1), jnp.concatenate([p[5], pc[5]], axis=1),
                        GQA_Q_HEADS, GQA_KV_HEADS, HEAD_DIM)

        qdc, kdc, vdc, gdc, bdc = _dn_prep(pc[6], pc[10], dn_conv_w[ly], dn_a_log[ly], dn_dt_bias[ly])
        odc_f, odc_b, s_f, s_b = _delta_bidir(qdc, kdc, vdc, gdc, bdc, s0, s0)
        qd, kd, vd, gd, bd = _dn_prep(p[6], p[10], dn_conv_w[ly], dn_a_log[ly], dn_dt_bias[ly])
        od_f, od_b, _, _ = _delta_bidir(qd, kd, vd, gd, bd, s_f, s_b)

        w_q, w_kv = _mla_weights(mla_w_uq[ly], mla_w_ukv[ly])
        qm = _mla_q(p[8], mla_cq_g[ly], w_q, mla_qn_g[ly], rope_m)
        km, vm = _mla_kv(p[9], p[10], mla_ckv_g[ly], w_kv, mla_kn_g[ly], rope_m)
        kmc, vmc = _mla_kv(pc[9], pc[10], mla_ckv_g[ly], w_kv, mla_kn_g[ly], None)
        yd = _attention(qm, jnp.concatenate([km, kmc], axis=1), jnp.concatenate([vm, vmc], axis=1),
                        MLA_HEADS, MLA_HEADS, MLA_V)

        rw_pad = jnp.pad(router_w[ly], ((0, 0), (0, V7X_LANES - N_EXPERTS)))
        rb_pad = jnp.pad(router_b[ly], (0, V7X_LANES - N_EXPERTS), constant_values=NEG_INF).reshape(1, V7X_LANES)
        w_out_b = w_out[ly].astype(BF16)
        x, h2, idx, gts = _out_proj(x, ya, yb, od_f, od_b, p[7], dn_out_g[ly], yd, w_out_b, mod[2], norm2_g[ly],
                                    mod[3], mod[4], rw_pad, rb_pad, PROJ_TM)
        tok = h2.reshape(b * n, d)
        idx4 = idx.reshape(b * n, V7X_LANES)[:, :TOP_K]
        gates = gts.reshape(b * n, V7X_LANES)

        if with_ctx:
            yac = _attention(pc[0], pc[1], pc[2], NA_HEADS, NA_HEADS, HEAD_DIM)
            ybc = _attention(pc[3], pc[4], pc[5], GQA_Q_HEADS, GQA_KV_HEADS, HEAD_DIM)
            qmc = _mla_q(pc[8], mla_cq_g[ly], w_q, mla_qn_g[ly], None)
            ydc = _attention(qmc, kmc, vmc, MLA_HEADS, MLA_HEADS, MLA_V)
            ctx, h2c, idxc, gtsc = _out_proj(ctx, yac, ybc, odc_f, odc_b, pc[7], dn_out_g[ly], ydc, w_out_b, mod_c[2],
                                             norm2_g[ly], mod_c[3], mod_c[4], rw_pad, rb_pad, l)
            tok = jnp.concatenate([tok, h2c.reshape(b * l, d)], axis=0)
            idx4 = jnp.concatenate([idx4, idxc.reshape(b * l, V7X_LANES)[:, :TOP_K]], axis=0)
            gates = jnp.concatenate([gates, gtsc.reshape(b * l, V7X_LANES)], axis=0)

        streams = [(x, mod[5])] + ([(ctx, mod_c[5])] if with_ctx else [])
        outs = _moe_residual(tok, idx4, gates, exp_w1, exp_b1, exp_w2, exp_b2, ly, streams)
        x = outs[0]
        if with_ctx:
            ctx = outs[1]
    return x
```
